```python
import math
import jax, jax.numpy as jnp
from jax import lax
import numpy as np

D_MODEL = 1024
BATCH = 2
SEQ = 16384
DEPTH = 1

A_HEADS = 8
A_DK = 128
A_DV = D_MODEL // A_HEADS
A_QK = A_HEADS * A_DK
A_V = A_HEADS * A_DV
B_HEADS = 4
B_DK = (D_MODEL // 2) // B_HEADS
B_DV = D_MODEL // B_HEADS
B_QK = B_HEADS * B_DK
B_V = B_HEADS * B_DV
GLA_RANK = 16
GLA_TAU = 16.0
CHUNK = 64
N_EXPERTS = 32
TOP_K = 4
D_FF = D_MODEL
SWIGLU_LIMIT = 7.0
SWIGLU_ALPHA = 1.702
MOE_BLOCK = 128
PLE_DIM = 256
LN_EPS = 1e-5
RMS_EPS = 1e-6
DN_ALPHA = (2.0 * DEPTH) ** 0.25
DN_BETA = (8.0 * DEPTH) ** -0.25
IN_SIZES = (A_QK, A_QK, A_V, A_V, B_QK, B_QK, B_V, B_V, GLA_RANK, D_MODEL, D_MODEL)
D_IN = sum(IN_SIZES)

kernel_name = 'hybrid_hgrn2_gla_moe_block'


def layer_norm(x, g, b):
    xf = x.astype(jnp.float32)
    mu = jnp.mean(xf, axis=-1, keepdims=True)
    var = jnp.mean(jnp.square(xf - mu), axis=-1, keepdims=True)
    y = (xf - mu) * lax.rsqrt(var + LN_EPS) * g.astype(jnp.float32) + b.astype(jnp.float32)
    return y.astype(x.dtype)


def head_rms_norm(o, g):
    return o * lax.rsqrt(jnp.mean(jnp.square(o), axis=-1, keepdims=True) + RMS_EPS) * g.astype(jnp.float32)


def chunk_gated_linear_attention(q, k, v, log_g):
    bsz, s, h, dk = q.shape
    dv = v.shape[-1]
    n = s // CHUNK

    def to_chunks(t):
        return t.astype(jnp.float32).reshape(bsz, n, CHUNK, h, t.shape[-1]).transpose(1, 0, 3, 2, 4)

    qc, kc, vc, gc = to_chunks(q), to_chunks(k), to_chunks(v), to_chunks(log_g)
    causal = jnp.tril(jnp.ones((CHUNK, CHUNK), dtype=bool))[:, :, None]

    def step(state, inp):
        qi, ki, vi, gi = inp
        b = jnp.cumsum(gi, axis=2)
        o_inter = jnp.einsum('bhcd,bhde->bhce', qi * jnp.exp(b), state)
        rel = b[:, :, :, None, :] - b[:, :, None, :, :]
        decay = jnp.exp(jnp.where(causal, rel, -jnp.inf))
        scores = jnp.einsum('bhid,bhjd,bhijd->bhij', qi, ki, decay)
        o_intra = jnp.einsum('bhij,bhje->bhie', scores, vi)
        b_last = b[:, :, -1:, :]
        k_dec = ki * jnp.exp(b_last - b)
        state = jnp.exp(b_last[:, :, 0, :])[..., None] * state + jnp.einsum('bhjd,bhje->bhde', k_dec, vi)
        return state, o_inter + o_intra

    state0 = jnp.zeros((bsz, h, dk, dv), jnp.float32)
    _, out = lax.scan(step, state0, (qc, kc, vc, gc))
    return out.transpose(1, 0, 3, 2, 4).reshape(bsz, s, h, dv)


def hybrid_mixer(x, lb, w_in, w_gla_up, b_gla_up, norm_a_g, norm_b_g, w_proj_a, w_proj_b, w_out):
    bsz, s, _ = x.shape
    f32 = jnp.float32
    offsets = [int(o) for o in np.cumsum(IN_SIZES)[:-1]]
    z = x @ w_in
    q_a, f_a, i_a, g_a, q_b, k_b, v_b, r_b, lr_b, gate_a, gate_b = jnp.split(z, offsets, axis=-1)

    fz = f_a.astype(f32)
    forget = lb + (1.0 - lb) * jax.nn.sigmoid(fz)
    k_a = (1.0 - lb) * jax.nn.sigmoid(-fz)
    qa = jax.nn.silu(q_a.astype(f32)) * (A_DK ** -0.5)
    o_a = chunk_gated_linear_attention(
        qa.reshape(bsz, s, A_HEADS, A_DK), k_a.reshape(bsz, s, A_HEADS, A_DK),
        i_a.reshape(bsz, s, A_HEADS, A_DV), jnp.log(forget).reshape(bsz, s, A_HEADS, A_DK))
    o_a = head_rms_norm(o_a, norm_a_g).reshape(bsz, s, A_V) * jax.nn.silu(g_a.astype(f32))

    log_g = jax.nn.log_sigmoid((lr_b @ w_gla_up + b_gla_up).astype(f32)) / GLA_TAU
    qb = q_b.astype(f32) * (B_DK ** -0.5)
    o_b = chunk_gated_linear_attention(
        qb.reshape(bsz, s, B_HEADS, B_DK), k_b.reshape(bsz, s, B_HEADS, B_DK),
        v_b.reshape(bsz, s, B_HEADS, B_DV), log_g.reshape(bsz, s, B_HEADS, B_DK))
    o_b = head_rms_norm(o_b, norm_b_g).reshape(bsz, s, B_V) * jax.nn.silu(r_b.astype(f32))

    ya = o_a.astype(x.dtype) @ w_proj_a
    yb = o_b.astype(x.dtype) @ w_proj_b
    merged = jax.nn.sigmoid(gate_a) * ya + jax.nn.sigmoid(gate_b) * yb
    return merged @ w_out


def moe_ffn(x, w_router, b_router, w_gate_up, b_gate_up, w_down, b_down):
    bsz, s, d = x.shape
    t = bsz * s
    xt = x.reshape(t, d)
    logits = (xt @ w_router).astype(jnp.float32) + b_router.astype(jnp.float32)
    top_logit, top_e = lax.top_k(logits, TOP_K)
    top_w = jax.nn.softmax(top_logit, axis=-1)
    n_assign = t * TOP_K
    flat_e = top_e.reshape(-1)
    flat_tok = jnp.arange(n_assign, dtype=jnp.int32) // TOP_K
    flat_w = top_w.reshape(-1)
    order = jnp.argsort(flat_e)
    sorted_e = flat_e[order]
    counts = jnp.bincount(flat_e, length=N_EXPERTS)
    padded = (counts + MOE_BLOCK - 1) // MOE_BLOCK * MOE_BLOCK
    start = jnp.cumsum(counts) - counts
    pend = jnp.cumsum(padded)
    pstart = pend - padded
    rank = jnp.arange(n_assign, dtype=jnp.int32) - start[sorted_e]
    dest = pstart[sorted_e] + rank
    n_blocks = (n_assign + MOE_BLOCK - 1) // MOE_BLOCK + N_EXPERTS
    n_rows = n_blocks * MOE_BLOCK
    row_tok = jnp.full((n_rows,), t, jnp.int32).at[dest].set(flat_tok[order])
    row_w = jnp.zeros((n_rows,), jnp.float32).at[dest].set(flat_w[order])
    block_e = jnp.minimum(
        jnp.searchsorted(pend, jnp.arange(n_blocks, dtype=jnp.int32) * MOE_BLOCK, side='right'),
        N_EXPERTS - 1)
    x_pad = jnp.concatenate([xt, jnp.zeros((1, d), xt.dtype)], axis=0)

    def expert_block(args):
        toks, e = args
        xb = x_pad[toks]
        hgu = xb @ w_gate_up[e] + b_gate_up[e]
        gate = jnp.minimum(hgu[:, :D_FF], SWIGLU_LIMIT)
        up = jnp.clip(hgu[:, D_FF:], -SWIGLU_LIMIT, SWIGLU_LIMIT)
        act = (up + 1.0) * gate * jax.nn.sigmoid(SWIGLU_ALPHA * gate)
        return act @ w_down[e] + b_down[e]

    out = lax.map(expert_block, (row_tok.reshape(n_blocks, MOE_BLOCK), block_e))
    out = out.reshape(n_rows, d) * row_w[:, None].astype(out.dtype)
    y = jax.ops.segment_sum(out, row_tok, num_segments=t + 1)[:t]
    return y.reshape(bsz, s, d)


def setup_inputs(seed: int = 0) -> dict:
    key = jax.random.key(seed)
    ks = jax.random.split(key, 32)
    f32 = jnp.float32
    L = DEPTH
    D = D_MODEL

    def nrm(k, shape, scale):
        return jax.random.normal(k, shape, f32) * scale

    col_scale = jnp.concatenate([
        jnp.full((sz,), DN_BETA if idx in (2, 6) else 1.0, f32) for idx, sz in enumerate(IN_SIZES)])
    return {
        'x': nrm(ks[0], (BATCH, SEQ, D), 1.0),
        'p': nrm(ks[1], (DEPTH, BATCH, SEQ, PLE_DIM), 1.0),
        'emb_ln_g': 1.0 + nrm(ks[2], (D,), 0.01),
        'emb_ln_b': nrm(ks[3], (D,), 0.01),
        'hgrn_lb': nrm(ks[4], (DEPTH + 1, A_QK), 0.1),
        'w_in': nrm(ks[5], (L, D, D_IN), D ** -0.5) * col_scale,
        'w_gla_up': nrm(ks[6], (L, GLA_RANK, B_QK), GLA_RANK ** -0.5),
        'b_gla_up': nrm(ks[7], (L, B_QK), 0.1),
        'norm_a_g': 1.0 + nrm(ks[8], (L, A_DV), 0.01),
        'norm_b_g': 1.0 + nrm(ks[9], (L, B_DV), 0.01),
        'w_proj_a': nrm(ks[10], (L, A_V, D), A_V ** -0.5 * DN_BETA),
        'w_proj_b': nrm(ks[11], (L, B_V, D), B_V ** -0.5 * DN_BETA),
        'w_out': nrm(ks[12], (L, D, D), D ** -0.5 * DN_BETA),
        'ln_mix_g': 1.0 + nrm(ks[13], (L, D), 0.01),
        'ln_mix_b': nrm(ks[14], (L, D), 0.01),
        'w_router': nrm(ks[15], (L, D, N_EXPERTS), D ** -0.5),
        'b_router': nrm(ks[16], (L, N_EXPERTS), 0.01),
        'w_gate_up': nrm(ks[17], (L, N_EXPERTS, D, 2 * D_FF), D ** -0.5),
        'b_gate_up': nrm(ks[18], (L, N_EXPERTS, 2 * D_FF), 0.01),
        'w_down': nrm(ks[19], (L, N_EXPERTS, D_FF, D), D_FF ** -0.5 * DN_BETA),
        'b_down': nrm(ks[20], (L, N_EXPERTS, D), 0.01),
        'w_ple_gate': nrm(ks[21], (L, D, D), D ** -0.5),
        'w_ple_proj': nrm(ks[22], (L, PLE_DIM, D), PLE_DIM ** -0.5 * DN_BETA),
        'ln_moe_g': 1.0 + nrm(ks[23], (L, D), 0.01),
        'ln_moe_b': nrm(ks[24], (L, D), 0.01),
    }


def reference(x, p, emb_ln_g, emb_ln_b, hgrn_lb, w_in, w_gla_up, b_gla_up, norm_a_g, norm_b_g,
              w_proj_a, w_proj_b, w_out, ln_mix_g, ln_mix_b, w_router, b_router, w_gate_up,
              b_gate_up, w_down, b_down, w_ple_gate, w_ple_proj, ln_moe_g, ln_moe_b):
    h = layer_norm(x, emb_ln_g, emb_ln_b)
    lb_table = jnp.cumsum(jax.nn.softmax(hgrn_lb.astype(jnp.float32), axis=0), axis=0)
    for i in range(DEPTH):
        y = hybrid_mixer(h, lb_table[i], w_in[i], w_gla_up[i], b_gla_up[i], norm_a_g[i], norm_b_g[i],
                         w_proj_a[i], w_proj_b[i], w_out[i])
        h = layer_norm(DN_ALPHA * h + y, ln_mix_g[i], ln_mix_b[i])
        y = moe_ffn(h, w_router[i], b_router[i], w_gate_up[i], b_gate_up[i], w_down[i], b_down[i])
        ple = jax.nn.sigmoid(h @ w_ple_gate[i]) * (p[i].astype(h.dtype) @ w_ple_proj[i])
        h = layer_norm(DN_ALPHA * h + y + ple, ln_moe_g[i], ln_moe_b[i])
    return h
```

```python
import functools

import jax
import jax.numpy as jnp
from jax import lax
from jax.experimental import pallas as pl
from jax.experimental.pallas import tpu as pltpu

F32 = jnp.float32
BF16 = jnp.bfloat16

D_MODEL = 1024
A_HEADS, A_DK, A_DV = 8, 128, 128
B_HEADS, B_DK, B_DV = 4, 128, 256
GLA_RANK = 16
GLA_TAU = 16.0
N_EXPERTS = 32
TOP_K = 4
D_FF = 1024
SWIGLU_LIMIT = 7.0
SWIGLU_ALPHA = 1.702
PLE_DIM = 256
LN_EPS = 1e-5
RMS_EPS = 1e-6
DEPTH = 1
DN_ALPHA = (2.0 * DEPTH) ** 0.25

LANES = 128
ROW_TILE = (8, LANES)
GLA_CHUNK = 64
GLA_TILE = 256
GLA_SAFE_LOG_DECAY = -60.0
MOE_ROWS = 256
DMA_BATCH = 128
VMEM_LIMIT = 56 * 1024 * 1024

Z_QA, Z_FA, Z_IA, Z_GA = 0, 1024, 2048, 3072
Z_QB, Z_KB, Z_VB, Z_RB = 4096, 4608, 5120, 6144
Z_GATE_A, Z_GATE_B = 7168, 8192
Z_COLS = 9216
W_IN_LR = 7168


def _layer_norm(x, g, b):
    mu = jnp.mean(x, axis=-1, keepdims=True)
    xc = x - mu
    var = jnp.mean(xc * xc, axis=-1, keepdims=True)
    return xc * lax.rsqrt(var + LN_EPS) * g + b


def _sigmoid(x):
    return jax.nn.sigmoid(x)


def _dot(a, b):
    return jnp.dot(a, b, preferred_element_type=F32)


def _dot_nt(a, b):
    return lax.dot_general(a, b, (((1,), (1,)), ((), ())), preferred_element_type=F32)


def _dot_tn(a, b):
    return lax.dot_general(a, b, (((0,), (0,)), ((), ())), preferred_element_type=F32)


def _ln_inproj_kernel(x_ref, g_ref, b_ref, w_ref, wlr_ref, h_ref, z_ref, zlr_ref, hb_ref):
    @pl.when(pl.program_id(1) == 0)
    def _():
        h = _layer_norm(x_ref[...], g_ref[...], b_ref[...])
        h_ref[...] = h
        hb = h.astype(BF16)
        hb_ref[...] = hb
        zlr_ref[...] = _dot(hb, wlr_ref[...])

    z_ref[...] = _dot(hb_ref[...], w_ref[...])


def _ln_inproj(x, g, b, w_main, w_lr, tm=1024, tn=1024):
    t = x.shape[0]
    return pl.pallas_call(
        _ln_inproj_kernel,
        grid=(t // tm, Z_COLS // tn),
        in_specs=[
            pl.BlockSpec((tm, D_MODEL), lambda i, j: (i, 0)),
            pl.BlockSpec((1, D_MODEL), lambda i, j: (0, 0)),
            pl.BlockSpec((1, D_MODEL), lambda i, j: (0, 0)),
            pl.BlockSpec((D_MODEL, tn), lambda i, j: (0, j)),
            pl.BlockSpec((D_MODEL, LANES), lambda i, j: (0, 0)),
        ],
        out_specs=[
            pl.BlockSpec((tm, D_MODEL), lambda i, j: (i, 0)),
            pl.BlockSpec((tm, tn), lambda i, j: (i, j)),
            pl.BlockSpec((tm, LANES), lambda i, j: (i, 0)),
        ],
        out_shape=[
            jax.ShapeDtypeStruct((t, D_MODEL), F32),
            jax.ShapeDtypeStruct((t, Z_COLS), F32),
            jax.ShapeDtypeStruct((t, LANES), F32),
        ],
        scratch_shapes=[pltpu.VMEM((tm, D_MODEL), BF16)],
        compiler_params=pltpu.CompilerParams(
            dimension_semantics=("parallel", "arbitrary"), vmem_limit_bytes=VMEM_LIMIT),
        name="ln_inproj",
    )(x, g, b, w_main, w_lr)


def _chunk_cumsum(x, chunk):
    rows = lax.broadcasted_iota(jnp.int32, x.shape, 0) % chunk
    s = 1
    while s < chunk:
        x = x + jnp.where(rows >= s, pltpu.roll(x, s, 0), 0.0)
        s *= 2
    return x


def _gla_tile(q, k, v, log_g, st_ref, sc_ref, ks_ref, bs_ref):
    ts = q.shape[0]
    c = GLA_CHUNK
    n = ts // c
    b = _chunk_cumsum(log_g, c)
    q1 = (q * jnp.exp(b)).astype(BF16)
    col = lax.broadcasted_iota(jnp.int32, (c, c), 1)
    causal = lax.broadcasted_iota(jnp.int32, (c, c), 0) >= col
    safe = jnp.min(b) >= GLA_SAFE_LOG_DECAY

    @pl.when(safe)
    def _product_form():
        k1 = (k * jnp.exp(-b)).astype(BF16)
        for ci in range(n):
            sl = slice(ci * c, (ci + 1) * c)
            sc_ref[ci] = _dot_nt(q1[sl], k1[sl])

    @pl.when(jnp.logical_not(safe))
    def _exact_form():
        ks_ref[...] = k
        bs_ref[...] = b
        for ci in range(n):
            sl = slice(ci * c, (ci + 1) * c)
            qc, bc = q[sl], b[sl]

            def column(j, acc):
                kj = ks_ref[pl.ds(ci * c + j, 1), :]
                bj = bs_ref[pl.ds(ci * c + j, 1), :]
                term = qc * kj * jnp.exp(jnp.minimum(bc - bj, 0.0))
                return jnp.where(col == j, jnp.sum(term, axis=-1, keepdims=True), acc)

            sc_ref[ci] = lax.fori_loop(0, c, column, jnp.zeros((c, c), F32))

    st = st_ref[...]
    outs = []
    for ci in range(n):
        sl = slice(ci * c, (ci + 1) * c)
        bc = b[sl]
        b_last = bc[c - 1:c]
        k_dec = (k[sl] * jnp.exp(b_last - bc)).astype(BF16)
        vb = v[sl].astype(BF16)
        scores = jnp.where(causal, sc_ref[ci], 0.0).astype(BF16)
        outs.append(_dot_nt(q1[sl], st.astype(BF16)) + _dot(scores, vb))
        st = st * jnp.exp(b_last) + _dot_tn(vb, k_dec)
    st_ref[...] = st
    return outs


def _head_norm_gate_store(outs, norm_g, gate, o_ref):
    c = GLA_CHUNK
    for ci, o in enumerate(outs):
        sl = slice(ci * c, (ci + 1) * c)
        on = o * lax.rsqrt(jnp.mean(o * o, axis=-1, keepdims=True) + RMS_EPS) * norm_g
        gc = gate[sl]
        o_ref[sl, :] = (on * (gc * _sigmoid(gc))).astype(o_ref.dtype)


def _gla_a_kernel(q_ref, f_ref, i_ref, g_ref, lb_ref, ng_ref, o_ref, st_ref, sc_ref, ks_ref, bs_ref):
    @pl.when(pl.program_id(2) == 0)
    def _():
        st_ref[...] = jnp.zeros_like(st_ref)

    lb = lb_ref[...]
    fz = f_ref[...]
    forget = lb + (1.0 - lb) * _sigmoid(fz)
    k = (1.0 - lb) * _sigmoid(-fz)
    qz = q_ref[...]
    q = qz * _sigmoid(qz) * (A_DK ** -0.5)
    outs = _gla_tile(q, k, i_ref[...], jnp.log(forget), st_ref, sc_ref, ks_ref, bs_ref)
    _head_norm_gate_store(outs, ng_ref[...], g_ref[...], o_ref)


def _gla_b_kernel(q_ref, k_ref, v_ref, r_ref, lr_ref, wup_ref, bup_ref, ng_ref, o_ref,
                  st_ref, sc_ref, ks_ref, bs_ref):
    @pl.when(pl.program_id(2) == 0)
    def _():
        st_ref[...] = jnp.zeros_like(st_ref)

    u = _dot(lr_ref[...].astype(BF16), wup_ref[...]) + bup_ref[...]
    log_sig = jnp.minimum(u, 0.0) - jnp.log(1.0 + jnp.exp(-jnp.abs(u)))
    q = q_ref[...] * (B_DK ** -0.5)
    outs = _gla_tile(q, k_ref[...], v_ref[...], log_sig / GLA_TAU, st_ref, sc_ref, ks_ref, bs_ref)
    _head_norm_gate_store(outs, ng_ref[...], r_ref[...], o_ref)


def _gla_scratch(dk, dv):
    return [
        pltpu.VMEM((dv, dk), F32),
        pltpu.VMEM((GLA_TILE // GLA_CHUNK, GLA_CHUNK, GLA_CHUNK), F32),
        pltpu.VMEM((GLA_TILE, dk), F32),
        pltpu.VMEM((GLA_TILE, dk), F32),
    ]


def _gla_a(z, lb, norm_g, bsz, seq):
    ts = GLA_TILE
    nt = seq // ts

    def zcol(off):
        return pl.BlockSpec((ts, A_DK), lambda b, h, t: (b * nt + t, off // A_DK + h))

    return pl.pallas_call(
        _gla_a_kernel,
        grid=(bsz, A_HEADS, nt),
        in_specs=[
            zcol(Z_QA), zcol(Z_FA), zcol(Z_IA), zcol(Z_GA),
            pl.BlockSpec((1, A_DK), lambda b, h, t: (0, h)),
            pl.BlockSpec((1, A_DV), lambda b, h, t: (0, 0)),
        ],
        out_specs=pl.BlockSpec((ts, A_DV), lambda b, h, t: (b * nt + t, h)),
        out_shape=jax.ShapeDtypeStruct((bsz * seq, A_HEADS * A_DV), BF16),
        scratch_shapes=_gla_scratch(A_DK, A_DV),
        compiler_params=pltpu.CompilerParams(
            dimension_semantics=("parallel", "parallel", "arbitrary"), vmem_limit_bytes=VMEM_LIMIT),
        name="gla_a",
    )(z, z, z, z, lb, norm_g)


def _gla_b(z, z_lr, w_up, b_up, norm_g, bsz, seq):
    ts = GLA_TILE
    nt = seq // ts

    def zcol(off, width):
        return pl.BlockSpec((ts, width), lambda b, h, t: (b * nt + t, off // width + h))

    return pl.pallas_call(
        _gla_b_kernel,
        grid=(bsz, B_HEADS, nt),
        in_specs=[
            zcol(Z_QB, B_DK), zcol(Z_KB, B_DK), zcol(Z_VB, B_DV), zcol(Z_RB, B_DV),
            pl.BlockSpec((ts, LANES), lambda b, h, t: (b * nt + t, 0)),
            pl.BlockSpec((LANES, B_DK), lambda b, h, t: (0, h)),
            pl.BlockSpec((1, B_DK), lambda b, h, t: (0, h)),
            pl.BlockSpec((1, B_DV), lambda b, h, t: (0, 0)),
        ],
        out_specs=pl.BlockSpec((ts, B_DV), lambda b, h, t: (b * nt + t, h)),
        out_shape=jax.ShapeDtypeStruct((bsz * seq, B_HEADS * B_DV), BF16),
        scratch_shapes=_gla_scratch(B_DK, B_DV),
        compiler_params=pltpu.CompilerParams(
            dimension_semantics=("parallel", "parallel", "arbitrary"), vmem_limit_bytes=VMEM_LIMIT),
        name="gla_b",
    )(z, z, z, z, z_lr, w_up, b_up, norm_g)


def _lane_pack(cols, dtype):
    m = cols[0].shape[0]
    lane = lax.broadcasted_iota(jnp.int32, (m, LANES), 1)
    out = jnp.zeros((m, LANES), dtype)
    for k, cvals in enumerate(cols):
        out = jnp.where(lane == k, cvals.astype(dtype), out)
    return out


def _post_mixer_kernel(oa_ref, ob_ref, ga_ref, gb_ref, h0_ref, wa_ref, wb_ref, wo_ref, lg_ref, lb_ref,
                       wrh_ref, wrl_ref, br_ref, h1_ref, te_ref, tw_ref):
    ya = _dot(oa_ref[...], wa_ref[...])
    yb = _dot(ob_ref[...], wb_ref[...])
    merged = _sigmoid(ga_ref[...]) * ya + _sigmoid(gb_ref[...]) * yb
    y = _dot(merged.astype(BF16), wo_ref[...])
    h1 = _layer_norm(DN_ALPHA * h0_ref[...] + y, lg_ref[...], lb_ref[...])
    h1_ref[...] = h1

    hh = h1.astype(BF16)
    hl = (h1 - hh.astype(F32)).astype(BF16)
    logits = _dot(hh, wrh_ref[...]) + (_dot(hl, wrh_ref[...]) + _dot(hh, wrl_ref[...])) + br_ref[...]
    lane = lax.broadcasted_iota(jnp.int32, logits.shape, 1)
    work = jnp.where(lane < N_EXPERTS, logits, -jnp.inf)
    vals, idxs = [], []
    for _ in range(TOP_K):
        m = jnp.max(work, axis=-1, keepdims=True)
        idx = jnp.min(jnp.where(work == m, lane, LANES), axis=-1, keepdims=True)
        vals.append(m)
        idxs.append(idx)
        work = jnp.where(lane == idx, -jnp.inf, work)
    exps = [jnp.exp(v - vals[0]) for v in vals]
    denom = exps[0] + exps[1] + exps[2] + exps[3]
    te_ref[...] = _lane_pack(idxs, jnp.int32)
    tw_ref[...] = _lane_pack([e / denom for e in exps], F32)


def _post_mixer(o_a, o_b, z, h0, w_a, w_b, w_o, ln_g, ln_b, wr_hi, wr_lo, b_r, tm=256):
    t = h0.shape[0]
    row = lambda i: (i, 0)
    const = lambda i: (0, 0)
    full_w = pl.BlockSpec((D_MODEL, D_MODEL), const)
    vec = pl.BlockSpec((1, D_MODEL), const)
    return pl.pallas_call(
        _post_mixer_kernel,
        grid=(t // tm,),
        in_specs=[
            pl.BlockSpec((tm, D_MODEL), row),
            pl.BlockSpec((tm, D_MODEL), row),
            pl.BlockSpec((tm, D_MODEL), lambda i: (i, Z_GATE_A // D_MODEL)),
            pl.BlockSpec((tm, D_MODEL), lambda i: (i, Z_GATE_B // D_MODEL)),
            pl.BlockSpec((tm, D_MODEL), row),
            full_w, full_w, full_w, vec, vec,
            pl.BlockSpec((D_MODEL, LANES), const),
            pl.BlockSpec((D_MODEL, LANES), const),
            pl.BlockSpec((1, LANES), const),
        ],
        out_specs=[
            pl.BlockSpec((tm, D_MODEL), row),
            pl.BlockSpec((tm, LANES), row),
            pl.BlockSpec((tm, LANES), row),
        ],
        out_shape=[
            jax.ShapeDtypeStruct((t, D_MODEL), F32),
            jax.ShapeDtypeStruct((t, LANES), jnp.int32),
            jax.ShapeDtypeStruct((t, LANES), F32),
        ],
        compiler_params=pltpu.CompilerParams(
            dimension_semantics=("parallel",), vmem_limit_bytes=VMEM_LIMIT),
        name="post_mixer",
    )(o_a, o_b, z, z, h0, w_a, w_b, w_o, ln_g, ln_b, wr_hi, wr_lo, b_r)


def _rank_kernel(te_ref, rank_ref, cnt_ref, carry_ref):
    @pl.when(pl.program_id(0) == 0)
    def _():
        carry_ref[...] = jnp.zeros_like(carry_ref)

    te = te_ref[...]
    tr = te.shape[0]
    lane = lax.broadcasted_iota(jnp.int32, (tr, LANES), 1)
    onehots = [te[:, k:k + 1] == lane for k in range(TOP_K)]
    cnt = jnp.zeros((tr, LANES), F32)
    for oh in onehots:
        cnt = cnt + oh.astype(F32)
    strict_lower = (lax.broadcasted_iota(jnp.int32, (tr, tr), 0)
                    > lax.broadcasted_iota(jnp.int32, (tr, tr), 1))
    earlier = _dot(strict_lower.astype(BF16), cnt.astype(BF16)) + carry_ref[...]
    ranks = [jnp.sum(jnp.where(oh, earlier, 0.0), axis=-1, keepdims=True) for oh in onehots]
    rank_ref[...] = _lane_pack(ranks, jnp.int32)
    carry_ref[...] = carry_ref[...] + jnp.sum(cnt, axis=0, keepdims=True)
    cnt_ref[...] = carry_ref[...]


def _rank(top_e, tr=512):
    t = top_e.shape[0]
    return pl.pallas_call(
        _rank_kernel,
        grid=(t // tr,),
        in_specs=[pl.BlockSpec((tr, LANES), lambda i: (i, 0))],
        out_specs=[
            pl.BlockSpec((tr, LANES), lambda i: (i, 0)),
            pl.BlockSpec((1, LANES), lambda i: (0, 0)),
        ],
        out_shape=[
            jax.ShapeDtypeStruct((t, LANES), jnp.int32),
            jax.ShapeDtypeStruct((1, LANES), F32),
        ],
        scratch_shapes=[pltpu.VMEM((1, LANES), F32)],
        compiler_params=pltpu.CompilerParams(dimension_semantics=("arbitrary",)),
        name="rank",
    )(top_e)


def _row_copy(src_hbm, src_row, dst_hbm, dst_row, sem):
    return pltpu.make_async_copy(src_hbm.at[src_row], dst_hbm.at[dst_row], sem)


def _wait_rows(ref_hbm, n, sem):
    pltpu.make_async_copy(ref_hbm.at[pl.ds(0, n)], ref_hbm.at[pl.ds(0, n)], sem).wait()


def _dispatch_kernel(dest_ref, h_hbm, xs_init_hbm, xs_hbm, sem):
    del xs_init_hbm
    n_tok = dest_ref.shape[0] // TOP_K
    base = pl.program_id(0) * n_tok
    tok_per_batch = DMA_BATCH // TOP_K
    n_batches = n_tok // tok_per_batch
    for bi in range(n_batches):
        def issue(r, carry):
            for k in range(TOP_K):
                _row_copy(h_hbm, base + r, xs_hbm, dest_ref[r * TOP_K + k], sem).start()
            return carry
        lax.fori_loop(bi * tok_per_batch, (bi + 1) * tok_per_batch, issue, 0)
        if bi > 0:
            _wait_rows(xs_hbm, DMA_BATCH, sem)
    _wait_rows(xs_hbm, DMA_BATCH, sem)


def _dispatch(dest_flat, h_rows, n_rows, tok_per_step=256):
    t = h_rows.shape[0]
    xs_init = jnp.zeros((n_rows,) + ROW_TILE, F32)
    return pl.pallas_call(
        _dispatch_kernel,
        grid=(t // tok_per_step,),
        in_specs=[
            pl.BlockSpec((tok_per_step * TOP_K,), lambda i: (i,), memory_space=pltpu.SMEM),
            pl.BlockSpec(memory_space=pl.ANY),
            pl.BlockSpec(memory_space=pl.ANY),
        ],
        out_specs=pl.BlockSpec(memory_space=pl.ANY),
        out_shape=jax.ShapeDtypeStruct((n_rows,) + ROW_TILE, F32),
        scratch_shapes=[pltpu.SemaphoreType.DMA(())],
        input_output_aliases={2: 0},
        compiler_params=pltpu.CompilerParams(dimension_semantics=("arbitrary",)),
        name="dispatch",
    )(dest_flat, h_rows, xs_init)


def _combine_kernel(dest_ref, ys_hbm, g_hbm, sem):
    n_tok = dest_ref.shape[0] // TOP_K
    base = pl.program_id(0) * n_tok
    tok_per_batch = DMA_BATCH // TOP_K
    n_batches = n_tok // tok_per_batch
    for bi in range(n_batches):
        def issue(r, carry):
            for k in range(TOP_K):
                _row_copy(ys_hbm, dest_ref[r * TOP_K + k], g_hbm.at[k], base + r, sem).start()
            return carry
        lax.fori_loop(bi * tok_per_batch, (bi + 1) * tok_per_batch, issue, 0)
        if bi > 0:
            _wait_rows(ys_hbm, DMA_BATCH, sem)
    _wait_rows(ys_hbm, DMA_BATCH, sem)


def _combine(dest_flat, ys_rows, t, tok_per_step=256):
    return pl.pallas_call(
        _combine_kernel,
        grid=(t // tok_per_step,),
        in_specs=[
            pl.BlockSpec((tok_per_step * TOP_K,), lambda i: (i,), memory_space=pltpu.SMEM),
            pl.BlockSpec(memory_space=pl.ANY),
        ],
        out_specs=pl.BlockSpec(memory_space=pl.ANY),
        out_shape=jax.ShapeDtypeStruct((TOP_K, t) + ROW_TILE, F32),
        scratch_shapes=[pltpu.SemaphoreType.DMA(())],
        compiler_params=pltpu.CompilerParams(dimension_semantics=("arbitrary",)),
        name="combine",
    )(dest_flat, ys_rows)


def _experts_kernel(be_ref, x_ref, wgu_ref, bgu_ref, wd_ref, bd_ref, y_ref):
    del be_ref
    hgu = _dot(x_ref[...].astype(BF16), wgu_ref[0]) + bgu_ref[0]
    gate = jnp.minimum(hgu[:, :D_FF], SWIGLU_LIMIT)
    up = jnp.clip(hgu[:, D_FF:], -SWIGLU_LIMIT, SWIGLU_LIMIT)
    act = (up + 1.0) * gate * _sigmoid(SWIGLU_ALPHA * gate)
    y_ref[...] = _dot(act.astype(BF16), wd_ref[0]) + bd_ref[0]


def _experts(block_e, xs, w_gu, b_gu, w_d, b_d):
    n_rows = xs.shape[0]
    n_blocks = n_rows // MOE_ROWS
    grid_spec = pltpu.PrefetchScalarGridSpec(
        num_scalar_prefetch=1,
        grid=(n_blocks,),
        in_specs=[
            pl.BlockSpec((MOE_ROWS, D_MODEL), lambda i, be: (i, 0)),
            pl.BlockSpec((1, D_MODEL, 2 * D_FF), lambda i, be: (be[i], 0, 0)),
            pl.BlockSpec((1, 1, 2 * D_FF), lambda i, be: (be[i], 0, 0)),
            pl.BlockSpec((1, D_FF, D_MODEL), lambda i, be: (be[i], 0, 0)),
            pl.BlockSpec((1, 1, D_MODEL), lambda i, be: (be[i], 0, 0)),
        ],
        out_specs=pl.BlockSpec((MOE_ROWS, D_MODEL), lambda i, be: (i, 0)),
    )
    return pl.pallas_call(
        _experts_kernel,
        grid_spec=grid_spec,
        out_shape=jax.ShapeDtypeStruct((n_rows, D_MODEL), F32),
        compiler_params=pltpu.CompilerParams(
            dimension_semantics=("arbitrary",), vmem_limit_bytes=VMEM_LIMIT),
        name="experts",
    )(block_e, xs, w_gu, b_gu, w_d, b_d)


def _final_kernel(g_ref, tw_ref, h1_ref, p_ref, wpg_ref, wpp_ref, lg_ref, lb_ref, o_ref):
    h1 = h1_ref[...]
    tw = tw_ref[...]
    y = tw[:, 0:1] * g_ref[0]
    for k in range(1, TOP_K):
        y = y + tw[:, k:k + 1] * g_ref[k]
    ple = _sigmoid(_dot(h1.astype(BF16), wpg_ref[...])) * _dot(p_ref[...].astype(BF16), wpp_ref[...])
    o_ref[...] = _layer_norm(DN_ALPHA * h1 + y + ple, lg_ref[...], lb_ref[...])


def _final(g, top_w, h1, p, w_pg, w_pp, ln_g, ln_b, tm=256):
    t = h1.shape[0]
    row = lambda i: (i, 0)
    const = lambda i: (0, 0)
    return pl.pallas_call(
        _final_kernel,
        grid=(t // tm,),
        in_specs=[
            pl.BlockSpec((TOP_K, tm, D_MODEL), lambda i: (0, i, 0)),
            pl.BlockSpec((tm, LANES), row),
            pl.BlockSpec((tm, D_MODEL), row),
            pl.BlockSpec((tm, PLE_DIM), row),
            pl.BlockSpec((D_MODEL, D_MODEL), const),
            pl.BlockSpec((PLE_DIM, D_MODEL), const),
            pl.BlockSpec((1, D_MODEL), const),
            pl.BlockSpec((1, D_MODEL), const),
        ],
        out_specs=pl.BlockSpec((tm, D_MODEL), row),
        out_shape=jax.ShapeDtypeStruct((t, D_MODEL), F32),
        compiler_params=pltpu.CompilerParams(
            dimension_semantics=("parallel",), vmem_limit_bytes=VMEM_LIMIT),
        name="final",
    )(g, top_w, h1, p, w_pg, w_pp, ln_g, ln_b)


def kernel(x, p, emb_ln_g, emb_ln_b, hgrn_lb, w_in, w_gla_up, b_gla_up, norm_a_g, norm_b_g, w_proj_a, w_proj_b, w_out, ln_mix_g, ln_mix_b, w_router, b_router, w_gate_up, b_gate_up, w_down, b_down, w_ple_gate, w_ple_proj, ln_moe_g, ln_moe_b):
    bsz, seq, d = x.shape
    t = bsz * seq
    assert d == D_MODEL and seq % GLA_TILE == 0 and t % 1024 == 0
    assert w_in.shape[0] == DEPTH == 1
    vec = lambda a: a.reshape(1, -1).astype(F32)

    lb0 = jax.nn.softmax(hgrn_lb.astype(F32), axis=0)[0:1]
    w_in0 = w_in[0]
    w_main = jnp.concatenate([w_in0[:, :W_IN_LR], w_in0[:, W_IN_LR + GLA_RANK:]], axis=1).astype(BF16)
    w_lr = jnp.pad(w_in0[:, W_IN_LR:W_IN_LR + GLA_RANK], ((0, 0), (0, LANES - GLA_RANK))).astype(BF16)
    w_up = jnp.pad(w_gla_up[0], ((0, LANES - GLA_RANK), (0, 0))).astype(BF16)
    wr = jnp.pad(w_router[0].astype(F32), ((0, 0), (0, LANES - N_EXPERTS)))
    wr_hi = wr.astype(BF16)
    wr_lo = (wr - wr_hi.astype(F32)).astype(BF16)
    b_r = jnp.pad(b_router[0].astype(F32), (0, LANES - N_EXPERTS)).reshape(1, LANES)

    xt = x.reshape(t, d)
    h0, z, z_lr = _ln_inproj(xt, vec(emb_ln_g), vec(emb_ln_b), w_main, w_lr)
    o_a = _gla_a(z, lb0, vec(norm_a_g[0]), bsz, seq)
    o_b = _gla_b(z, z_lr, w_up, vec(b_gla_up[0]), vec(norm_b_g[0]), bsz, seq)
    h1, top_e, top_w = _post_mixer(
        o_a, o_b, z, h0, w_proj_a[0].astype(BF16), w_proj_b[0].astype(BF16), w_out[0].astype(BF16),
        vec(ln_mix_g[0]), vec(ln_mix_b[0]), wr_hi, wr_lo, b_r)

    rank, cnt = _rank(top_e)
    counts = cnt[0, :N_EXPERTS].astype(jnp.int32)
    padded = (counts + MOE_ROWS - 1) // MOE_ROWS * MOE_ROWS
    pend = jnp.cumsum(padded)
    pstart = pend - padded
    n_blocks = t * TOP_K // MOE_ROWS + N_EXPERTS
    n_rows = n_blocks * MOE_ROWS
    dest = (pstart[top_e[:, :TOP_K]] + rank[:, :TOP_K]).reshape(-1)
    block_e = jnp.minimum(
        jnp.searchsorted(pend, jnp.arange(n_blocks, dtype=jnp.int32) * MOE_ROWS, side='right'),
        N_EXPERTS - 1).astype(jnp.int32)

    xs = _dispatch(dest, h1.reshape((t,) + ROW_TILE), n_rows)
    ys = _experts(block_e, xs.reshape(n_rows, d), w_gate_up[0].astype(BF16),
                  b_gate_up[0].reshape(N_EXPERTS, 1, 2 * D_FF).astype(F32),
                  w_down[0].astype(BF16), b_down[0].reshape(N_EXPERTS, 1, d).astype(F32))
    g = _combine(dest, ys.reshape((n_rows,) + ROW_TILE), t)
    out = _final(g.reshape(TOP_K, t, d), top_w, h1, p[0].reshape(t, PLE_DIM),
                 w_ple_gate[0].astype(BF16), w_ple_proj[0].astype(BF16),
                 vec(ln_moe_g[0]), vec(ln_moe_b[0]))
    return out.reshape(bsz, seq, d)
```

```python
import functools

import jax
import jax.numpy as jnp
from jax import lax
from jax.experimental import pallas as pl
from jax.experimental.pallas import tpu as pltpu

F32 = jnp.float32
BF16 = jnp.bfloat16

D_MODEL = 1024
A_HEADS, A_DK, A_DV = 8, 128, 128
B_HEADS, B_DK, B_DV = 4, 128, 256
GLA_RANK = 16
GLA_TAU = 16.0
N_EXPERTS = 32
TOP_K = 4
D_FF = 1024
SWIGLU_LIMIT = 7.0
SWIGLU_ALPHA = 1.702
PLE_DIM = 256
LN_EPS = 1e-5
RMS_EPS = 1e-6
DEPTH = 1
DN_ALPHA = (2.0 * DEPTH) ** 0.25

LANES = 128
ROW_TILE = (8, LANES)
GLA_CHUNK = 64
GLA_TILE = 256
GLA_SAFE_LOG_DECAY = -60.0
MOE_ROWS = 256
DMA_BATCH = 128
VMEM_LIMIT = 56 * 1024 * 1024

Z_QA, Z_FA, Z_IA, Z_GA = 0, 1024, 2048, 3072
Z_QB, Z_KB, Z_VB, Z_RB = 4096, 4608, 5120, 6144
Z_GATE_A, Z_GATE_B = 7168, 8192
Z_COLS = 9216
W_IN_LR = 7168


def _layer_norm(x, g, b):
    mu = jnp.mean(x, axis=-1, keepdims=True)
    xc = x - mu
    var = jnp.mean(xc * xc, axis=-1, keepdims=True)
    return xc * lax.rsqrt(var + LN_EPS) * g + b


def _sigmoid(x):
    return jax.nn.sigmoid(x)


def _dot(a, b):
    return jnp.dot(a, b, preferred_element_type=F32)


def _dot_nt(a, b):
    return lax.dot_general(a, b, (((1,), (1,)), ((), ())), preferred_element_type=F32)


def _dot_tn(a, b):
    return lax.dot_general(a, b, (((0,), (0,)), ((), ())), preferred_element_type=F32)


def _load_row_tiles(ref):
    return jnp.concatenate([ref[:, s, :] for s in range(ROW_TILE[0])], axis=1)


def _store_row_tiles(ref, x):
    for s in range(ROW_TILE[0]):
        ref[:, s, :] = x[:, s * LANES:(s + 1) * LANES]


def _ln_inproj_kernel(x_ref, g_ref, b_ref, w_ref, wlr_ref, h_ref, z_ref, zlr_ref, hb_ref):
    @pl.when(pl.program_id(1) == 0)
    def _():
        h = _layer_norm(x_ref[...], g_ref[...], b_ref[...])
        h_ref[...] = h
        hb = h.astype(BF16)
        hb_ref[...] = hb
        zlr_ref[...] = _dot(hb, wlr_ref[...])

    z_ref[...] = _dot(hb_ref[...], w_ref[...])


def _ln_inproj(x, g, b, w_main, w_lr, tm=1024, tn=1024):
    t = x.shape[0]
    return pl.pallas_call(
        _ln_inproj_kernel,
        grid=(t // tm, Z_COLS // tn),
        in_specs=[
            pl.BlockSpec((tm, D_MODEL), lambda i, j: (i, 0)),
            pl.BlockSpec((1, D_MODEL), lambda i, j: (0, 0)),
            pl.BlockSpec((1, D_MODEL), lambda i, j: (0, 0)),
            pl.BlockSpec((D_MODEL, tn), lambda i, j: (0, j)),
            pl.BlockSpec((D_MODEL, LANES), lambda i, j: (0, 0)),
        ],
        out_specs=[
            pl.BlockSpec((tm, D_MODEL), lambda i, j: (i, 0)),
            pl.BlockSpec((tm, tn), lambda i, j: (i, j)),
            pl.BlockSpec((tm, LANES), lambda i, j: (i, 0)),
        ],
        out_shape=[
            jax.ShapeDtypeStruct((t, D_MODEL), F32),
            jax.ShapeDtypeStruct((t, Z_COLS), F32),
            jax.ShapeDtypeStruct((t, LANES), F32),
        ],
        scratch_shapes=[pltpu.VMEM((tm, D_MODEL), BF16)],
        compiler_params=pltpu.CompilerParams(
            dimension_semantics=("parallel", "arbitrary"), vmem_limit_bytes=VMEM_LIMIT),
        name="ln_inproj",
    )(x, g, b, w_main, w_lr)


def _chunk_cumsum(x, chunk):
    rows = lax.broadcasted_iota(jnp.int32, x.shape, 0) % chunk
    s = 1
    while s < chunk:
        x = x + jnp.where(rows >= s, pltpu.roll(x, s, 0), 0.0)
        s *= 2
    return x


def _gla_tile(q, k, v, log_g, st_ref, sc_ref, ks_ref, bs_ref):
    ts = q.shape[0]
    c = GLA_CHUNK
    n = ts // c
    b = _chunk_cumsum(log_g, c)
    q1 = (q * jnp.exp(b)).astype(BF16)
    col = lax.broadcasted_iota(jnp.int32, (c, c), 1)
    causal = lax.broadcasted_iota(jnp.int32, (c, c), 0) >= col
    safe = jnp.min(b) >= GLA_SAFE_LOG_DECAY

    @pl.when(safe)
    def _product_form():
        k1 = (k * jnp.exp(-b)).astype(BF16)
        for ci in range(n):
            sl = slice(ci * c, (ci + 1) * c)
            sc_ref[ci] = _dot_nt(q1[sl], k1[sl])

    @pl.when(jnp.logical_not(safe))
    def _exact_form():
        ks_ref[...] = k
        bs_ref[...] = b
        for ci in range(n):
            sl = slice(ci * c, (ci + 1) * c)
            qc, bc = q[sl], b[sl]

            def column(j, acc):
                kj = ks_ref[pl.ds(ci * c + j, 1), :]
                bj = bs_ref[pl.ds(ci * c + j, 1), :]
                term = qc * kj * jnp.exp(jnp.minimum(bc - bj, 0.0))
                return jnp.where(col == j, jnp.sum(term, axis=-1, keepdims=True), acc)

            sc_ref[ci] = lax.fori_loop(0, c, column, jnp.zeros((c, c), F32))

    st = st_ref[...]
    outs = []
    for ci in range(n):
        sl = slice(ci * c, (ci + 1) * c)
        bc = b[sl]
        b_last = bc[c - 1:c]
        k_dec = (k[sl] * jnp.exp(b_last - bc)).astype(BF16)
        vb = v[sl].astype(BF16)
        scores = jnp.where(causal, sc_ref[ci], 0.0).astype(BF16)
        outs.append(_dot_nt(q1[sl], st.astype(BF16)) + _dot(scores, vb))
        st = st * jnp.exp(b_last) + _dot_tn(vb, k_dec)
    st_ref[...] = st
    return outs


def _head_norm_gate_store(outs, norm_g, gate, o_ref):
    c = GLA_CHUNK
    for ci, o in enumerate(outs):
        sl = slice(ci * c, (ci + 1) * c)
        on = o * lax.rsqrt(jnp.mean(o * o, axis=-1, keepdims=True) + RMS_EPS) * norm_g
        gc = gate[sl]
        o_ref[sl, :] = (on * (gc * _sigmoid(gc))).astype(o_ref.dtype)


def _gla_a_kernel(q_ref, f_ref, i_ref, g_ref, lb_ref, ng_ref, o_ref, st_ref, sc_ref, ks_ref, bs_ref):
    @pl.when(pl.program_id(2) == 0)
    def _():
        st_ref[...] = jnp.zeros_like(st_ref)

    lb = lb_ref[...]
    fz = f_ref[...]
    forget = lb + (1.0 - lb) * _sigmoid(fz)
    k = (1.0 - lb) * _sigmoid(-fz)
    qz = q_ref[...]
    q = qz * _sigmoid(qz) * (A_DK ** -0.5)
    outs = _gla_tile(q, k, i_ref[...], jnp.log(forget), st_ref, sc_ref, ks_ref, bs_ref)
    _head_norm_gate_store(outs, ng_ref[...], g_ref[...], o_ref)


def _gla_b_kernel(q_ref, k_ref, v_ref, r_ref, lr_ref, wup_ref, bup_ref, ng_ref, o_ref,
                  st_ref, sc_ref, ks_ref, bs_ref):
    @pl.when(pl.program_id(2) == 0)
    def _():
        st_ref[...] = jnp.zeros_like(st_ref)

    u = _dot(lr_ref[...].astype(BF16), wup_ref[...]) + bup_ref[...]
    log_sig = jnp.minimum(u, 0.0) - jnp.log(1.0 + jnp.exp(-jnp.abs(u)))
    q = q_ref[...] * (B_DK ** -0.5)
    outs = _gla_tile(q, k_ref[...], v_ref[...], log_sig / GLA_TAU, st_ref, sc_ref, ks_ref, bs_ref)
    _head_norm_gate_store(outs, ng_ref[...], r_ref[...], o_ref)


def _gla_scratch(dk, dv):
    return [
        pltpu.VMEM((dv, dk), F32),
        pltpu.VMEM((GLA_TILE // GLA_CHUNK, GLA_CHUNK, GLA_CHUNK), F32),
        pltpu.VMEM((GLA_TILE, dk), F32),
        pltpu.VMEM((GLA_TILE, dk), F32),
    ]


def _gla_a(z, lb, norm_g, bsz, seq):
    ts = GLA_TILE
    nt = seq // ts

    def zcol(off):
        return pl.BlockSpec((ts, A_DK), lambda b, h, t: (b * nt + t, off // A_DK + h))

    return pl.pallas_call(
        _gla_a_kernel,
        grid=(bsz, A_HEADS, nt),
        in_specs=[
            zcol(Z_QA), zcol(Z_FA), zcol(Z_IA), zcol(Z_GA),
            pl.BlockSpec((1, A_DK), lambda b, h, t: (0, h)),
            pl.BlockSpec((1, A_DV), lambda b, h, t: (0, 0)),
        ],
        out_specs=pl.BlockSpec((ts, A_DV), lambda b, h, t: (b * nt + t, h)),
        out_shape=jax.ShapeDtypeStruct((bsz * seq, A_HEADS * A_DV), BF16),
        scratch_shapes=_gla_scratch(A_DK, A_DV),
        compiler_params=pltpu.CompilerParams(
            dimension_semantics=("parallel", "parallel", "arbitrary"), vmem_limit_bytes=VMEM_LIMIT),
        name="gla_a",
    )(z, z, z, z, lb, norm_g)


def _gla_b(z, z_lr, w_up, b_up, norm_g, bsz, seq):
    ts = GLA_TILE
    nt = seq // ts

    def zcol(off, width):
        return pl.BlockSpec((ts, width), lambda b, h, t: (b * nt + t, off // width + h))

    return pl.pallas_call(
        _gla_b_kernel,
        grid=(bsz, B_HEADS, nt),
        in_specs=[
            zcol(Z_QB, B_DK), zcol(Z_KB, B_DK), zcol(Z_VB, B_DV), zcol(Z_RB, B_DV),
            pl.BlockSpec((ts, LANES), lambda b, h, t: (b * nt + t, 0)),
            pl.BlockSpec((LANES, B_DK), lambda b, h, t: (0, h)),
            pl.BlockSpec((1, B_DK), lambda b, h, t: (0, h)),
            pl.BlockSpec((1, B_DV), lambda b, h, t: (0, 0)),
        ],
        out_specs=pl.BlockSpec((ts, B_DV), lambda b, h, t: (b * nt + t, h)),
        out_shape=jax.ShapeDtypeStruct((bsz * seq, B_HEADS * B_DV), BF16),
        scratch_shapes=_gla_scratch(B_DK, B_DV),
        compiler_params=pltpu.CompilerParams(
            dimension_semantics=("parallel", "parallel", "arbitrary"), vmem_limit_bytes=VMEM_LIMIT),
        name="gla_b",
    )(z, z, z, z, z_lr, w_up, b_up, norm_g)


def _lane_pack(cols, dtype):
    m = cols[0].shape[0]
    lane = lax.broadcasted_iota(jnp.int32, (m, LANES), 1)
    out = jnp.zeros((m, LANES), dtype)
    for k, cvals in enumerate(cols):
        out = jnp.where(lane == k, cvals.astype(dtype), out)
    return out


def _post_mixer_kernel(oa_ref, ob_ref, ga_ref, gb_ref, h0_ref, wa_ref, wb_ref, wo_ref, lg_ref, lb_ref,
                       wrh_ref, wrl_ref, br_ref, h1_ref, te_ref, tw_ref):
    ya = _dot(oa_ref[...], wa_ref[...])
    yb = _dot(ob_ref[...], wb_ref[...])
    merged = _sigmoid(ga_ref[...]) * ya + _sigmoid(gb_ref[...]) * yb
    y = _dot(merged.astype(BF16), wo_ref[...])
    h1 = _layer_norm(DN_ALPHA * h0_ref[...] + y, lg_ref[...], lb_ref[...])
    _store_row_tiles(h1_ref, h1)

    hh = h1.astype(BF16)
    hl = (h1 - hh.astype(F32)).astype(BF16)
    logits = _dot(hh, wrh_ref[...]) + (_dot(hl, wrh_ref[...]) + _dot(hh, wrl_ref[...])) + br_ref[...]
    lane = lax.broadcasted_iota(jnp.int32, logits.shape, 1)
    work = jnp.where(lane < N_EXPERTS, logits, -jnp.inf)
    vals, idxs = [], []
    for _ in range(TOP_K):
        m = jnp.max(work, axis=-1, keepdims=True)
        idx = jnp.min(jnp.where(work == m, lane, LANES), axis=-1, keepdims=True)
        vals.append(m)
        idxs.append(idx)
        work = jnp.where(lane == idx, -jnp.inf, work)
    exps = [jnp.exp(v - vals[0]) for v in vals]
    denom = exps[0] + exps[1] + exps[2] + exps[3]
    te_ref[...] = _lane_pack(idxs, jnp.int32)
    tw_ref[...] = _lane_pack([e / denom for e in exps], F32)


def _post_mixer(o_a, o_b, z, h0, w_a, w_b, w_o, ln_g, ln_b, wr_hi, wr_lo, b_r, tm=256):
    t = h0.shape[0]
    row = lambda i: (i, 0)
    const = lambda i: (0, 0)
    full_w = pl.BlockSpec((D_MODEL, D_MODEL), const)
    vec = pl.BlockSpec((1, D_MODEL), const)
    return pl.pallas_call(
        _post_mixer_kernel,
        grid=(t // tm,),
        in_specs=[
            pl.BlockSpec((tm, D_MODEL), row),
            pl.BlockSpec((tm, D_MODEL), row),
            pl.BlockSpec((tm, D_MODEL), lambda i: (i, Z_GATE_A // D_MODEL)),
            pl.BlockSpec((tm, D_MODEL), lambda i: (i, Z_GATE_B // D_MODEL)),
            pl.BlockSpec((tm, D_MODEL), row),
            full_w, full_w, full_w, vec, vec,
            pl.BlockSpec((D_MODEL, LANES), const),
            pl.BlockSpec((D_MODEL, LANES), const),
            pl.BlockSpec((1, LANES), const),
        ],
        out_specs=[
            pl.BlockSpec((tm,) + ROW_TILE, lambda i: (i, 0, 0)),
            pl.BlockSpec((tm, LANES), row),
            pl.BlockSpec((tm, LANES), row),
        ],
        out_shape=[
            jax.ShapeDtypeStruct((t,) + ROW_TILE, F32),
            jax.ShapeDtypeStruct((t, LANES), jnp.int32),
            jax.ShapeDtypeStruct((t, LANES), F32),
        ],
        compiler_params=pltpu.CompilerParams(
            dimension_semantics=("parallel",), vmem_limit_bytes=VMEM_LIMIT),
        name="post_mixer",
    )(o_a, o_b, z, z, h0, w_a, w_b, w_o, ln_g, ln_b, wr_hi, wr_lo, b_r)


def _rank_kernel(te_ref, rank_ref, cnt_ref, carry_ref):
    @pl.when(pl.program_id(0) == 0)
    def _():
        carry_ref[...] = jnp.zeros_like(carry_ref)

    te = te_ref[...]
    tr = te.shape[0]
    lane = lax.broadcasted_iota(jnp.int32, (tr, LANES), 1)
    onehots = [te[:, k:k + 1] == lane for k in range(TOP_K)]
    cnt = jnp.zeros((tr, LANES), F32)
    for oh in onehots:
        cnt = cnt + oh.astype(F32)
    strict_lower = (lax.broadcasted_iota(jnp.int32, (tr, tr), 0)
                    > lax.broadcasted_iota(jnp.int32, (tr, tr), 1))
    earlier = _dot(strict_lower.astype(BF16), cnt.astype(BF16)) + carry_ref[...]
    ranks = [jnp.sum(jnp.where(oh, earlier, 0.0), axis=-1, keepdims=True) for oh in onehots]
    rank_ref[...] = _lane_pack(ranks, jnp.int32)
    carry_ref[...] = carry_ref[...] + jnp.sum(cnt, axis=0, keepdims=True)
    cnt_ref[...] = carry_ref[...]


def _rank(top_e, tr=512):
    t = top_e.shape[0]
    return pl.pallas_call(
        _rank_kernel,
        grid=(t // tr,),
        in_specs=[pl.BlockSpec((tr, LANES), lambda i: (i, 0))],
        out_specs=[
            pl.BlockSpec((tr, LANES), lambda i: (i, 0)),
            pl.BlockSpec((1, LANES), lambda i: (0, 0)),
        ],
        out_shape=[
            jax.ShapeDtypeStruct((t, LANES), jnp.int32),
            jax.ShapeDtypeStruct((1, LANES), F32),
        ],
        scratch_shapes=[pltpu.VMEM((1, LANES), F32)],
        compiler_params=pltpu.CompilerParams(dimension_semantics=("arbitrary",)),
        name="rank",
    )(top_e)


def _wait_rows(ref_hbm, n, sem):
    pltpu.make_async_copy(ref_hbm.at[pl.ds(0, n)], ref_hbm.at[pl.ds(0, n)], sem).wait()


def _dispatch_kernel(dest_ref, h_ref, xs_init_hbm, xs_hbm, sem):
    del xs_init_hbm
    n_tok = h_ref.shape[0]

    def issue(r, carry):
        for k in range(TOP_K):
            pltpu.make_async_copy(h_ref.at[r], xs_hbm.at[dest_ref[r * TOP_K + k]], sem).start()
        return carry

    lax.fori_loop(0, n_tok, issue, 0)
    for _ in range(n_tok * TOP_K // DMA_BATCH):
        _wait_rows(xs_hbm, DMA_BATCH, sem)


def _dispatch(dest_flat, h_rows, n_rows, tok_per_step=256):
    t = h_rows.shape[0]
    xs_init = jnp.zeros((n_rows,) + ROW_TILE, F32)
    return pl.pallas_call(
        _dispatch_kernel,
        grid=(t // tok_per_step,),
        in_specs=[
            pl.BlockSpec((tok_per_step * TOP_K,), lambda i: (i,), memory_space=pltpu.SMEM),
            pl.BlockSpec((tok_per_step,) + ROW_TILE, lambda i: (i, 0, 0)),
            pl.BlockSpec(memory_space=pl.ANY),
        ],
        out_specs=pl.BlockSpec(memory_space=pl.ANY),
        out_shape=jax.ShapeDtypeStruct((n_rows,) + ROW_TILE, F32),
        scratch_shapes=[pltpu.SemaphoreType.DMA(())],
        input_output_aliases={2: 0},
        compiler_params=pltpu.CompilerParams(dimension_semantics=("arbitrary",)),
        name="dispatch",
    )(dest_flat, h_rows, xs_init)


def _experts_kernel(be_ref, x_ref, wgu_ref, bgu_ref, wd_ref, bd_ref, y_ref):
    del be_ref
    hgu = _dot(_load_row_tiles(x_ref).astype(BF16), wgu_ref[0]) + bgu_ref[0]
    gate = jnp.minimum(hgu[:, :D_FF], SWIGLU_LIMIT)
    up = jnp.clip(hgu[:, D_FF:], -SWIGLU_LIMIT, SWIGLU_LIMIT)
    act = (up + 1.0) * gate * _sigmoid(SWIGLU_ALPHA * gate)
    _store_row_tiles(y_ref, _dot(act.astype(BF16), wd_ref[0]) + bd_ref[0])


def _experts(block_e, xs, w_gu, b_gu, w_d, b_d):
    n_rows = xs.shape[0]
    n_blocks = n_rows // MOE_ROWS
    rows = pl.BlockSpec((MOE_ROWS,) + ROW_TILE, lambda i, be: (i, 0, 0))
    grid_spec = pltpu.PrefetchScalarGridSpec(
        num_scalar_prefetch=1,
        grid=(n_blocks,),
        in_specs=[
            rows,
            pl.BlockSpec((1, D_MODEL, 2 * D_FF), lambda i, be: (be[i], 0, 0)),
            pl.BlockSpec((1, 1, 2 * D_FF), lambda i, be: (be[i], 0, 0)),
            pl.BlockSpec((1, D_FF, D_MODEL), lambda i, be: (be[i], 0, 0)),
            pl.BlockSpec((1, 1, D_MODEL), lambda i, be: (be[i], 0, 0)),
        ],
        out_specs=rows,
    )
    return pl.pallas_call(
        _experts_kernel,
        grid_spec=grid_spec,
        out_shape=jax.ShapeDtypeStruct((n_rows,) + ROW_TILE, F32),
        compiler_params=pltpu.CompilerParams(
            dimension_semantics=("arbitrary",), vmem_limit_bytes=VMEM_LIMIT),
        name="experts",
    )(block_e, xs, w_gu, b_gu, w_d, b_d)


def _final_kernel(dest_ref, ys_hbm, tw_ref, h1_ref, p_ref, wpg_ref, wpp_ref, lg_ref, lb_ref, o_ref,
                  g_ref, sem):
    tm = h1_ref.shape[0]

    def issue(r, carry):
        for k in range(TOP_K):
            pltpu.make_async_copy(ys_hbm.at[dest_ref[r * TOP_K + k]], g_ref.at[k, r], sem).start()
        return carry

    lax.fori_loop(0, tm, issue, 0)
    h1 = _load_row_tiles(h1_ref)
    ple = _sigmoid(_dot(h1.astype(BF16), wpg_ref[...])) * _dot(p_ref[...].astype(BF16), wpp_ref[...])
    acc = DN_ALPHA * h1 + ple
    for _ in range(tm * TOP_K // DMA_BATCH):
        _wait_rows(ys_hbm, DMA_BATCH, sem)
    tw = tw_ref[...]
    y = jnp.concatenate(
        [sum(tw[:, k:k + 1] * g_ref[k, :, s, :] for k in range(TOP_K)) for s in range(ROW_TILE[0])],
        axis=1)
    o_ref[...] = _layer_norm(acc + y, lg_ref[...], lb_ref[...])


def _final(dest_flat, ys_rows, top_w, h1_rows, p, w_pg, w_pp, ln_g, ln_b, tm=256):
    t = h1_rows.shape[0]
    row = lambda i: (i, 0)
    const = lambda i: (0, 0)
    return pl.pallas_call(
        _final_kernel,
        grid=(t // tm,),
        in_specs=[
            pl.BlockSpec((tm * TOP_K,), lambda i: (i,), memory_space=pltpu.SMEM),
            pl.BlockSpec(memory_space=pl.ANY),
            pl.BlockSpec((tm, LANES), row),
            pl.BlockSpec((tm,) + ROW_TILE, lambda i: (i, 0, 0)),
            pl.BlockSpec((tm, PLE_DIM), row),
            pl.BlockSpec((D_MODEL, D_MODEL), const),
            pl.BlockSpec((PLE_DIM, D_MODEL), const),
            pl.BlockSpec((1, D_MODEL), const),
            pl.BlockSpec((1, D_MODEL), const),
        ],
        out_specs=pl.BlockSpec((tm, D_MODEL), row),
        out_shape=jax.ShapeDtypeStruct((t, D_MODEL), F32),
        scratch_shapes=[pltpu.VMEM((TOP_K, tm) + ROW_TILE, F32), pltpu.SemaphoreType.DMA(())],
        compiler_params=pltpu.CompilerParams(
            dimension_semantics=("arbitrary",), vmem_limit_bytes=VMEM_LIMIT),
        name="final",
    )(dest_flat, ys_rows, top_w, h1_rows, p, w_pg, w_pp, ln_g, ln_b)


def kernel(x, p, emb_ln_g, emb_ln_b, hgrn_lb, w_in, w_gla_up, b_gla_up, norm_a_g, norm_b_g, w_proj_a, w_proj_b, w_out, ln_mix_g, ln_mix_b, w_router, b_router, w_gate_up, b_gate_up, w_down, b_down, w_ple_gate, w_ple_proj, ln_moe_g, ln_moe_b):
    bsz, seq, d = x.shape
    t = bsz * seq
    assert d == D_MODEL and seq % GLA_TILE == 0 and t % 1024 == 0
    assert w_in.shape[0] == DEPTH == 1
    vec = lambda a: a.reshape(1, -1).astype(F32)

    lb0 = jax.nn.softmax(hgrn_lb.astype(F32), axis=0)[0:1]
    w_in0 = w_in[0]
    w_main = jnp.concatenate([w_in0[:, :W_IN_LR], w_in0[:, W_IN_LR + GLA_RANK:]], axis=1).astype(BF16)
    w_lr = jnp.pad(w_in0[:, W_IN_LR:W_IN_LR + GLA_RANK], ((0, 0), (0, LANES - GLA_RANK))).astype(BF16)
    w_up = jnp.pad(w_gla_up[0], ((0, LANES - GLA_RANK), (0, 0))).astype(BF16)
    wr = jnp.pad(w_router[0].astype(F32), ((0, 0), (0, LANES - N_EXPERTS)))
    wr_hi = wr.astype(BF16)
    wr_lo = (wr - wr_hi.astype(F32)).astype(BF16)
    b_r = jnp.pad(b_router[0].astype(F32), (0, LANES - N_EXPERTS)).reshape(1, LANES)

    xt = x.reshape(t, d)
    h0, z, z_lr = _ln_inproj(xt, vec(emb_ln_g), vec(emb_ln_b), w_main, w_lr)
    o_a = _gla_a(z, lb0, vec(norm_a_g[0]), bsz, seq)
    o_b = _gla_b(z, z_lr, w_up, vec(b_gla_up[0]), vec(norm_b_g[0]), bsz, seq)
    h1, top_e, top_w = _post_mixer(
        o_a, o_b, z, h0, w_proj_a[0].astype(BF16), w_proj_b[0].astype(BF16), w_out[0].astype(BF16),
        vec(ln_mix_g[0]), vec(ln_mix_b[0]), wr_hi, wr_lo, b_r)

    rank, cnt = _rank(top_e)
    counts = cnt[0, :N_EXPERTS].astype(jnp.int32)
    padded = (counts + MOE_ROWS - 1) // MOE_ROWS * MOE_ROWS
    pend = jnp.cumsum(padded)
    pstart = pend - padded
    n_blocks = t * TOP_K // MOE_ROWS + N_EXPERTS
    n_rows = n_blocks * MOE_ROWS
    dest = (pstart[top_e[:, :TOP_K]] + rank[:, :TOP_K]).reshape(-1)
    block_pos = jnp.arange(n_blocks, dtype=jnp.int32) * MOE_ROWS
    block_e = jnp.minimum(
        jnp.sum((pend[None, :] <= block_pos[:, None]).astype(jnp.int32), axis=1), N_EXPERTS - 1)

    xs = _dispatch(dest, h1, n_rows)
    ys = _experts(block_e, xs, w_gate_up[0].astype(BF16),
                  b_gate_up[0].reshape(N_EXPERTS, 1, 2 * D_FF).astype(F32),
                  w_down[0].astype(BF16), b_down[0].reshape(N_EXPERTS, 1, d).astype(F32))
    out = _final(dest, ys, top_w, h1, p[0].reshape(t, PLE_DIM),
                 w_ple_gate[0].astype(BF16), w_ple_proj[0].astype(BF16),
                 vec(ln_moe_g[0]), vec(ln_moe_b[0]))
    return out.reshape(bsz, seq, d)
```

```python
import jax
import jax.numpy as jnp
from jax import lax
from jax.experimental import pallas as pl
from jax.experimental.pallas import tpu as pltpu

F32 = jnp.float32
BF16 = jnp.bfloat16

D_MODEL = 1024
A_HEADS, A_DK, A_DV = 8, 128, 128
B_HEADS, B_DK, B_DV = 4, 128, 256
N_HEADS = A_HEADS + B_HEADS
HEAD_DK = 128
A_V = A_HEADS * A_DV
GLA_RANK = 16
GLA_TAU = 16.0
N_EXPERTS = 32
TOP_K = 4
D_FF = 1024
SWIGLU_LIMIT = 7.0
SWIGLU_ALPHA = 1.702
PLE_DIM = 256
LN_EPS = 1e-5
RMS_EPS = 1e-6
DEPTH = 1
DN_ALPHA = (2.0 * DEPTH) ** 0.25

LANES = 128
GLA_CHUNK = 64
GLA_TILE = 256
GLA_SAFE_LOG_DECAY = -60.0
MOE_ROWS = 256
DMA_BATCH = 128
VMEM_LIMIT = 56 * 1024 * 1024

Z_QA, Z_FA, Z_IA, Z_GA = 0, 1024, 2048, 3072
Z_QB, Z_KB, Z_VB, Z_RB = 4096, 4608, 5120, 6144
Z_MIXER = 7168
Z_GATE_A, Z_GATE_B = 7168, 8192
Z_COLS = 9216
W_IN_LR = 7168


def _layer_norm(x, g, b):
    mu = jnp.mean(x, axis=-1, keepdims=True)
    xc = x - mu
    var = jnp.mean(xc * xc, axis=-1, keepdims=True)
    return xc * lax.rsqrt(var + LN_EPS) * g + b


def _sigmoid(x):
    return jax.nn.sigmoid(x)


def _sigmoid_pair(x):
    e = jnp.exp(-jnp.abs(x))
    r = 1.0 / (1.0 + e)
    er = e * r
    pos = x >= 0.0
    return jnp.where(pos, r, er), jnp.where(pos, er, r)


def _dot(a, b):
    return jnp.dot(a, b, preferred_element_type=F32)


def _dot_nt(a, b):
    return lax.dot_general(a, b, (((1,), (1,)), ((), ())), preferred_element_type=F32)


def _dot_tn(a, b):
    return lax.dot_general(a, b, (((0,), (0,)), ((), ())), preferred_element_type=F32)


def _ln_inproj_kernel(x_ref, g_ref, b_ref, w_ref, wlr_ref, h_ref, z_ref, zlr_ref, hb_ref):
    @pl.when(pl.program_id(1) == 0)
    def _():
        h = _layer_norm(x_ref[...], g_ref[...], b_ref[...])
        h_ref[...] = h
        hb = h.astype(BF16)
        hb_ref[...] = hb
        zlr_ref[...] = _dot(hb, wlr_ref[...])

    z_ref[...] = _dot(hb_ref[...], w_ref[...])


def _ln_inproj(x, g, b, w_main, w_lr, tm=1024, tn=1024):
    t = x.shape[0]
    return pl.pallas_call(
        _ln_inproj_kernel,
        grid=(t // tm, Z_COLS // tn),
        in_specs=[
            pl.BlockSpec((tm, D_MODEL), lambda i, j: (i, 0)),
            pl.BlockSpec((1, D_MODEL), lambda i, j: (0, 0)),
            pl.BlockSpec((1, D_MODEL), lambda i, j: (0, 0)),
            pl.BlockSpec((D_MODEL, tn), lambda i, j: (0, j)),
            pl.BlockSpec((D_MODEL, LANES), lambda i, j: (0, 0)),
        ],
        out_specs=[
            pl.BlockSpec((tm, D_MODEL), lambda i, j: (i, 0)),
            pl.BlockSpec((tm, tn), lambda i, j: (i, j)),
            pl.BlockSpec((tm, LANES), lambda i, j: (i, 0)),
        ],
        out_shape=[
            jax.ShapeDtypeStruct((t, D_MODEL), F32),
            jax.ShapeDtypeStruct((t, Z_COLS), F32),
            jax.ShapeDtypeStruct((t, LANES), F32),
        ],
        scratch_shapes=[pltpu.VMEM((tm, D_MODEL), BF16)],
        compiler_params=pltpu.CompilerParams(
            dimension_semantics=("parallel", "arbitrary"), vmem_limit_bytes=VMEM_LIMIT),
        name="ln_inproj",
    )(x, g, b, w_main, w_lr)


def _chunk_cumsum(x, chunk):
    rows = lax.broadcasted_iota(jnp.int32, x.shape, 0) % chunk
    s = 1
    while s < chunk:
        x = x + jnp.where(rows >= s, pltpu.roll(x, s, 0), 0.0)
        s *= 2
    return x


def _gla_prep(z_ref, lr_ref, lb_ref, wup_ref, bup_ref, qs_ref, ks_ref, bs_ref):
    low = None
    for h in range(A_HEADS):
        cols = slice(h * A_DK, (h + 1) * A_DK)
        lb = lb_ref[:, cols]
        sig, sig_neg = _sigmoid_pair(z_ref[:, Z_FA + h * A_DK:Z_FA + (h + 1) * A_DK])
        b = _chunk_cumsum(jnp.log(lb + (1.0 - lb) * sig), GLA_CHUNK)
        qz = z_ref[:, Z_QA + h * A_DK:Z_QA + (h + 1) * A_DK]
        qs_ref[h] = qz * _sigmoid(qz) * (A_DK ** -0.5)
        ks_ref[h] = (1.0 - lb) * sig_neg
        bs_ref[h] = b
        low = b if low is None else jnp.minimum(low, b)
    u_all = _dot(lr_ref[...].astype(BF16), wup_ref[...]) + bup_ref[...]
    for hb in range(B_HEADS):
        h = A_HEADS + hb
        u = u_all[:, hb * B_DK:(hb + 1) * B_DK]
        log_sig = jnp.minimum(u, 0.0) - jnp.log(1.0 + jnp.exp(-jnp.abs(u)))
        b = _chunk_cumsum(log_sig / GLA_TAU, GLA_CHUNK)
        qs_ref[h] = z_ref[:, Z_QB + hb * B_DK:Z_QB + (hb + 1) * B_DK] * (B_DK ** -0.5)
        ks_ref[h] = z_ref[:, Z_KB + hb * B_DK:Z_KB + (hb + 1) * B_DK]
        bs_ref[h] = b
        low = jnp.minimum(low, b)
    return low


def _exact_scores(qs_ref, ks_ref, bs_ref, sc_ref):
    c = GLA_CHUNK
    n = GLA_TILE // c
    col = lax.broadcasted_iota(jnp.int32, (c, c), 1)

    def per_chunk(idx, carry):
        h = idx // n
        ci = idx - h * n
        r0 = pl.multiple_of(ci * c, c)
        qc = qs_ref[h, pl.ds(r0, c), :]
        bc = bs_ref[h, pl.ds(r0, c), :]

        def column(j, acc):
            kj = ks_ref[h, pl.ds(r0 + j, 1), :]
            bj = bs_ref[h, pl.ds(r0 + j, 1), :]
            term = qc * kj * jnp.exp(jnp.minimum(bc - bj, 0.0))
            return jnp.where(col == j, jnp.sum(term, axis=-1, keepdims=True), acc)

        sc_ref[h, ci] = lax.fori_loop(0, c, column, jnp.zeros((c, c), F32))
        return carry

    lax.fori_loop(0, N_HEADS * n, per_chunk, 0)


def _gla_outputs(z_ref, qs_ref, ks_ref, bs_ref, sta_ref, stb_ref, nga_ref, ngb_ref, o_ref, scores_fn):
    c = GLA_CHUNK
    causal = lax.broadcasted_iota(jnp.int32, (c, c), 0) >= lax.broadcasted_iota(jnp.int32, (c, c), 1)
    for h in range(N_HEADS):
        if h < A_HEADS:
            dv, st_ref, norm_g = A_DV, sta_ref.at[h], nga_ref[...]
            v_off, g_off, o_off = Z_IA + h * A_DV, Z_GA + h * A_DV, h * A_DV
        else:
            hb = h - A_HEADS
            dv, st_ref, norm_g = B_DV, stb_ref.at[hb], ngb_ref[...]
            v_off, g_off, o_off = Z_VB + hb * B_DV, Z_RB + hb * B_DV, A_V + hb * B_DV
        st = st_ref[...]
        for ci in range(GLA_TILE // c):
            rows = slice(ci * c, (ci + 1) * c)
            k = ks_ref[h, rows, :]
            bc = bs_ref[h, rows, :]
            b_last = bc[c - 1:c]
            q1 = (qs_ref[h, rows, :] * jnp.exp(bc)).astype(BF16)
            k_dec = (k * jnp.exp(b_last - bc)).astype(BF16)
            vb = z_ref[rows, v_off:v_off + dv].astype(BF16)
            scores = jnp.where(causal, scores_fn(h, ci, q1, k, bc), 0.0).astype(BF16)
            o = _dot_nt(q1, st.astype(BF16)) + _dot(scores, vb)
            st = st * jnp.exp(b_last) + _dot_tn(vb, k_dec)
            on = o * lax.rsqrt(jnp.mean(o * o, axis=-1, keepdims=True) + RMS_EPS) * norm_g
            gc = z_ref[rows, g_off:g_off + dv]
            o_ref[rows, o_off:o_off + dv] = (on * (gc * _sigmoid(gc))).astype(o_ref.dtype)
        st_ref[...] = st


def _gla_kernel(z_ref, lr_ref, lb_ref, wup_ref, bup_ref, nga_ref, ngb_ref, o_ref,
                sta_ref, stb_ref, qs_ref, ks_ref, bs_ref, sc_ref):
    @pl.when(pl.program_id(1) == 0)
    def _():
        sta_ref[...] = jnp.zeros_like(sta_ref)
        stb_ref[...] = jnp.zeros_like(stb_ref)

    low = _gla_prep(z_ref, lr_ref, lb_ref, wup_ref, bup_ref, qs_ref, ks_ref, bs_ref)
    safe = jnp.min(low) >= GLA_SAFE_LOG_DECAY
    args = (z_ref, qs_ref, ks_ref, bs_ref, sta_ref, stb_ref, nga_ref, ngb_ref, o_ref)

    @pl.when(safe)
    def _product_form():
        _gla_outputs(*args, lambda h, ci, q1, k, bc: _dot_nt(q1, (k * jnp.exp(-bc)).astype(BF16)))

    @pl.when(jnp.logical_not(safe))
    def _exact_form():
        _exact_scores(qs_ref, ks_ref, bs_ref, sc_ref)
        _gla_outputs(*args, lambda h, ci, q1, k, bc: sc_ref[h, ci])


def _gla(z, z_lr, lb, w_up, b_up, norm_a, norm_b, bsz, seq):
    ts = GLA_TILE
    nt = seq // ts
    n_chunks = ts // GLA_CHUNK
    row = lambda b, t: (b * nt + t, 0)
    const = lambda b, t: (0, 0)
    per_head = pltpu.VMEM((N_HEADS, ts, HEAD_DK), F32)
    return pl.pallas_call(
        _gla_kernel,
        grid=(bsz, nt),
        in_specs=[
            pl.BlockSpec((ts, Z_MIXER), row),
            pl.BlockSpec((ts, LANES), row),
            pl.BlockSpec((1, A_HEADS * A_DK), const),
            pl.BlockSpec((LANES, B_HEADS * B_DK), const),
            pl.BlockSpec((1, B_HEADS * B_DK), const),
            pl.BlockSpec((1, A_DV), const),
            pl.BlockSpec((1, B_DV), const),
        ],
        out_specs=pl.BlockSpec((ts, 2 * D_MODEL), row),
        out_shape=jax.ShapeDtypeStruct((bsz * seq, 2 * D_MODEL), BF16),
        scratch_shapes=[
            pltpu.VMEM((A_HEADS, A_DV, A_DK), F32),
            pltpu.VMEM((B_HEADS, B_DV, B_DK), F32),
            per_head, per_head, per_head,
            pltpu.VMEM((N_HEADS, n_chunks, GLA_CHUNK, GLA_CHUNK), F32),
        ],
        compiler_params=pltpu.CompilerParams(
            dimension_semantics=("parallel", "arbitrary"), vmem_limit_bytes=VMEM_LIMIT),
        name="gla",
    )(z, z_lr, lb, w_up, b_up, norm_a, norm_b)


def _lane_pack(cols, dtype):
    m = cols[0].shape[0]
    lane = lax.broadcasted_iota(jnp.int32, (m, LANES), 1)
    out = jnp.zeros((m, LANES), dtype)
    for k, cvals in enumerate(cols):
        out = jnp.where(lane == k, cvals.astype(dtype), out)
    return out


def _post_mixer_kernel(o_ref, ga_ref, gb_ref, h0_ref, wa_ref, wb_ref, wo_ref, lg_ref, lb_ref,
                       wrh_ref, wrl_ref, br_ref, h1_ref, te_ref, tw_ref):
    ya = _dot(o_ref[:, :D_MODEL], wa_ref[...])
    yb = _dot(o_ref[:, D_MODEL:], wb_ref[...])
    merged = _sigmoid(ga_ref[...]) * ya + _sigmoid(gb_ref[...]) * yb
    y = _dot(merged.astype(BF16), wo_ref[...])
    h1 = _layer_norm(DN_ALPHA * h0_ref[...] + y, lg_ref[...], lb_ref[...])
    h1_ref[...] = h1

    hh = h1.astype(BF16)
    hl = (h1 - hh.astype(F32)).astype(BF16)
    logits = _dot(hh, wrh_ref[...]) + (_dot(hl, wrh_ref[...]) + _dot(hh, wrl_ref[...])) + br_ref[...]
    lane = lax.broadcasted_iota(jnp.int32, logits.shape, 1)
    work = jnp.where(lane < N_EXPERTS, logits, -jnp.inf)
    vals, idxs = [], []
    for _ in range(TOP_K):
        m = jnp.max(work, axis=-1, keepdims=True)
        idx = jnp.min(jnp.where(work == m, lane, LANES), axis=-1, keepdims=True)
        vals.append(m)
        idxs.append(idx)
        work = jnp.where(lane == idx, -jnp.inf, work)
    exps = [jnp.exp(v - vals[0]) for v in vals]
    denom = exps[0] + exps[1] + exps[2] + exps[3]
    te_ref[...] = _lane_pack(idxs, jnp.int32)
    tw_ref[...] = _lane_pack([e / denom for e in exps], F32)


def _post_mixer(o, z, h0, w_a, w_b, w_o, ln_g, ln_b, wr_hi, wr_lo, b_r, tm=256):
    t = h0.shape[0]
    row = lambda i: (i, 0)
    const = lambda i: (0, 0)
    full_w = pl.BlockSpec((D_MODEL, D_MODEL), const)
    vec = pl.BlockSpec((1, D_MODEL), const)
    return pl.pallas_call(
        _post_mixer_kernel,
        grid=(t // tm,),
        in_specs=[
            pl.BlockSpec((tm, 2 * D_MODEL), row),
            pl.BlockSpec((tm, D_MODEL), lambda i: (i, Z_GATE_A // D_MODEL)),
            pl.BlockSpec((tm, D_MODEL), lambda i: (i, Z_GATE_B // D_MODEL)),
            pl.BlockSpec((tm, D_MODEL), row),
            full_w, full_w, full_w, vec, vec,
            pl.BlockSpec((D_MODEL, LANES), const),
            pl.BlockSpec((D_MODEL, LANES), const),
            pl.BlockSpec((1, LANES), const),
        ],
        out_specs=[
            pl.BlockSpec((tm, D_MODEL), row),
            pl.BlockSpec((tm, LANES), row),
            pl.BlockSpec((tm, LANES), row),
        ],
        out_shape=[
            jax.ShapeDtypeStruct((t, D_MODEL), F32),
            jax.ShapeDtypeStruct((t, LANES), jnp.int32),
            jax.ShapeDtypeStruct((t, LANES), F32),
        ],
        compiler_params=pltpu.CompilerParams(
            dimension_semantics=("parallel",), vmem_limit_bytes=VMEM_LIMIT),
        name="post_mixer",
    )(o, z, z, h0, w_a, w_b, w_o, ln_g, ln_b, wr_hi, wr_lo, b_r)


def _rank_kernel(te_ref, rank_ref, cnt_ref, carry_ref):
    @pl.when(pl.program_id(0) == 0)
    def _():
        carry_ref[...] = jnp.zeros_like(carry_ref)

    te = te_ref[...]
    tr = te.shape[0]
    lane = lax.broadcasted_iota(jnp.int32, (tr, LANES), 1)
    onehots = [te[:, k:k + 1] == lane for k in range(TOP_K)]
    cnt = jnp.zeros((tr, LANES), F32)
    for oh in onehots:
        cnt = cnt + oh.astype(F32)
    strict_lower = (lax.broadcasted_iota(jnp.int32, (tr, tr), 0)
                    > lax.broadcasted_iota(jnp.int32, (tr, tr), 1))
    earlier = _dot(strict_lower.astype(BF16), cnt.astype(BF16)) + carry_ref[...]
    ranks = [jnp.sum(jnp.where(oh, earlier, 0.0), axis=-1, keepdims=True) for oh in onehots]
    rank_ref[...] = _lane_pack(ranks, jnp.int32)
    carry_ref[...] = carry_ref[...] + jnp.sum(cnt, axis=0, keepdims=True)
    cnt_ref[...] = carry_ref[...]


def _rank(top_e, tr=512):
    t = top_e.shape[0]
    return pl.pallas_call(
        _rank_kernel,
        grid=(t // tr,),
        in_specs=[pl.BlockSpec((tr, LANES), lambda i: (i, 0))],
        out_specs=[
            pl.BlockSpec((tr, LANES), lambda i: (i, 0)),
            pl.BlockSpec((1, LANES), lambda i: (0, 0)),
        ],
        out_shape=[
            jax.ShapeDtypeStruct((t, LANES), jnp.int32),
            jax.ShapeDtypeStruct((1, LANES), F32),
        ],
        scratch_shapes=[pltpu.VMEM((1, LANES), F32)],
        compiler_params=pltpu.CompilerParams(dimension_semantics=("arbitrary",)),
        name="rank",
    )(top_e)


def _row(ref, r):
    return ref.at[pl.ds(r, 1), :]


def _wait_rows(ref_hbm, n, sem):
    pltpu.make_async_copy(ref_hbm.at[pl.ds(0, n), :], ref_hbm.at[pl.ds(0, n), :], sem).wait()


def _dispatch_kernel(pad_start_ref, pad_cnt_ref, nu_ref, dest_ref, h_ref, xs_hbm, zero_ref, sem):
    n_tok = h_ref.shape[0]

    @pl.when(pl.program_id(0) == 0)
    def _zero_rows_without_token():
        zero_ref[...] = jnp.zeros_like(zero_ref)

        def per_expert(e, carry):
            def start(r, c):
                pltpu.make_async_copy(_row(zero_ref, 0), _row(xs_hbm, pad_start_ref[e] + r), sem).start()
                return c

            def wait(r, c):
                _wait_rows(xs_hbm, 1, sem)
                return c

            lax.fori_loop(0, pad_cnt_ref[e], start, 0)
            lax.fori_loop(0, pad_cnt_ref[e], wait, 0)
            return carry

        lax.fori_loop(0, N_EXPERTS, per_expert, 0)

        def block_copy(blk):
            r0 = pl.multiple_of(blk * MOE_ROWS, MOE_ROWS)
            return pltpu.make_async_copy(zero_ref, xs_hbm.at[pl.ds(r0, MOE_ROWS), :], sem)

        def start_block(blk, c):
            block_copy(blk).start()
            return c

        def wait_block(blk, c):
            block_copy(blk).wait()
            return c

        n_blocks = xs_hbm.shape[0] // MOE_ROWS
        lax.fori_loop(nu_ref[0], n_blocks, start_block, 0)
        lax.fori_loop(nu_ref[0], n_blocks, wait_block, 0)

    def issue(r, carry):
        for k in range(TOP_K):
            pltpu.make_async_copy(_row(h_ref, r), _row(xs_hbm, dest_ref[r * TOP_K + k]), sem).start()
        return carry

    lax.fori_loop(0, n_tok, issue, 0)
    for _ in range(n_tok * TOP_K // DMA_BATCH):
        _wait_rows(xs_hbm, DMA_BATCH, sem)


def _dispatch(pad_start, pad_cnt, n_used, dest_flat, h1, n_rows, tok_per_step=256):
    t = h1.shape[0]
    grid_spec = pltpu.PrefetchScalarGridSpec(
        num_scalar_prefetch=3,
        grid=(t // tok_per_step,),
        in_specs=[
            pl.BlockSpec((tok_per_step * TOP_K,), lambda i, ps, pc, nu: (i,), memory_space=pltpu.SMEM),
            pl.BlockSpec((tok_per_step, D_MODEL), lambda i, ps, pc, nu: (i, 0)),
        ],
        out_specs=pl.BlockSpec(memory_space=pl.ANY),
        scratch_shapes=[pltpu.VMEM((MOE_ROWS, D_MODEL), F32), pltpu.SemaphoreType.DMA(())],
    )
    return pl.pallas_call(
        _dispatch_kernel,
        grid_spec=grid_spec,
        out_shape=jax.ShapeDtypeStruct((n_rows, D_MODEL), F32),
        compiler_params=pltpu.CompilerParams(dimension_semantics=("arbitrary",)),
        name="dispatch",
    )(pad_start, pad_cnt, n_used, dest_flat, h1)


def _experts_kernel(be_ref, nu_ref, x_ref, wgu_ref, bgu_ref, wd_ref, bd_ref, y_ref):
    del be_ref

    used = pl.program_id(0) < nu_ref[0]

    @pl.when(used)
    def _():
        hgu = _dot(x_ref[...].astype(BF16), wgu_ref[0]) + bgu_ref[0]
        gate = jnp.minimum(hgu[:, :D_FF], SWIGLU_LIMIT)
        up = jnp.clip(hgu[:, D_FF:], -SWIGLU_LIMIT, SWIGLU_LIMIT)
        act = (up + 1.0) * gate * _sigmoid(SWIGLU_ALPHA * gate)
        y_ref[...] = _dot(act.astype(BF16), wd_ref[0]) + bd_ref[0]

    @pl.when(jnp.logical_not(used))
    def _():
        y_ref[...] = jnp.zeros_like(y_ref)


def _experts(block_e, n_used, xs, w_gu, b_gu, w_d, b_d):
    n_rows = xs.shape[0]
    n_blocks = n_rows // MOE_ROWS
    grid_spec = pltpu.PrefetchScalarGridSpec(
        num_scalar_prefetch=2,
        grid=(n_blocks,),
        in_specs=[
            pl.BlockSpec((MOE_ROWS, D_MODEL), lambda i, be, nu: (jnp.minimum(i, nu[0] - 1), 0)),
            pl.BlockSpec((1, D_MODEL, 2 * D_FF), lambda i, be, nu: (be[i], 0, 0)),
            pl.BlockSpec((1, 1, 2 * D_FF), lambda i, be, nu: (be[i], 0, 0)),
            pl.BlockSpec((1, D_FF, D_MODEL), lambda i, be, nu: (be[i], 0, 0)),
            pl.BlockSpec((1, 1, D_MODEL), lambda i, be, nu: (be[i], 0, 0)),
        ],
        out_specs=pl.BlockSpec((MOE_ROWS, D_MODEL), lambda i, be, nu: (i, 0)),
    )
    return pl.pallas_call(
        _experts_kernel,
        grid_spec=grid_spec,
        out_shape=jax.ShapeDtypeStruct((n_rows, D_MODEL), F32),
        compiler_params=pltpu.CompilerParams(
            dimension_semantics=("arbitrary",), vmem_limit_bytes=VMEM_LIMIT),
        name="experts",
    )(block_e, n_used, xs, w_gu, b_gu, w_d, b_d)


def _final_kernel(dest_ref, ys_hbm, tw_ref, h1_ref, p_ref, wpg_ref, wpp_ref, lg_ref, lb_ref, o_ref,
                  g_ref, sem):
    tm = h1_ref.shape[0]

    def issue(r, carry):
        for k in range(TOP_K):
            pltpu.make_async_copy(_row(ys_hbm, dest_ref[r * TOP_K + k]), _row(g_ref.at[k], r), sem).start()
        return carry

    lax.fori_loop(0, tm, issue, 0)
    h1 = h1_ref[...]
    ple = _sigmoid(_dot(h1.astype(BF16), wpg_ref[...])) * _dot(p_ref[...].astype(BF16), wpp_ref[...])
    acc = DN_ALPHA * h1 + ple
    for _ in range(tm * TOP_K // DMA_BATCH):
        _wait_rows(ys_hbm, DMA_BATCH, sem)
    tw = tw_ref[...]
    for k in range(TOP_K):
        acc = acc + tw[:, k:k + 1] * g_ref[k]
    o_ref[...] = _layer_norm(acc, lg_ref[...], lb_ref[...])


def _final(dest_flat, ys, top_w, h1, p, w_pg, w_pp, ln_g, ln_b, tm=256):
    t = h1.shape[0]
    row = lambda i: (i, 0)
    const = lambda i: (0, 0)
    return pl.pallas_call(
        _final_kernel,
        grid=(t // tm,),
        in_specs=[
            pl.BlockSpec((tm * TOP_K,), lambda i: (i,), memory_space=pltpu.SMEM),
            pl.BlockSpec(memory_space=pl.ANY),
            pl.BlockSpec((tm, LANES), row),
            pl.BlockSpec((tm, D_MODEL), row),
            pl.BlockSpec((tm, PLE_DIM), row),
            pl.BlockSpec((D_MODEL, D_MODEL), const),
            pl.BlockSpec((PLE_DIM, D_MODEL), const),
            pl.BlockSpec((1, D_MODEL), const),
            pl.BlockSpec((1, D_MODEL), const),
        ],
        out_specs=pl.BlockSpec((tm, D_MODEL), row),
        out_shape=jax.ShapeDtypeStruct((t, D_MODEL), F32),
        scratch_shapes=[pltpu.VMEM((TOP_K, tm, D_MODEL), F32), pltpu.SemaphoreType.DMA(())],
        compiler_params=pltpu.CompilerParams(
            dimension_semantics=("arbitrary",), vmem_limit_bytes=VMEM_LIMIT),
        name="final",
    )(dest_flat, ys, top_w, h1, p, w_pg, w_pp, ln_g, ln_b)


def kernel(x, p, emb_ln_g, emb_ln_b, hgrn_lb, w_in, w_gla_up, b_gla_up, norm_a_g, norm_b_g, w_proj_a, w_proj_b, w_out, ln_mix_g, ln_mix_b, w_router, b_router, w_gate_up, b_gate_up, w_down, b_down, w_ple_gate, w_ple_proj, ln_moe_g, ln_moe_b):
    bsz, seq, d = x.shape
    t = bsz * seq
    assert d == D_MODEL and seq % GLA_TILE == 0 and t % 1024 == 0
    assert w_in.shape[0] == DEPTH == 1
    vec = lambda a: a.reshape(1, -1).astype(F32)

    lb0 = jax.nn.softmax(hgrn_lb.astype(F32), axis=0)[0:1]
    w_in0 = w_in[0]
    w_main = jnp.concatenate([w_in0[:, :W_IN_LR], w_in0[:, W_IN_LR + GLA_RANK:]], axis=1).astype(BF16)
    w_lr = jnp.pad(w_in0[:, W_IN_LR:W_IN_LR + GLA_RANK], ((0, 0), (0, LANES - GLA_RANK))).astype(BF16)
    w_up = jnp.pad(w_gla_up[0], ((0, LANES - GLA_RANK), (0, 0))).astype(BF16)
    wr = jnp.pad(w_router[0].astype(F32), ((0, 0), (0, LANES - N_EXPERTS)))
    wr_hi = wr.astype(BF16)
    wr_lo = (wr - wr_hi.astype(F32)).astype(BF16)
    b_r = jnp.pad(b_router[0].astype(F32), (0, LANES - N_EXPERTS)).reshape(1, LANES)

    xt = x.reshape(t, d)
    h0, z, z_lr = _ln_inproj(xt, vec(emb_ln_g), vec(emb_ln_b), w_main, w_lr)
    o = _gla(z, z_lr, lb0, w_up, vec(b_gla_up[0]), vec(norm_a_g[0]), vec(norm_b_g[0]), bsz, seq)
    h1, top_e, top_w = _post_mixer(
        o, z, h0, w_proj_a[0].astype(BF16), w_proj_b[0].astype(BF16), w_out[0].astype(BF16),
        vec(ln_mix_g[0]), vec(ln_mix_b[0]), wr_hi, wr_lo, b_r)

    rank, cnt = _rank(top_e)
    counts = cnt[0, :N_EXPERTS].astype(jnp.int32)
    padded = (counts + MOE_ROWS - 1) // MOE_ROWS * MOE_ROWS
    pend = jnp.cumsum(padded)
    pstart = pend - padded
    n_blocks = t * TOP_K // MOE_ROWS + N_EXPERTS
    n_rows = n_blocks * MOE_ROWS
    n_used = pend[N_EXPERTS - 1:] // MOE_ROWS
    dest = (pstart[top_e[:, :TOP_K]] + rank[:, :TOP_K]).reshape(-1)
    block_pos = jnp.minimum(jnp.arange(n_blocks, dtype=jnp.int32), n_used - 1) * MOE_ROWS
    block_e = jnp.minimum(
        jnp.sum((pend[None, :] <= block_pos[:, None]).astype(jnp.int32), axis=1), N_EXPERTS - 1)

    xs = _dispatch(pstart + counts, padded - counts, n_used, dest, h1, n_rows)
    ys = _experts(block_e, n_used, xs, w_gate_up[0].astype(BF16),
                  b_gate_up[0].reshape(N_EXPERTS, 1, 2 * D_FF).astype(F32),
                  w_down[0].astype(BF16), b_down[0].reshape(N_EXPERTS, 1, d).astype(F32))
    out = _final(dest, ys, top_w, h1, p[0].reshape(t, PLE_DIM),
                 w_ple_gate[0].astype(BF16), w_ple_proj[0].astype(BF16),
                 vec(ln_moe_g[0]), vec(ln_moe_b[0]))
    return out.reshape(bsz, seq, d)
```

```python
import jax
import jax.numpy as jnp
from jax import lax
from jax.experimental import pallas as pl
from jax.experimental.pallas import tpu as pltpu

F32 = jnp.float32
BF16 = jnp.bfloat16

D_MODEL = 1024
A_HEADS, A_DK, A_DV = 8, 128, 128
B_HEADS, B_DK, B_DV = 4, 128, 256
N_HEADS = A_HEADS + B_HEADS
HEAD_DK = 128
A_V = A_HEADS * A_DV
GLA_RANK = 16
GLA_TAU = 16.0
N_EXPERTS = 32
TOP_K = 4
D_FF = 1024
SWIGLU_LIMIT = 7.0
SWIGLU_ALPHA = 1.702
PLE_DIM = 256
LN_EPS = 1e-5
RMS_EPS = 1e-6
DEPTH = 1
DN_ALPHA = (2.0 * DEPTH) ** 0.25

LANES = 128
GLA_CHUNK = 64
GLA_TILE = 256
GLA_SAFE_LOG_DECAY = -60.0
MOE_ROWS = 512
DMA_BATCH = 128
DMA_PRIORITIES = 2
VMEM_LIMIT = 56 * 1024 * 1024

Z_QA, Z_FA, Z_IA, Z_GA = 0, 1024, 2048, 3072
Z_QB, Z_KB, Z_VB, Z_RB = 4096, 4608, 5120, 6144
Z_MIXER = 7168
Z_GATE_A, Z_GATE_B = 7168, 8192
Z_COLS = 9216
W_IN_LR = 7168


def _layer_norm(x, g, b):
    mu = jnp.mean(x, axis=-1, keepdims=True)
    xc = x - mu
    var = jnp.mean(xc * xc, axis=-1, keepdims=True)
    return xc * lax.rsqrt(var + LN_EPS) * g + b


def _sigmoid(x):
    return jax.nn.sigmoid(x)


def _sigmoid_pair(x):
    e = jnp.exp(-jnp.abs(x))
    r = 1.0 / (1.0 + e)
    er = e * r
    pos = x >= 0.0
    return jnp.where(pos, r, er), jnp.where(pos, er, r)


def _dot(a, b):
    return jnp.dot(a, b, preferred_element_type=F32)


def _dot_nt(a, b):
    return lax.dot_general(a, b, (((1,), (1,)), ((), ())), preferred_element_type=F32)


def _dot_tn(a, b):
    return lax.dot_general(a, b, (((0,), (0,)), ((), ())), preferred_element_type=F32)


def _ln_inproj_kernel(x_ref, g_ref, b_ref, w_ref, wlr_ref, h_ref, z_ref, zlr_ref, hb_ref):
    @pl.when(pl.program_id(1) == 0)
    def _():
        h = _layer_norm(x_ref[...], g_ref[...], b_ref[...])
        h_ref[...] = h
        hb = h.astype(BF16)
        hb_ref[...] = hb
        zlr_ref[...] = _dot(hb, wlr_ref[...])

    z_ref[...] = _dot(hb_ref[...], w_ref[...])


def _ln_inproj(x, g, b, w_main, w_lr, tm=1024, tn=1024):
    t = x.shape[0]
    return pl.pallas_call(
        _ln_inproj_kernel,
        grid=(t // tm, Z_COLS // tn),
        in_specs=[
            pl.BlockSpec((tm, D_MODEL), lambda i, j: (i, 0)),
            pl.BlockSpec((1, D_MODEL), lambda i, j: (0, 0)),
            pl.BlockSpec((1, D_MODEL), lambda i, j: (0, 0)),
            pl.BlockSpec((D_MODEL, tn), lambda i, j: (0, j)),
            pl.BlockSpec((D_MODEL, LANES), lambda i, j: (0, 0)),
        ],
        out_specs=[
            pl.BlockSpec((tm, D_MODEL), lambda i, j: (i, 0)),
            pl.BlockSpec((tm, tn), lambda i, j: (i, j)),
            pl.BlockSpec((tm, LANES), lambda i, j: (i, 0)),
        ],
        out_shape=[
            jax.ShapeDtypeStruct((t, D_MODEL), F32),
            jax.ShapeDtypeStruct((t, Z_COLS), F32),
            jax.ShapeDtypeStruct((t, LANES), F32),
        ],
        scratch_shapes=[pltpu.VMEM((tm, D_MODEL), BF16)],
        compiler_params=pltpu.CompilerParams(
            dimension_semantics=("parallel", "arbitrary"), vmem_limit_bytes=VMEM_LIMIT),
        name="ln_inproj",
    )(x, g, b, w_main, w_lr)


def _chunk_cumsum(x, chunk):
    rows = lax.broadcasted_iota(jnp.int32, x.shape, 0) % chunk
    s = 1
    while s < chunk:
        x = x + jnp.where(rows >= s, pltpu.roll(x, s, 0), 0.0)
        s *= 2
    return x


def _gla_prep(z_ref, lr_ref, lb_ref, wup_ref, bup_ref, qs_ref, ks_ref, bs_ref):
    low = None
    for h in range(A_HEADS):
        cols = slice(h * A_DK, (h + 1) * A_DK)
        lb = lb_ref[:, cols]
        sig, sig_neg = _sigmoid_pair(z_ref[:, Z_FA + h * A_DK:Z_FA + (h + 1) * A_DK])
        b = _chunk_cumsum(jnp.log(lb + (1.0 - lb) * sig), GLA_CHUNK)
        qz = z_ref[:, Z_QA + h * A_DK:Z_QA + (h + 1) * A_DK]
        qs_ref[h] = qz * _sigmoid(qz) * (A_DK ** -0.5)
        ks_ref[h] = (1.0 - lb) * sig_neg
        bs_ref[h] = b
        low = b if low is None else jnp.minimum(low, b)
    u_all = _dot(lr_ref[...].astype(BF16), wup_ref[...]) + bup_ref[...]
    for hb in range(B_HEADS):
        h = A_HEADS + hb
        u = u_all[:, hb * B_DK:(hb + 1) * B_DK]
        log_sig = jnp.minimum(u, 0.0) - jnp.log(1.0 + jnp.exp(-jnp.abs(u)))
        b = _chunk_cumsum(log_sig / GLA_TAU, GLA_CHUNK)
        qs_ref[h] = z_ref[:, Z_QB + hb * B_DK:Z_QB + (hb + 1) * B_DK] * (B_DK ** -0.5)
        ks_ref[h] = z_ref[:, Z_KB + hb * B_DK:Z_KB + (hb + 1) * B_DK]
        bs_ref[h] = b
        low = jnp.minimum(low, b)
    return low


def _exact_scores(qs_ref, ks_ref, bs_ref, sc_ref):
    c = GLA_CHUNK
    n = GLA_TILE // c
    col = lax.broadcasted_iota(jnp.int32, (c, c), 1)

    def per_chunk(idx, carry):
        h = idx // n
        ci = idx - h * n
        r0 = pl.multiple_of(ci * c, c)
        qc = qs_ref[h, pl.ds(r0, c), :]
        bc = bs_ref[h, pl.ds(r0, c), :]

        def column(j, acc):
            kj = ks_ref[h, pl.ds(r0 + j, 1), :]
            bj = bs_ref[h, pl.ds(r0 + j, 1), :]
            term = qc * kj * jnp.exp(jnp.minimum(bc - bj, 0.0))
            return jnp.where(col == j, jnp.sum(term, axis=-1, keepdims=True), acc)

        sc_ref[h, ci] = lax.fori_loop(0, c, column, jnp.zeros((c, c), F32))
        return carry

    lax.fori_loop(0, N_HEADS * n, per_chunk, 0)


def _gla_outputs(z_ref, qs_ref, ks_ref, bs_ref, sta_ref, stb_ref, nga_ref, ngb_ref, o_ref, scores_fn):
    c = GLA_CHUNK
    causal = lax.broadcasted_iota(jnp.int32, (c, c), 0) >= lax.broadcasted_iota(jnp.int32, (c, c), 1)
    for h in range(N_HEADS):
        if h < A_HEADS:
            dv, st_ref, norm_g = A_DV, sta_ref.at[h], nga_ref[...]
            v_off, g_off, o_off = Z_IA + h * A_DV, Z_GA + h * A_DV, h * A_DV
        else:
            hb = h - A_HEADS
            dv, st_ref, norm_g = B_DV, stb_ref.at[hb], ngb_ref[...]
            v_off, g_off, o_off = Z_VB + hb * B_DV, Z_RB + hb * B_DV, A_V + hb * B_DV
        st = st_ref[...]
        for ci in range(GLA_TILE // c):
            rows = slice(ci * c, (ci + 1) * c)
            k = ks_ref[h, rows, :]
            bc = bs_ref[h, rows, :]
            b_last = bc[c - 1:c]
            q1 = (qs_ref[h, rows, :] * jnp.exp(bc)).astype(BF16)
            k_dec = (k * jnp.exp(b_last - bc)).astype(BF16)
            vb = z_ref[rows, v_off:v_off + dv].astype(BF16)
            scores = jnp.where(causal, scores_fn(h, ci, q1, k, bc), 0.0).astype(BF16)
            o = _dot_nt(q1, st.astype(BF16)) + _dot(scores, vb)
            st = st * jnp.exp(b_last) + _dot_tn(vb, k_dec)
            on = o * lax.rsqrt(jnp.mean(o * o, axis=-1, keepdims=True) + RMS_EPS) * norm_g
            gc = z_ref[rows, g_off:g_off + dv]
            o_ref[rows, o_off:o_off + dv] = (on * (gc * _sigmoid(gc))).astype(o_ref.dtype)
        st_ref[...] = st


def _gla_kernel(z_ref, lr_ref, lb_ref, wup_ref, bup_ref, nga_ref, ngb_ref, o_ref,
                sta_ref, stb_ref, qs_ref, ks_ref, bs_ref, sc_ref):
    @pl.when(pl.program_id(1) == 0)
    def _():
        sta_ref[...] = jnp.zeros_like(sta_ref)
        stb_ref[...] = jnp.zeros_like(stb_ref)

    low = _gla_prep(z_ref, lr_ref, lb_ref, wup_ref, bup_ref, qs_ref, ks_ref, bs_ref)
    safe = jnp.min(low) >= GLA_SAFE_LOG_DECAY
    args = (z_ref, qs_ref, ks_ref, bs_ref, sta_ref, stb_ref, nga_ref, ngb_ref, o_ref)

    @pl.when(safe)
    def _product_form():
        _gla_outputs(*args, lambda h, ci, q1, k, bc: _dot_nt(q1, (k * jnp.exp(-bc)).astype(BF16)))

    @pl.when(jnp.logical_not(safe))
    def _exact_form():
        _exact_scores(qs_ref, ks_ref, bs_ref, sc_ref)
        _gla_outputs(*args, lambda h, ci, q1, k, bc: sc_ref[h, ci])


def _gla(z, z_lr, lb, w_up, b_up, norm_a, norm_b, bsz, seq):
    ts = GLA_TILE
    nt = seq // ts
    n_chunks = ts // GLA_CHUNK
    row = lambda b, t: (b * nt + t, 0)
    const = lambda b, t: (0, 0)
    per_head = pltpu.VMEM((N_HEADS, ts, HEAD_DK), F32)
    return pl.pallas_call(
        _gla_kernel,
        grid=(bsz, nt),
        in_specs=[
            pl.BlockSpec((ts, Z_MIXER), row),
            pl.BlockSpec((ts, LANES), row),
            pl.BlockSpec((1, A_HEADS * A_DK), const),
            pl.BlockSpec((LANES, B_HEADS * B_DK), const),
            pl.BlockSpec((1, B_HEADS * B_DK), const),
            pl.BlockSpec((1, A_DV), const),
            pl.BlockSpec((1, B_DV), const),
        ],
        out_specs=pl.BlockSpec((ts, 2 * D_MODEL), row),
        out_shape=jax.ShapeDtypeStruct((bsz * seq, 2 * D_MODEL), BF16),
        scratch_shapes=[
            pltpu.VMEM((A_HEADS, A_DV, A_DK), F32),
            pltpu.VMEM((B_HEADS, B_DV, B_DK), F32),
            per_head, per_head, per_head,
            pltpu.VMEM((N_HEADS, n_chunks, GLA_CHUNK, GLA_CHUNK), F32),
        ],
        compiler_params=pltpu.CompilerParams(
            dimension_semantics=("parallel", "arbitrary"), vmem_limit_bytes=VMEM_LIMIT),
        name="gla",
    )(z, z_lr, lb, w_up, b_up, norm_a, norm_b)


def _lane_pack(cols, dtype):
    m = cols[0].shape[0]
    lane = lax.broadcasted_iota(jnp.int32, (m, LANES), 1)
    out = jnp.zeros((m, LANES), dtype)
    for k, cvals in enumerate(cols):
        out = jnp.where(lane == k, cvals.astype(dtype), out)
    return out


def _post_mixer_kernel(o_ref, ga_ref, gb_ref, h0_ref, wa_ref, wb_ref, wo_ref, lg_ref, lb_ref,
                       wrh_ref, wrl_ref, br_ref, h1_ref, te_ref, tw_ref):
    ya = _dot(o_ref[:, :D_MODEL], wa_ref[...])
    yb = _dot(o_ref[:, D_MODEL:], wb_ref[...])
    merged = _sigmoid(ga_ref[...]) * ya + _sigmoid(gb_ref[...]) * yb
    y = _dot(merged.astype(BF16), wo_ref[...])
    h1 = _layer_norm(DN_ALPHA * h0_ref[...] + y, lg_ref[...], lb_ref[...])
    h1_ref[...] = h1

    hh = h1.astype(BF16)
    hl = (h1 - hh.astype(F32)).astype(BF16)
    logits = _dot(hh, wrh_ref[...]) + (_dot(hl, wrh_ref[...]) + _dot(hh, wrl_ref[...])) + br_ref[...]
    lane = lax.broadcasted_iota(jnp.int32, logits.shape, 1)
    work = jnp.where(lane < N_EXPERTS, logits, -jnp.inf)
    vals, idxs = [], []
    for _ in range(TOP_K):
        m = jnp.max(work, axis=-1, keepdims=True)
        idx = jnp.min(jnp.where(work == m, lane, LANES), axis=-1, keepdims=True)
        vals.append(m)
        idxs.append(idx)
        work = jnp.where(lane == idx, -jnp.inf, work)
    exps = [jnp.exp(v - vals[0]) for v in vals]
    denom = exps[0] + exps[1] + exps[2] + exps[3]
    te_ref[...] = _lane_pack(idxs, jnp.int32)
    tw_ref[...] = _lane_pack([e / denom for e in exps], F32)


def _post_mixer(o, z, h0, w_a, w_b, w_o, ln_g, ln_b, wr_hi, wr_lo, b_r, tm=512):
    t = h0.shape[0]
    row = lambda i: (i, 0)
    const = lambda i: (0, 0)
    full_w = pl.BlockSpec((D_MODEL, D_MODEL), const)
    vec = pl.BlockSpec((1, D_MODEL), const)
    return pl.pallas_call(
        _post_mixer_kernel,
        grid=(t // tm,),
        in_specs=[
            pl.BlockSpec((tm, 2 * D_MODEL), row),
            pl.BlockSpec((tm, D_MODEL), lambda i: (i, Z_GATE_A // D_MODEL)),
            pl.BlockSpec((tm, D_MODEL), lambda i: (i, Z_GATE_B // D_MODEL)),
            pl.BlockSpec((tm, D_MODEL), row),
            full_w, full_w, full_w, vec, vec,
            pl.BlockSpec((D_MODEL, LANES), const),
            pl.BlockSpec((D_MODEL, LANES), const),
            pl.BlockSpec((1, LANES), const),
        ],
        out_specs=[
            pl.BlockSpec((tm, D_MODEL), row),
            pl.BlockSpec((tm, LANES), row),
            pl.BlockSpec((tm, LANES), row),
        ],
        out_shape=[
            jax.ShapeDtypeStruct((t, D_MODEL), F32),
            jax.ShapeDtypeStruct((t, LANES), jnp.int32),
            jax.ShapeDtypeStruct((t, LANES), F32),
        ],
        compiler_params=pltpu.CompilerParams(
            dimension_semantics=("parallel",), vmem_limit_bytes=VMEM_LIMIT),
        name="post_mixer",
    )(o, z, z, h0, w_a, w_b, w_o, ln_g, ln_b, wr_hi, wr_lo, b_r)


def _rank_kernel(te_ref, rank_ref, cnt_ref, carry_ref):
    @pl.when(pl.program_id(0) == 0)
    def _():
        carry_ref[...] = jnp.zeros_like(carry_ref)

    te = te_ref[...]
    tr = te.shape[0]
    lane = lax.broadcasted_iota(jnp.int32, (tr, LANES), 1)
    onehots = [te[:, k:k + 1] == lane for k in range(TOP_K)]
    cnt = jnp.zeros((tr, LANES), F32)
    for oh in onehots:
        cnt = cnt + oh.astype(F32)
    strict_lower = (lax.broadcasted_iota(jnp.int32, (tr, tr), 0)
                    > lax.broadcasted_iota(jnp.int32, (tr, tr), 1))
    earlier = _dot(strict_lower.astype(BF16), cnt.astype(BF16)) + carry_ref[...]
    ranks = [jnp.sum(jnp.where(oh, earlier, 0.0), axis=-1, keepdims=True) for oh in onehots]
    rank_ref[...] = _lane_pack(ranks, jnp.int32)
    carry_ref[...] = carry_ref[...] + jnp.sum(cnt, axis=0, keepdims=True)
    cnt_ref[...] = carry_ref[...]


def _rank(top_e, tr=512):
    t = top_e.shape[0]
    return pl.pallas_call(
        _rank_kernel,
        grid=(t // tr,),
        in_specs=[pl.BlockSpec((tr, LANES), lambda i: (i, 0))],
        out_specs=[
            pl.BlockSpec((tr, LANES), lambda i: (i, 0)),
            pl.BlockSpec((1, LANES), lambda i: (0, 0)),
        ],
        out_shape=[
            jax.ShapeDtypeStruct((t, LANES), jnp.int32),
            jax.ShapeDtypeStruct((1, LANES), F32),
        ],
        scratch_shapes=[pltpu.VMEM((1, LANES), F32)],
        compiler_params=pltpu.CompilerParams(dimension_semantics=("arbitrary",)),
        name="rank",
    )(top_e)


def _row(ref, r):
    return ref.at[pl.ds(r, 1), :]


def _wait_rows(ref_hbm, n, sem):
    pltpu.make_async_copy(ref_hbm.at[pl.ds(0, n), :], ref_hbm.at[pl.ds(0, n), :], sem).wait()


def _dispatch_kernel(pad_start_ref, pad_cnt_ref, nu_ref, dest_ref, h_ref, xs_hbm, zero_ref, sem):
    n_tok = h_ref.shape[0]

    @pl.when(pl.program_id(0) == 0)
    def _zero_rows_without_token():
        zero_ref[...] = jnp.zeros_like(zero_ref)

        def per_expert(e, carry):
            def start(r, c):
                pltpu.make_async_copy(_row(zero_ref, 0), _row(xs_hbm, pad_start_ref[e] + r), sem).start()
                return c

            def wait(r, c):
                _wait_rows(xs_hbm, 1, sem)
                return c

            lax.fori_loop(0, pad_cnt_ref[e], start, 0)
            lax.fori_loop(0, pad_cnt_ref[e], wait, 0)
            return carry

        lax.fori_loop(0, N_EXPERTS, per_expert, 0)

        def block_copy(blk):
            r0 = pl.multiple_of(blk * MOE_ROWS, MOE_ROWS)
            return pltpu.make_async_copy(zero_ref, xs_hbm.at[pl.ds(r0, MOE_ROWS), :], sem)

        def start_block(blk, c):
            block_copy(blk).start()
            return c

        def wait_block(blk, c):
            block_copy(blk).wait()
            return c

        n_blocks = xs_hbm.shape[0] // MOE_ROWS
        lax.fori_loop(nu_ref[0], n_blocks, start_block, 0)
        lax.fori_loop(nu_ref[0], n_blocks, wait_block, 0)

    def issue(r, carry):
        for k in range(TOP_K):
            pltpu.make_async_copy(_row(h_ref, r), _row(xs_hbm, dest_ref[r * TOP_K + k]), sem).start(
                priority=k % DMA_PRIORITIES)
        return carry

    lax.fori_loop(0, n_tok, issue, 0)
    for _ in range(n_tok * TOP_K // DMA_BATCH):
        _wait_rows(xs_hbm, DMA_BATCH, sem)


def _dispatch(pad_start, pad_cnt, n_used, dest_flat, h1, n_rows, tok_per_step=256):
    t = h1.shape[0]
    grid_spec = pltpu.PrefetchScalarGridSpec(
        num_scalar_prefetch=3,
        grid=(t // tok_per_step,),
        in_specs=[
            pl.BlockSpec((tok_per_step * TOP_K,), lambda i, ps, pc, nu: (i,), memory_space=pltpu.SMEM),
            pl.BlockSpec((tok_per_step, D_MODEL), lambda i, ps, pc, nu: (i, 0)),
        ],
        out_specs=pl.BlockSpec(memory_space=pl.ANY),
        scratch_shapes=[pltpu.VMEM((MOE_ROWS, D_MODEL), F32), pltpu.SemaphoreType.DMA(())],
    )
    return pl.pallas_call(
        _dispatch_kernel,
        grid_spec=grid_spec,
        out_shape=jax.ShapeDtypeStruct((n_rows, D_MODEL), F32),
        compiler_params=pltpu.CompilerParams(dimension_semantics=("arbitrary",)),
        name="dispatch",
    )(pad_start, pad_cnt, n_used, dest_flat, h1)


def _experts_kernel(be_ref, nu_ref, x_ref, wgu_ref, bgu_ref, wd_ref, bd_ref, y_ref):
    del be_ref

    used = pl.program_id(0) < nu_ref[0]

    @pl.when(used)
    def _():
        hgu = _dot(x_ref[...].astype(BF16), wgu_ref[0]) + bgu_ref[0]
        gate = jnp.minimum(hgu[:, :D_FF], SWIGLU_LIMIT)
        up = jnp.clip(hgu[:, D_FF:], -SWIGLU_LIMIT, SWIGLU_LIMIT)
        act = (up + 1.0) * gate * _sigmoid(SWIGLU_ALPHA * gate)
        y_ref[...] = _dot(act.astype(BF16), wd_ref[0]) + bd_ref[0]

    @pl.when(jnp.logical_not(used))
    def _():
        y_ref[...] = jnp.zeros_like(y_ref)


def _experts(block_e, n_used, xs, w_gu, b_gu, w_d, b_d):
    n_rows = xs.shape[0]
    n_blocks = n_rows // MOE_ROWS
    grid_spec = pltpu.PrefetchScalarGridSpec(
        num_scalar_prefetch=2,
        grid=(n_blocks,),
        in_specs=[
            pl.BlockSpec((MOE_ROWS, D_MODEL), lambda i, be, nu: (jnp.minimum(i, nu[0] - 1), 0)),
            pl.BlockSpec((1, D_MODEL, 2 * D_FF), lambda i, be, nu: (be[i], 0, 0)),
            pl.BlockSpec((1, 1, 2 * D_FF), lambda i, be, nu: (be[i], 0, 0)),
            pl.BlockSpec((1, D_FF, D_MODEL), lambda i, be, nu: (be[i], 0, 0)),
            pl.BlockSpec((1, 1, D_MODEL), lambda i, be, nu: (be[i], 0, 0)),
        ],
        out_specs=pl.BlockSpec((MOE_ROWS, D_MODEL), lambda i, be, nu: (i, 0)),
    )
    return pl.pallas_call(
        _experts_kernel,
        grid_spec=grid_spec,
        out_shape=jax.ShapeDtypeStruct((n_rows, D_MODEL), F32),
        compiler_params=pltpu.CompilerParams(
            dimension_semantics=("arbitrary",), vmem_limit_bytes=VMEM_LIMIT),
        name="experts",
    )(block_e, n_used, xs, w_gu, b_gu, w_d, b_d)


def _final_kernel(dest_ref, ys_hbm, tw_ref, h1_ref, p_ref, wpg_ref, wpp_ref, lg_ref, lb_ref, o_ref,
                  g_ref, sem):
    tm = h1_ref.shape[0]

    def issue(r, carry):
        for k in range(TOP_K):
            pltpu.make_async_copy(_row(ys_hbm, dest_ref[r * TOP_K + k]), _row(g_ref.at[k], r), sem).start(
                priority=k % DMA_PRIORITIES)
        return carry

    lax.fori_loop(0, tm, issue, 0)
    h1 = h1_ref[...]
    ple = _sigmoid(_dot(h1.astype(BF16), wpg_ref[...])) * _dot(p_ref[...].astype(BF16), wpp_ref[...])
    acc = DN_ALPHA * h1 + ple
    for _ in range(tm * TOP_K // DMA_BATCH):
        _wait_rows(ys_hbm, DMA_BATCH, sem)
    tw = tw_ref[...]
    for k in range(TOP_K):
        acc = acc + tw[:, k:k + 1] * g_ref[k]
    o_ref[...] = _layer_norm(acc, lg_ref[...], lb_ref[...])


def _final(dest_flat, ys, top_w, h1, p, w_pg, w_pp, ln_g, ln_b, tm=256):
    t = h1.shape[0]
    row = lambda i: (i, 0)
    const = lambda i: (0, 0)
    return pl.pallas_call(
        _final_kernel,
        grid=(t // tm,),
        in_specs=[
            pl.BlockSpec((tm * TOP_K,), lambda i: (i,), memory_space=pltpu.SMEM),
            pl.BlockSpec(memory_space=pl.ANY),
            pl.BlockSpec((tm, LANES), row),
            pl.BlockSpec((tm, D_MODEL), row),
            pl.BlockSpec((tm, PLE_DIM), row),
            pl.BlockSpec((D_MODEL, D_MODEL), const),
            pl.BlockSpec((PLE_DIM, D_MODEL), const),
            pl.BlockSpec((1, D_MODEL), const),
            pl.BlockSpec((1, D_MODEL), const),
        ],
        out_specs=pl.BlockSpec((tm, D_MODEL), row),
        out_shape=jax.ShapeDtypeStruct((t, D_MODEL), F32),
        scratch_shapes=[pltpu.VMEM((TOP_K, tm, D_MODEL), F32), pltpu.SemaphoreType.DMA(())],
        compiler_params=pltpu.CompilerParams(
            dimension_semantics=("arbitrary",), vmem_limit_bytes=VMEM_LIMIT),
        name="final",
    )(dest_flat, ys, top_w, h1, p, w_pg, w_pp, ln_g, ln_b)


def kernel(x, p, emb_ln_g, emb_ln_b, hgrn_lb, w_in, w_gla_up, b_gla_up, norm_a_g, norm_b_g, w_proj_a, w_proj_b, w_out, ln_mix_g, ln_mix_b, w_router, b_router, w_gate_up, b_gate_up, w_down, b_down, w_ple_gate, w_ple_proj, ln_moe_g, ln_moe_b):
    bsz, seq, d = x.shape
    t = bsz * seq
    assert d == D_MODEL and seq % GLA_TILE == 0 and t % 1024 == 0
    assert w_in.shape[0] == DEPTH == 1
    vec = lambda a: a.reshape(1, -1).astype(F32)

    lb0 = jax.nn.softmax(hgrn_lb.astype(F32), axis=0)[0:1]
    w_in0 = w_in[0]
    w_main = jnp.concatenate([w_in0[:, :W_IN_LR], w_in0[:, W_IN_LR + GLA_RANK:]], axis=1).astype(BF16)
    w_lr = jnp.pad(w_in0[:, W_IN_LR:W_IN_LR + GLA_RANK], ((0, 0), (0, LANES - GLA_RANK))).astype(BF16)
    w_up = jnp.pad(w_gla_up[0], ((0, LANES - GLA_RANK), (0, 0))).astype(BF16)
    wr = jnp.pad(w_router[0].astype(F32), ((0, 0), (0, LANES - N_EXPERTS)))
    wr_hi = wr.astype(BF16)
    wr_lo = (wr - wr_hi.astype(F32)).astype(BF16)
    b_r = jnp.pad(b_router[0].astype(F32), (0, LANES - N_EXPERTS)).reshape(1, LANES)

    xt = x.reshape(t, d)
    h0, z, z_lr = _ln_inproj(xt, vec(emb_ln_g), vec(emb_ln_b), w_main, w_lr)
    o = _gla(z, z_lr, lb0, w_up, vec(b_gla_up[0]), vec(norm_a_g[0]), vec(norm_b_g[0]), bsz, seq)
    h1, top_e, top_w = _post_mixer(
        o, z, h0, w_proj_a[0].astype(BF16), w_proj_b[0].astype(BF16), w_out[0].astype(BF16),
        vec(ln_mix_g[0]), vec(ln_mix_b[0]), wr_hi, wr_lo, b_r)

    rank, cnt = _rank(top_e)
    counts = cnt[0, :N_EXPERTS].astype(jnp.int32)
    padded = (counts + MOE_ROWS - 1) // MOE_ROWS * MOE_ROWS
    pend = jnp.cumsum(padded)
    pstart = pend - padded
    n_blocks = t * TOP_K // MOE_ROWS + N_EXPERTS
    n_rows = n_blocks * MOE_ROWS
    n_used = pend[N_EXPERTS - 1:] // MOE_ROWS
    dest = (pstart[top_e[:, :TOP_K]] + rank[:, :TOP_K]).reshape(-1)
    block_pos = jnp.minimum(jnp.arange(n_blocks, dtype=jnp.int32), n_used - 1) * MOE_ROWS
    block_e = jnp.minimum(
        jnp.sum((pend[None, :] <= block_pos[:, None]).astype(jnp.int32), axis=1), N_EXPERTS - 1)

    xs = _dispatch(pstart + counts, padded - counts, n_used, dest, h1, n_rows)
    ys = _experts(block_e, n_used, xs, w_gate_up[0].astype(BF16),
                  b_gate_up[0].reshape(N_EXPERTS, 1, 2 * D_FF).astype(F32),
                  w_down[0].astype(BF16), b_down[0].reshape(N_EXPERTS, 1, d).astype(F32))
    out = _final(dest, ys, top_w, h1, p[0].reshape(t, PLE_DIM),
                 w_ple_gate[0].astype(BF16), w_ple_proj[0].astype(BF16),
                 vec(ln_moe_g[0]), vec(ln_moe_b[0]))
    return out.reshape(bsz, seq, d)
```

```python
import jax
import jax.numpy as jnp
from jax import lax
from jax.experimental import pallas as pl
from jax.experimental.pallas import tpu as pltpu

F32 = jnp.float32
BF16 = jnp.bfloat16

D_MODEL = 1024
A_HEADS, A_DK, A_DV = 8, 128, 128
B_HEADS, B_DK, B_DV = 4, 128, 256
N_HEADS = A_HEADS + B_HEADS
HEAD_DK = 128
A_V = A_HEADS * A_DV
GLA_RANK = 16
GLA_TAU = 16.0
N_EXPERTS = 32
TOP_K = 4
D_FF = 1024
SWIGLU_LIMIT = 7.0
SWIGLU_ALPHA = 1.702
PLE_DIM = 256
LN_EPS = 1e-5
RMS_EPS = 1e-6
DEPTH = 1
DN_ALPHA = (2.0 * DEPTH) ** 0.25

LANES = 128
GLA_CHUNK = 64
GLA_TILE = 256
GLA_SAFE_LOG_DECAY = -60.0
MOE_ROWS = 512
DMA_BATCH = 128
VMEM_LIMIT = 56 * 1024 * 1024

Z_QA, Z_FA, Z_IA, Z_GA = 0, 1024, 2048, 3072
Z_QB, Z_KB, Z_VB, Z_RB = 4096, 4608, 5120, 6144
Z_MIXER = 7168
Z_GATE_A, Z_GATE_B = 7168, 8192
Z_COLS = 9216
W_IN_LR = 7168


def _layer_norm(x, g, b):
    mu = jnp.mean(x, axis=-1, keepdims=True)
    xc = x - mu
    var = jnp.mean(xc * xc, axis=-1, keepdims=True)
    return xc * lax.rsqrt(var + LN_EPS) * g + b


def _sigmoid(x):
    return jax.nn.sigmoid(x)


def _sigmoid_pair(x):
    e = jnp.exp(-jnp.abs(x))
    r = 1.0 / (1.0 + e)
    er = e * r
    pos = x >= 0.0
    return jnp.where(pos, r, er), jnp.where(pos, er, r)


def _dot(a, b):
    return jnp.dot(a, b, preferred_element_type=F32)


def _dot_nt(a, b):
    return lax.dot_general(a, b, (((1,), (1,)), ((), ())), preferred_element_type=F32)


def _dot_tn(a, b):
    return lax.dot_general(a, b, (((0,), (0,)), ((), ())), preferred_element_type=F32)


def _ln_inproj_kernel(x_ref, g_ref, b_ref, w_ref, wlr_ref, h_ref, z_ref, zlr_ref, hb_ref):
    @pl.when(pl.program_id(1) == 0)
    def _():
        h = _layer_norm(x_ref[...], g_ref[...], b_ref[...])
        h_ref[...] = h
        hb = h.astype(BF16)
        hb_ref[...] = hb
        zlr_ref[...] = _dot(hb, wlr_ref[...])

    z_ref[...] = _dot(hb_ref[...], w_ref[...])


def _ln_inproj(x, g, b, w_main, w_lr, tm=1024, tn=1024):
    t = x.shape[0]
    return pl.pallas_call(
        _ln_inproj_kernel,
        grid=(t // tm, Z_COLS // tn),
        in_specs=[
            pl.BlockSpec((tm, D_MODEL), lambda i, j: (i, 0)),
            pl.BlockSpec((1, D_MODEL), lambda i, j: (0, 0)),
            pl.BlockSpec((1, D_MODEL), lambda i, j: (0, 0)),
            pl.BlockSpec((D_MODEL, tn), lambda i, j: (0, j)),
            pl.BlockSpec((D_MODEL, LANES), lambda i, j: (0, 0)),
        ],
        out_specs=[
            pl.BlockSpec((tm, D_MODEL), lambda i, j: (i, 0)),
            pl.BlockSpec((tm, tn), lambda i, j: (i, j)),
            pl.BlockSpec((tm, LANES), lambda i, j: (i, 0)),
        ],
        out_shape=[
            jax.ShapeDtypeStruct((t, D_MODEL), F32),
            jax.ShapeDtypeStruct((t, Z_COLS), F32),
            jax.ShapeDtypeStruct((t, LANES), F32),
        ],
        scratch_shapes=[pltpu.VMEM((tm, D_MODEL), BF16)],
        compiler_params=pltpu.CompilerParams(
            dimension_semantics=("parallel", "arbitrary"), vmem_limit_bytes=VMEM_LIMIT),
        name="ln_inproj",
    )(x, g, b, w_main, w_lr)


def _chunk_tri(tile, chunk):
    i = jnp.arange(tile)[:, None]
    j = jnp.arange(tile)[None, :]
    return ((j <= i) & (i // chunk == j // chunk)).astype(BF16)


def _chunk_cumsum(tri, x):
    hi = x.astype(BF16)
    lo = (x - hi.astype(F32)).astype(BF16)
    return _dot(tri, hi) + _dot(tri, lo)


def _gla_prep(z_ref, lr_ref, lb_ref, wup_ref, bup_ref, tri_ref, qs_ref, q1_ref, ks_ref, bs_ref):
    low = None
    tri = tri_ref[...]

    def put(h, q, k, b):
        qs_ref[h] = q
        q1_ref[h] = (q * jnp.exp(b)).astype(BF16)
        ks_ref[h] = k
        bs_ref[h] = b

    for h in range(A_HEADS):
        cols = slice(h * A_DK, (h + 1) * A_DK)
        lb = lb_ref[:, cols]
        sig, sig_neg = _sigmoid_pair(z_ref[:, Z_FA + h * A_DK:Z_FA + (h + 1) * A_DK])
        b = _chunk_cumsum(tri, jnp.log(lb + (1.0 - lb) * sig))
        qz = z_ref[:, Z_QA + h * A_DK:Z_QA + (h + 1) * A_DK]
        put(h, qz * _sigmoid(qz) * (A_DK ** -0.5), (1.0 - lb) * sig_neg, b)
        low = b if low is None else jnp.minimum(low, b)
    u_all = _dot(lr_ref[...].astype(BF16), wup_ref[...]) + bup_ref[...]
    for hb in range(B_HEADS):
        u = u_all[:, hb * B_DK:(hb + 1) * B_DK]
        log_sig = jnp.minimum(u, 0.0) - jnp.log(1.0 + jnp.exp(-jnp.abs(u)))
        b = _chunk_cumsum(tri, log_sig / GLA_TAU)
        put(A_HEADS + hb, z_ref[:, Z_QB + hb * B_DK:Z_QB + (hb + 1) * B_DK] * (B_DK ** -0.5),
            z_ref[:, Z_KB + hb * B_DK:Z_KB + (hb + 1) * B_DK], b)
        low = jnp.minimum(low, b)
    return low


def _product_scores(q1_ref, ks_ref, bs_ref, sc_ref):
    c = GLA_CHUNK
    for h in range(N_HEADS):
        k1 = (ks_ref[h] * jnp.exp(-bs_ref[h])).astype(BF16)
        for ci in range(GLA_TILE // c):
            rows = slice(ci * c, (ci + 1) * c)
            sc_ref[h, ci] = _dot_nt(q1_ref[h, rows, :], k1[rows])


def _exact_scores(qs_ref, ks_ref, bs_ref, sc_ref):
    c = GLA_CHUNK
    n = GLA_TILE // c
    col = lax.broadcasted_iota(jnp.int32, (c, c), 1)

    def per_chunk(idx, carry):
        h = idx // n
        ci = idx - h * n
        r0 = pl.multiple_of(ci * c, c)
        qc = qs_ref[h, pl.ds(r0, c), :]
        bc = bs_ref[h, pl.ds(r0, c), :]

        def column(j, acc):
            kj = ks_ref[h, pl.ds(r0 + j, 1), :]
            bj = bs_ref[h, pl.ds(r0 + j, 1), :]
            term = qc * kj * jnp.exp(jnp.minimum(bc - bj, 0.0))
            return jnp.where(col == j, jnp.sum(term, axis=-1, keepdims=True), acc)

        sc_ref[h, ci] = lax.fori_loop(0, c, column, jnp.zeros((c, c), F32))
        return carry

    lax.fori_loop(0, N_HEADS * n, per_chunk, 0)


def _gla_outputs(z_ref, q1_ref, ks_ref, bs_ref, sc_ref, sta_ref, stb_ref, nga_ref, ngb_ref, o_ref):
    c = GLA_CHUNK
    causal = lax.broadcasted_iota(jnp.int32, (c, c), 0) >= lax.broadcasted_iota(jnp.int32, (c, c), 1)
    for ci in range(GLA_TILE // c):
        rows = slice(ci * c, (ci + 1) * c)
        for h in range(N_HEADS):
            if h < A_HEADS:
                dv, st_ref, norm_g = A_DV, sta_ref.at[h], nga_ref[...]
                v_off, g_off, o_off = Z_IA + h * A_DV, Z_GA + h * A_DV, h * A_DV
            else:
                hb = h - A_HEADS
                dv, st_ref, norm_g = B_DV, stb_ref.at[hb], ngb_ref[...]
                v_off, g_off, o_off = Z_VB + hb * B_DV, Z_RB + hb * B_DV, A_V + hb * B_DV
            st = st_ref[...]
            k = ks_ref[h, rows, :]
            bc = bs_ref[h, rows, :]
            b_last = bc[c - 1:c]
            k_dec = (k * jnp.exp(b_last - bc)).astype(BF16)
            vb = z_ref[rows, v_off:v_off + dv].astype(BF16)
            scores = jnp.where(causal, sc_ref[h, ci], 0.0).astype(BF16)
            o = _dot_nt(q1_ref[h, rows, :], st.astype(BF16)) + _dot(scores, vb)
            st_ref[...] = st * jnp.exp(b_last) + _dot_tn(vb, k_dec)
            on = o * lax.rsqrt(jnp.mean(o * o, axis=-1, keepdims=True) + RMS_EPS) * norm_g
            gc = z_ref[rows, g_off:g_off + dv]
            o_ref[rows, o_off:o_off + dv] = (on * (gc * _sigmoid(gc))).astype(o_ref.dtype)


def _gla_kernel(z_ref, lr_ref, lb_ref, wup_ref, bup_ref, tri_ref, nga_ref, ngb_ref, o_ref,
                sta_ref, stb_ref, qs_ref, q1_ref, ks_ref, bs_ref, sc_ref):
    @pl.when(pl.program_id(1) == 0)
    def _():
        sta_ref[...] = jnp.zeros_like(sta_ref)
        stb_ref[...] = jnp.zeros_like(stb_ref)

    low = _gla_prep(z_ref, lr_ref, lb_ref, wup_ref, bup_ref, tri_ref, qs_ref, q1_ref, ks_ref, bs_ref)
    safe = jnp.min(low) >= GLA_SAFE_LOG_DECAY

    @pl.when(safe)
    def _product_form():
        _product_scores(q1_ref, ks_ref, bs_ref, sc_ref)

    @pl.when(jnp.logical_not(safe))
    def _exact_form():
        _exact_scores(qs_ref, ks_ref, bs_ref, sc_ref)

    _gla_outputs(z_ref, q1_ref, ks_ref, bs_ref, sc_ref, sta_ref, stb_ref, nga_ref, ngb_ref, o_ref)


def _gla(z, z_lr, lb, w_up, b_up, norm_a, norm_b, bsz, seq):
    ts = GLA_TILE
    nt = seq // ts
    n_chunks = ts // GLA_CHUNK
    row = lambda b, t: (b * nt + t, 0)
    const = lambda b, t: (0, 0)
    per_head = pltpu.VMEM((N_HEADS, ts, HEAD_DK), F32)
    return pl.pallas_call(
        _gla_kernel,
        grid=(bsz, nt),
        in_specs=[
            pl.BlockSpec((ts, Z_MIXER), row),
            pl.BlockSpec((ts, LANES), row),
            pl.BlockSpec((1, A_HEADS * A_DK), const),
            pl.BlockSpec((LANES, B_HEADS * B_DK), const),
            pl.BlockSpec((1, B_HEADS * B_DK), const),
            pl.BlockSpec((ts, ts), const),
            pl.BlockSpec((1, A_DV), const),
            pl.BlockSpec((1, B_DV), const),
        ],
        out_specs=pl.BlockSpec((ts, 2 * D_MODEL), row),
        out_shape=jax.ShapeDtypeStruct((bsz * seq, 2 * D_MODEL), BF16),
        scratch_shapes=[
            pltpu.VMEM((A_HEADS, A_DV, A_DK), F32),
            pltpu.VMEM((B_HEADS, B_DV, B_DK), F32),
            per_head, pltpu.VMEM((N_HEADS, ts, HEAD_DK), BF16), per_head, per_head,
            pltpu.VMEM((N_HEADS, n_chunks, GLA_CHUNK, GLA_CHUNK), F32),
        ],
        compiler_params=pltpu.CompilerParams(
            dimension_semantics=("parallel", "arbitrary"), vmem_limit_bytes=VMEM_LIMIT),
        name="gla",
    )(z, z_lr, lb, w_up, b_up, _chunk_tri(ts, GLA_CHUNK), norm_a, norm_b)


def _lane_pack(cols, dtype):
    m = cols[0].shape[0]
    lane = lax.broadcasted_iota(jnp.int32, (m, LANES), 1)
    out = jnp.zeros((m, LANES), dtype)
    for k, cvals in enumerate(cols):
        out = jnp.where(lane == k, cvals.astype(dtype), out)
    return out


def _post_mixer_kernel(o_ref, ga_ref, gb_ref, h0_ref, wa_ref, wb_ref, wo_ref, lg_ref, lb_ref,
                       wrh_ref, wrl_ref, br_ref, h1_ref, te_ref, tw_ref):
    ya = _dot(o_ref[:, :D_MODEL], wa_ref[...])
    yb = _dot(o_ref[:, D_MODEL:], wb_ref[...])
    merged = _sigmoid(ga_ref[...]) * ya + _sigmoid(gb_ref[...]) * yb
    y = _dot(merged.astype(BF16), wo_ref[...])
    h1 = _layer_norm(DN_ALPHA * h0_ref[...] + y, lg_ref[...], lb_ref[...])
    h1_ref[...] = h1

    hh = h1.astype(BF16)
    hl = (h1 - hh.astype(F32)).astype(BF16)
    logits = _dot(hh, wrh_ref[...]) + (_dot(hl, wrh_ref[...]) + _dot(hh, wrl_ref[...])) + br_ref[...]
    lane = lax.broadcasted_iota(jnp.int32, logits.shape, 1)
    work = jnp.where(lane < N_EXPERTS, logits, -jnp.inf)
    vals, idxs = [], []
    for _ in range(TOP_K):
        m = jnp.max(work, axis=-1, keepdims=True)
        idx = jnp.min(jnp.where(work == m, lane, LANES), axis=-1, keepdims=True)
        vals.append(m)
        idxs.append(idx)
        work = jnp.where(lane == idx, -jnp.inf, work)
    exps = [jnp.exp(v - vals[0]) for v in vals]
    denom = exps[0] + exps[1] + exps[2] + exps[3]
    te_ref[...] = _lane_pack(idxs, jnp.int32)
    tw_ref[...] = _lane_pack([e / denom for e in exps], F32)


def _post_mixer(o, z, h0, w_a, w_b, w_o, ln_g, ln_b, wr_hi, wr_lo, b_r, tm=512):
    t = h0.shape[0]
    row = lambda i: (i, 0)
    const = lambda i: (0, 0)
    full_w = pl.BlockSpec((D_MODEL, D_MODEL), const)
    vec = pl.BlockSpec((1, D_MODEL), const)
    return pl.pallas_call(
        _post_mixer_kernel,
        grid=(t // tm,),
        in_specs=[
            pl.BlockSpec((tm, 2 * D_MODEL), row),
            pl.BlockSpec((tm, D_MODEL), lambda i: (i, Z_GATE_A // D_MODEL)),
            pl.BlockSpec((tm, D_MODEL), lambda i: (i, Z_GATE_B // D_MODEL)),
            pl.BlockSpec((tm, D_MODEL), row),
            full_w, full_w, full_w, vec, vec,
            pl.BlockSpec((D_MODEL, LANES), const),
            pl.BlockSpec((D_MODEL, LANES), const),
            pl.BlockSpec((1, LANES), const),
        ],
        out_specs=[
            pl.BlockSpec((tm, D_MODEL), row),
            pl.BlockSpec((tm, LANES), row),
            pl.BlockSpec((tm, LANES), row),
        ],
        out_shape=[
            jax.ShapeDtypeStruct((t, D_MODEL), F32),
            jax.ShapeDtypeStruct((t, LANES), jnp.int32),
            jax.ShapeDtypeStruct((t, LANES), F32),
        ],
        compiler_params=pltpu.CompilerParams(
            dimension_semantics=("parallel",), vmem_limit_bytes=VMEM_LIMIT),
        name="post_mixer",
    )(o, z, z, h0, w_a, w_b, w_o, ln_g, ln_b, wr_hi, wr_lo, b_r)


def _rank_kernel(te_ref, rank_ref, cnt_ref, carry_ref):
    @pl.when(pl.program_id(0) == 0)
    def _():
        carry_ref[...] = jnp.zeros_like(carry_ref)

    te = te_ref[...]
    tr = te.shape[0]
    lane = lax.broadcasted_iota(jnp.int32, (tr, LANES), 1)
    onehots = [te[:, k:k + 1] == lane for k in range(TOP_K)]
    cnt = jnp.zeros((tr, LANES), F32)
    for oh in onehots:
        cnt = cnt + oh.astype(F32)
    strict_lower = (lax.broadcasted_iota(jnp.int32, (tr, tr), 0)
                    > lax.broadcasted_iota(jnp.int32, (tr, tr), 1))
    earlier = _dot(strict_lower.astype(BF16), cnt.astype(BF16)) + carry_ref[...]
    ranks = [jnp.sum(jnp.where(oh, earlier, 0.0), axis=-1, keepdims=True) for oh in onehots]
    rank_ref[...] = _lane_pack(ranks, jnp.int32)
    carry_ref[...] = carry_ref[...] + jnp.sum(cnt, axis=0, keepdims=True)
    cnt_ref[...] = carry_ref[...]


def _rank(top_e, tr=512):
    t = top_e.shape[0]
    return pl.pallas_call(
        _rank_kernel,
        grid=(t // tr,),
        in_specs=[pl.BlockSpec((tr, LANES), lambda i: (i, 0))],
        out_specs=[
            pl.BlockSpec((tr, LANES), lambda i: (i, 0)),
            pl.BlockSpec((1, LANES), lambda i: (0, 0)),
        ],
        out_shape=[
            jax.ShapeDtypeStruct((t, LANES), jnp.int32),
            jax.ShapeDtypeStruct((1, LANES), F32),
        ],
        scratch_shapes=[pltpu.VMEM((1, LANES), F32)],
        compiler_params=pltpu.CompilerParams(dimension_semantics=("arbitrary",)),
        name="rank",
    )(top_e)


def _row(ref, r):
    return ref.at[pl.ds(r, 1), :]


def _wait_rows(ref_hbm, n, sem):
    pltpu.make_async_copy(ref_hbm.at[pl.ds(0, n), :], ref_hbm.at[pl.ds(0, n), :], sem).wait()


def _dispatch_kernel(pad_start_ref, pad_cnt_ref, nu_ref, dest_ref, h_ref, xs_hbm, zero_ref, sem):
    n_tok = h_ref.shape[0]

    @pl.when(pl.program_id(0) == 0)
    def _zero_rows_without_token():
        zero_ref[...] = jnp.zeros_like(zero_ref)

        def per_expert(e, carry):
            def start(r, c):
                pltpu.make_async_copy(_row(zero_ref, 0), _row(xs_hbm, pad_start_ref[e] + r), sem).start()
                return c

            def wait(r, c):
                _wait_rows(xs_hbm, 1, sem)
                return c

            lax.fori_loop(0, pad_cnt_ref[e], start, 0)
            lax.fori_loop(0, pad_cnt_ref[e], wait, 0)
            return carry

        lax.fori_loop(0, N_EXPERTS, per_expert, 0)

        def block_copy(blk):
            r0 = pl.multiple_of(blk * MOE_ROWS, MOE_ROWS)
            return pltpu.make_async_copy(zero_ref, xs_hbm.at[pl.ds(r0, MOE_ROWS), :], sem)

        def start_block(blk, c):
            block_copy(blk).start()
            return c

        def wait_block(blk, c):
            block_copy(blk).wait()
            return c

        n_blocks = xs_hbm.shape[0] // MOE_ROWS
        lax.fori_loop(nu_ref[0], n_blocks, start_block, 0)
        lax.fori_loop(nu_ref[0], n_blocks, wait_block, 0)

    def issue(r, carry):
        for k in range(TOP_K):
            pltpu.make_async_copy(_row(h_ref, r), _row(xs_hbm, dest_ref[r * TOP_K + k]), sem).start()
        return carry

    lax.fori_loop(0, n_tok, issue, 0)
    for _ in range(n_tok * TOP_K // DMA_BATCH):
        _wait_rows(xs_hbm, DMA_BATCH, sem)


def _dispatch(pad_start, pad_cnt, n_used, dest_flat, h1, n_rows, tok_per_step=256):
    t = h1.shape[0]
    grid_spec = pltpu.PrefetchScalarGridSpec(
        num_scalar_prefetch=3,
        grid=(t // tok_per_step,),
        in_specs=[
            pl.BlockSpec((tok_per_step * TOP_K,), lambda i, ps, pc, nu: (i,), memory_space=pltpu.SMEM),
            pl.BlockSpec((tok_per_step, D_MODEL), lambda i, ps, pc, nu: (i, 0)),
        ],
        out_specs=pl.BlockSpec(memory_space=pl.ANY),
        scratch_shapes=[pltpu.VMEM((MOE_ROWS, D_MODEL), F32), pltpu.SemaphoreType.DMA(())],
    )
    return pl.pallas_call(
        _dispatch_kernel,
        grid_spec=grid_spec,
        out_shape=jax.ShapeDtypeStruct((n_rows, D_MODEL), F32),
        compiler_params=pltpu.CompilerParams(dimension_semantics=("arbitrary",)),
        name="dispatch",
    )(pad_start, pad_cnt, n_used, dest_flat, h1)


def _experts_kernel(be_ref, nu_ref, x_ref, wgu_ref, bgu_ref, wd_ref, bd_ref, y_ref, wgu_bf_ref, wd_bf_ref):
    i = pl.program_id(0)
    used = i < nu_ref[0]

    @pl.when((i == 0) | (be_ref[i] != be_ref[jnp.maximum(i - 1, 0)]))
    def _():
        wgu_bf_ref[...] = wgu_ref[0].astype(BF16)
        wd_bf_ref[...] = wd_ref[0].astype(BF16)

    @pl.when(used)
    def _():
        hgu = _dot(x_ref[...].astype(BF16), wgu_bf_ref[...]) + bgu_ref[0]
        gate = jnp.minimum(hgu[:, :D_FF], SWIGLU_LIMIT)
        up = jnp.clip(hgu[:, D_FF:], -SWIGLU_LIMIT, SWIGLU_LIMIT)
        act = (up + 1.0) * gate * _sigmoid(SWIGLU_ALPHA * gate)
        y_ref[...] = _dot(act.astype(BF16), wd_bf_ref[...]) + bd_ref[0]

    @pl.when(jnp.logical_not(used))
    def _():
        y_ref[...] = jnp.zeros_like(y_ref)


def _experts(block_e, n_used, xs, w_gu, b_gu, w_d, b_d):
    n_rows = xs.shape[0]
    n_blocks = n_rows // MOE_ROWS
    grid_spec = pltpu.PrefetchScalarGridSpec(
        num_scalar_prefetch=2,
        grid=(n_blocks,),
        in_specs=[
            pl.BlockSpec((MOE_ROWS, D_MODEL), lambda i, be, nu: (jnp.minimum(i, nu[0] - 1), 0)),
            pl.BlockSpec((1, D_MODEL, 2 * D_FF), lambda i, be, nu: (be[i], 0, 0)),
            pl.BlockSpec((1, 1, 2 * D_FF), lambda i, be, nu: (be[i], 0, 0)),
            pl.BlockSpec((1, D_FF, D_MODEL), lambda i, be, nu: (be[i], 0, 0)),
            pl.BlockSpec((1, 1, D_MODEL), lambda i, be, nu: (be[i], 0, 0)),
        ],
        out_specs=pl.BlockSpec((MOE_ROWS, D_MODEL), lambda i, be, nu: (i, 0)),
        scratch_shapes=[pltpu.VMEM((D_MODEL, 2 * D_FF), BF16), pltpu.VMEM((D_FF, D_MODEL), BF16)],
    )
    return pl.pallas_call(
        _experts_kernel,
        grid_spec=grid_spec,
        out_shape=jax.ShapeDtypeStruct((n_rows, D_MODEL), F32),
        compiler_params=pltpu.CompilerParams(
            dimension_semantics=("arbitrary",), vmem_limit_bytes=VMEM_LIMIT),
        name="experts",
    )(block_e, n_used, xs, w_gu, b_gu, w_d, b_d)


def _final_kernel(dest_ref, ys_hbm, tw_ref, h1_ref, p_ref, wpg_ref, wpp_ref, lg_ref, lb_ref, o_ref,
                  g_ref, sem):
    tm = h1_ref.shape[0]

    def issue(r, carry):
        for k in range(TOP_K):
            pltpu.make_async_copy(_row(ys_hbm, dest_ref[r * TOP_K + k]), _row(g_ref.at[k], r), sem).start()
        return carry

    lax.fori_loop(0, tm, issue, 0)
    h1 = h1_ref[...]
    ple = _sigmoid(_dot(h1.astype(BF16), wpg_ref[...])) * _dot(p_ref[...].astype(BF16), wpp_ref[...])
    acc = DN_ALPHA * h1 + ple
    for _ in range(tm * TOP_K // DMA_BATCH):
        _wait_rows(ys_hbm, DMA_BATCH, sem)
    tw = tw_ref[...]
    for k in range(TOP_K):
        acc = acc + tw[:, k:k + 1] * g_ref[k]
    o_ref[...] = _layer_norm(acc, lg_ref[...], lb_ref[...])


def _final(dest_flat, ys, top_w, h1, p, w_pg, w_pp, ln_g, ln_b, tm=256):
    t = h1.shape[0]
    row = lambda i: (i, 0)
    const = lambda i: (0, 0)
    return pl.pallas_call(
        _final_kernel,
        grid=(t // tm,),
        in_specs=[
            pl.BlockSpec((tm * TOP_K,), lambda i: (i,), memory_space=pltpu.SMEM),
            pl.BlockSpec(memory_space=pl.ANY),
            pl.BlockSpec((tm, LANES), row),
            pl.BlockSpec((tm, D_MODEL), row),
            pl.BlockSpec((tm, PLE_DIM), row),
            pl.BlockSpec((D_MODEL, D_MODEL), const),
            pl.BlockSpec((PLE_DIM, D_MODEL), const),
            pl.BlockSpec((1, D_MODEL), const),
            pl.BlockSpec((1, D_MODEL), const),
        ],
        out_specs=pl.BlockSpec((tm, D_MODEL), row),
        out_shape=jax.ShapeDtypeStruct((t, D_MODEL), F32),
        scratch_shapes=[pltpu.VMEM((TOP_K, tm, D_MODEL), F32), pltpu.SemaphoreType.DMA(())],
        compiler_params=pltpu.CompilerParams(
            dimension_semantics=("arbitrary",), vmem_limit_bytes=VMEM_LIMIT),
        name="final",
    )(dest_flat, ys, top_w, h1, p, w_pg, w_pp, ln_g, ln_b)


def kernel(x, p, emb_ln_g, emb_ln_b, hgrn_lb, w_in, w_gla_up, b_gla_up, norm_a_g, norm_b_g, w_proj_a, w_proj_b, w_out, ln_mix_g, ln_mix_b, w_router, b_router, w_gate_up, b_gate_up, w_down, b_down, w_ple_gate, w_ple_proj, ln_moe_g, ln_moe_b):
    bsz, seq, d = x.shape
    t = bsz * seq
    assert d == D_MODEL and seq % GLA_TILE == 0 and t % 1024 == 0
    assert w_in.shape[0] == DEPTH == 1
    vec = lambda a: a.reshape(1, -1).astype(F32)

    lb0 = jax.nn.softmax(hgrn_lb.astype(F32), axis=0)[0:1]
    w_in0 = w_in[0]
    w_main = jnp.concatenate([w_in0[:, :W_IN_LR], w_in0[:, W_IN_LR + GLA_RANK:]], axis=1).astype(BF16)
    w_lr = jnp.pad(w_in0[:, W_IN_LR:W_IN_LR + GLA_RANK], ((0, 0), (0, LANES - GLA_RANK))).astype(BF16)
    w_up = jnp.pad(w_gla_up[0], ((0, LANES - GLA_RANK), (0, 0))).astype(BF16)
    wr = jnp.pad(w_router[0].astype(F32), ((0, 0), (0, LANES - N_EXPERTS)))
    wr_hi = wr.astype(BF16)
    wr_lo = (wr - wr_hi.astype(F32)).astype(BF16)
    b_r = jnp.pad(b_router[0].astype(F32), (0, LANES - N_EXPERTS)).reshape(1, LANES)

    xt = x.reshape(t, d)
    h0, z, z_lr = _ln_inproj(xt, vec(emb_ln_g), vec(emb_ln_b), w_main, w_lr)
    o = _gla(z, z_lr, lb0, w_up, vec(b_gla_up[0]), vec(norm_a_g[0]), vec(norm_b_g[0]), bsz, seq)
    h1, top_e, top_w = _post_mixer(
        o, z, h0, w_proj_a[0].astype(BF16), w_proj_b[0].astype(BF16), w_out[0].astype(BF16),
        vec(ln_mix_g[0]), vec(ln_mix_b[0]), wr_hi, wr_lo, b_r)

    rank, cnt = _rank(top_e)
    counts = cnt[0, :N_EXPERTS].astype(jnp.int32)
    padded = (counts + MOE_ROWS - 1) // MOE_ROWS * MOE_ROWS
    pend = jnp.cumsum(padded)
    pstart = pend - padded
    n_blocks = t * TOP_K // MOE_ROWS + N_EXPERTS
    n_rows = n_blocks * MOE_ROWS
    n_used = pend[N_EXPERTS - 1:] // MOE_ROWS
    dest = (pstart[top_e[:, :TOP_K]] + rank[:, :TOP_K]).reshape(-1)
    block_pos = jnp.minimum(jnp.arange(n_blocks, dtype=jnp.int32), n_used - 1) * MOE_ROWS
    block_e = jnp.minimum(
        jnp.sum((pend[None, :] <= block_pos[:, None]).astype(jnp.int32), axis=1), N_EXPERTS - 1)

    xs = _dispatch(pstart + counts, padded - counts, n_used, dest, h1, n_rows)
    ys = _experts(block_e, n_used, xs, w_gate_up[0].astype(F32),
                  b_gate_up[0].reshape(N_EXPERTS, 1, 2 * D_FF).astype(F32),
                  w_down[0].astype(F32), b_down[0].reshape(N_EXPERTS, 1, d).astype(F32))
    out = _final(dest, ys, top_w, h1, p[0].reshape(t, PLE_DIM),
                 w_ple_gate[0].astype(BF16), w_ple_proj[0].astype(BF16),
                 vec(ln_moe_g[0]), vec(ln_moe_b[0]))
    return out.reshape(bsz, seq, d)
```

```python
import jax
import jax.numpy as jnp
from jax import lax
from jax.experimental import pallas as pl
from jax.experimental.pallas import tpu as pltpu

F32 = jnp.float32
BF16 = jnp.bfloat16

D_MODEL = 1024
A_HEADS, A_DK, A_DV = 8, 128, 128
B_HEADS, B_DK, B_DV = 4, 128, 256
N_HEADS = A_HEADS + B_HEADS
HEAD_DK = 128
A_V = A_HEADS * A_DV
GLA_RANK = 16
GLA_TAU = 16.0
N_EXPERTS = 32
TOP_K = 4
D_FF = 1024
SWIGLU_LIMIT = 7.0
SWIGLU_ALPHA = 1.702
PLE_DIM = 256
LN_EPS = 1e-5
RMS_EPS = 1e-6
DEPTH = 1
DN_ALPHA = (2.0 * DEPTH) ** 0.25

LANES = 128
GLA_CHUNK = 64
GLA_TILE = 256
GLA_SAFE_LOG_DECAY = -60.0
MOE_ROWS = 512
DMA_BATCH = 128
PAD_GROUP = 8
VMEM_LIMIT = 56 * 1024 * 1024

Z_QA, Z_FA, Z_IA, Z_GA = 0, 1024, 2048, 3072
Z_QB, Z_KB, Z_VB, Z_RB = 4096, 4608, 5120, 6144
Z_MIXER = 7168
Z_GATE_A, Z_GATE_B = 7168, 8192
Z_COLS = 9216
W_IN_LR = 7168


def _layer_norm(x, g, b):
    mu = jnp.mean(x, axis=-1, keepdims=True)
    xc = x - mu
    var = jnp.mean(xc * xc, axis=-1, keepdims=True)
    return xc * lax.rsqrt(var + LN_EPS) * g + b


def _sigmoid(x):
    return jax.nn.sigmoid(x)


def _sigmoid_pair(x):
    e = jnp.exp(-jnp.abs(x))
    r = 1.0 / (1.0 + e)
    er = e * r
    pos = x >= 0.0
    return jnp.where(pos, r, er), jnp.where(pos, er, r)


def _dot(a, b):
    return jnp.dot(a, b, preferred_element_type=F32)


def _dot_nt(a, b):
    return lax.dot_general(a, b, (((1,), (1,)), ((), ())), preferred_element_type=F32)


def _dot_tn(a, b):
    return lax.dot_general(a, b, (((0,), (0,)), ((), ())), preferred_element_type=F32)


def _ln_inproj_kernel(x_ref, g_ref, b_ref, w_ref, wlr_ref, h_ref, z_ref, zlr_ref, hb_ref):
    @pl.when(pl.program_id(1) == 0)
    def _():
        h = _layer_norm(x_ref[...], g_ref[...], b_ref[...])
        h_ref[...] = h
        hb = h.astype(BF16)
        hb_ref[...] = hb
        zlr_ref[...] = _dot(hb, wlr_ref[...])

    z_ref[...] = _dot(hb_ref[...], w_ref[...])


def _ln_inproj(x, g, b, w_main, w_lr, tm=1024, tn=1024):
    t = x.shape[0]
    return pl.pallas_call(
        _ln_inproj_kernel,
        grid=(t // tm, Z_COLS // tn),
        in_specs=[
            pl.BlockSpec((tm, D_MODEL), lambda i, j: (i, 0)),
            pl.BlockSpec((1, D_MODEL), lambda i, j: (0, 0)),
            pl.BlockSpec((1, D_MODEL), lambda i, j: (0, 0)),
            pl.BlockSpec((D_MODEL, tn), lambda i, j: (0, j)),
            pl.BlockSpec((D_MODEL, LANES), lambda i, j: (0, 0)),
        ],
        out_specs=[
            pl.BlockSpec((tm, D_MODEL), lambda i, j: (i, 0)),
            pl.BlockSpec((tm, tn), lambda i, j: (i, j)),
            pl.BlockSpec((tm, LANES), lambda i, j: (i, 0)),
        ],
        out_shape=[
            jax.ShapeDtypeStruct((t, D_MODEL), F32),
            jax.ShapeDtypeStruct((t, Z_COLS), F32),
            jax.ShapeDtypeStruct((t, LANES), F32),
        ],
        scratch_shapes=[pltpu.VMEM((tm, D_MODEL), BF16)],
        compiler_params=pltpu.CompilerParams(
            dimension_semantics=("parallel", "arbitrary"), vmem_limit_bytes=VMEM_LIMIT),
        name="ln_inproj",
    )(x, g, b, w_main, w_lr)


def _chunk_tri(tile, chunk):
    i = jnp.arange(tile)[:, None]
    j = jnp.arange(tile)[None, :]
    return ((j <= i) & (i // chunk == j // chunk)).astype(BF16)


def _chunk_cumsum(tri, x):
    hi = x.astype(BF16)
    lo = (x - hi.astype(F32)).astype(BF16)
    return _dot(tri, hi) + _dot(tri, lo)


def _gla_prep(z_ref, lr_ref, lb_ref, wup_ref, bup_ref, tri_ref, qs_ref, q1_ref, ks_ref, bs_ref):
    low = None
    tri = tri_ref[...]

    def put(h, q, k, b):
        qs_ref[h] = q
        q1_ref[h] = (q * jnp.exp(b)).astype(BF16)
        ks_ref[h] = k
        bs_ref[h] = b

    for h in range(A_HEADS):
        cols = slice(h * A_DK, (h + 1) * A_DK)
        lb = lb_ref[:, cols]
        sig, sig_neg = _sigmoid_pair(z_ref[:, Z_FA + h * A_DK:Z_FA + (h + 1) * A_DK])
        b = _chunk_cumsum(tri, jnp.log(lb + (1.0 - lb) * sig))
        qz = z_ref[:, Z_QA + h * A_DK:Z_QA + (h + 1) * A_DK]
        put(h, qz * _sigmoid(qz) * (A_DK ** -0.5), (1.0 - lb) * sig_neg, b)
        low = b if low is None else jnp.minimum(low, b)
    u_all = _dot(lr_ref[...].astype(BF16), wup_ref[...]) + bup_ref[...]
    for hb in range(B_HEADS):
        u = u_all[:, hb * B_DK:(hb + 1) * B_DK]
        log_sig = jnp.minimum(u, 0.0) - jnp.log(1.0 + jnp.exp(-jnp.abs(u)))
        b = _chunk_cumsum(tri, log_sig / GLA_TAU)
        put(A_HEADS + hb, z_ref[:, Z_QB + hb * B_DK:Z_QB + (hb + 1) * B_DK] * (B_DK ** -0.5),
            z_ref[:, Z_KB + hb * B_DK:Z_KB + (hb + 1) * B_DK], b)
        low = jnp.minimum(low, b)
    return low


def _product_scores(q1_ref, ks_ref, bs_ref, sc_ref):
    c = GLA_CHUNK
    for h in range(N_HEADS):
        k1 = (ks_ref[h] * jnp.exp(-bs_ref[h])).astype(BF16)
        for ci in range(GLA_TILE // c):
            rows = slice(ci * c, (ci + 1) * c)
            sc_ref[h, ci] = _dot_nt(q1_ref[h, rows, :], k1[rows])


def _exact_scores(qs_ref, ks_ref, bs_ref, sc_ref):
    c = GLA_CHUNK
    n = GLA_TILE // c
    col = lax.broadcasted_iota(jnp.int32, (c, c), 1)

    def per_chunk(idx, carry):
        h = idx // n
        ci = idx - h * n
        r0 = pl.multiple_of(ci * c, c)
        qc = qs_ref[h, pl.ds(r0, c), :]
        bc = bs_ref[h, pl.ds(r0, c), :]

        def column(j, acc):
            kj = ks_ref[h, pl.ds(r0 + j, 1), :]
            bj = bs_ref[h, pl.ds(r0 + j, 1), :]
            term = qc * kj * jnp.exp(jnp.minimum(bc - bj, 0.0))
            return jnp.where(col == j, jnp.sum(term, axis=-1, keepdims=True), acc)

        sc_ref[h, ci] = lax.fori_loop(0, c, column, jnp.zeros((c, c), F32))
        return carry

    lax.fori_loop(0, N_HEADS * n, per_chunk, 0)


def _gla_outputs(z_ref, q1_ref, ks_ref, bs_ref, sc_ref, sta_ref, stb_ref, nga_ref, ngb_ref, o_ref):
    c = GLA_CHUNK
    causal = lax.broadcasted_iota(jnp.int32, (c, c), 0) >= lax.broadcasted_iota(jnp.int32, (c, c), 1)
    for ci in range(GLA_TILE // c):
        rows = slice(ci * c, (ci + 1) * c)
        for h in range(N_HEADS):
            if h < A_HEADS:
                dv, st_ref, norm_g = A_DV, sta_ref.at[h], nga_ref[...]
                v_off, g_off, o_off = Z_IA + h * A_DV, Z_GA + h * A_DV, h * A_DV
            else:
                hb = h - A_HEADS
                dv, st_ref, norm_g = B_DV, stb_ref.at[hb], ngb_ref[...]
                v_off, g_off, o_off = Z_VB + hb * B_DV, Z_RB + hb * B_DV, A_V + hb * B_DV
            st = st_ref[...]
            k = ks_ref[h, rows, :]
            bc = bs_ref[h, rows, :]
            b_last = bc[c - 1:c]
            k_dec = (k * jnp.exp(b_last - bc)).astype(BF16)
            vb = z_ref[rows, v_off:v_off + dv].astype(BF16)
            scores = jnp.where(causal, sc_ref[h, ci], 0.0).astype(BF16)
            o = _dot_nt(q1_ref[h, rows, :], st.astype(BF16)) + _dot(scores, vb)
            st_ref[...] = st * jnp.exp(b_last) + _dot_tn(vb, k_dec)
            on = o * lax.rsqrt(jnp.mean(o * o, axis=-1, keepdims=True) + RMS_EPS) * norm_g
            gc = z_ref[rows, g_off:g_off + dv]
            o_ref[rows, o_off:o_off + dv] = (on * (gc * _sigmoid(gc))).astype(o_ref.dtype)


def _gla_kernel(z_ref, lr_ref, lb_ref, wup_ref, bup_ref, tri_ref, nga_ref, ngb_ref, o_ref,
                sta_ref, stb_ref, qs_ref, q1_ref, ks_ref, bs_ref, sc_ref):
    @pl.when(pl.program_id(1) == 0)
    def _():
        sta_ref[...] = jnp.zeros_like(sta_ref)
        stb_ref[...] = jnp.zeros_like(stb_ref)

    low = _gla_prep(z_ref, lr_ref, lb_ref, wup_ref, bup_ref, tri_ref, qs_ref, q1_ref, ks_ref, bs_ref)
    safe = jnp.min(low) >= GLA_SAFE_LOG_DECAY

    @pl.when(safe)
    def _product_form():
        _product_scores(q1_ref, ks_ref, bs_ref, sc_ref)

    @pl.when(jnp.logical_not(safe))
    def _exact_form():
        _exact_scores(qs_ref, ks_ref, bs_ref, sc_ref)

    _gla_outputs(z_ref, q1_ref, ks_ref, bs_ref, sc_ref, sta_ref, stb_ref, nga_ref, ngb_ref, o_ref)


def _gla(z, z_lr, lb, w_up, b_up, norm_a, norm_b, bsz, seq):
    ts = GLA_TILE
    nt = seq // ts
    n_chunks = ts // GLA_CHUNK
    row = lambda b, t: (b * nt + t, 0)
    const = lambda b, t: (0, 0)
    per_head = pltpu.VMEM((N_HEADS, ts, HEAD_DK), F32)
    return pl.pallas_call(
        _gla_kernel,
        grid=(bsz, nt),
        in_specs=[
            pl.BlockSpec((ts, Z_MIXER), row),
            pl.BlockSpec((ts, LANES), row),
            pl.BlockSpec((1, A_HEADS * A_DK), const),
            pl.BlockSpec((LANES, B_HEADS * B_DK), const),
            pl.BlockSpec((1, B_HEADS * B_DK), const),
            pl.BlockSpec((ts, ts), const),
            pl.BlockSpec((1, A_DV), const),
            pl.BlockSpec((1, B_DV), const),
        ],
        out_specs=pl.BlockSpec((ts, 2 * D_MODEL), row),
        out_shape=jax.ShapeDtypeStruct((bsz * seq, 2 * D_MODEL), BF16),
        scratch_shapes=[
            pltpu.VMEM((A_HEADS, A_DV, A_DK), F32),
            pltpu.VMEM((B_HEADS, B_DV, B_DK), F32),
            per_head, pltpu.VMEM((N_HEADS, ts, HEAD_DK), BF16), per_head, per_head,
            pltpu.VMEM((N_HEADS, n_chunks, GLA_CHUNK, GLA_CHUNK), F32),
        ],
        compiler_params=pltpu.CompilerParams(
            dimension_semantics=("parallel", "arbitrary"), vmem_limit_bytes=VMEM_LIMIT),
        name="gla",
    )(z, z_lr, lb, w_up, b_up, _chunk_tri(ts, GLA_CHUNK), norm_a, norm_b)


def _lane_pack(cols, dtype):
    m = cols[0].shape[0]
    lane = lax.broadcasted_iota(jnp.int32, (m, LANES), 1)
    out = jnp.zeros((m, LANES), dtype)
    for k, cvals in enumerate(cols):
        out = jnp.where(lane == k, cvals.astype(dtype), out)
    return out


def _post_mixer_kernel(o_ref, ga_ref, gb_ref, h0_ref, wa_ref, wb_ref, wo_ref, lg_ref, lb_ref,
                       wrh_ref, wrl_ref, br_ref, h1_ref, te_ref, tw_ref):
    ya = _dot(o_ref[:, :D_MODEL], wa_ref[...])
    yb = _dot(o_ref[:, D_MODEL:], wb_ref[...])
    merged = _sigmoid(ga_ref[...]) * ya + _sigmoid(gb_ref[...]) * yb
    y = _dot(merged.astype(BF16), wo_ref[...])
    h1 = _layer_norm(DN_ALPHA * h0_ref[...] + y, lg_ref[...], lb_ref[...])
    h1_ref[...] = h1

    hh = h1.astype(BF16)
    hl = (h1 - hh.astype(F32)).astype(BF16)
    logits = _dot(hh, wrh_ref[...]) + (_dot(hl, wrh_ref[...]) + _dot(hh, wrl_ref[...])) + br_ref[...]
    lane = lax.broadcasted_iota(jnp.int32, logits.shape, 1)
    work = jnp.where(lane < N_EXPERTS, logits, -jnp.inf)
    vals, idxs = [], []
    for _ in range(TOP_K):
        m = jnp.max(work, axis=-1, keepdims=True)
        idx = jnp.min(jnp.where(work == m, lane, LANES), axis=-1, keepdims=True)
        vals.append(m)
        idxs.append(idx)
        work = jnp.where(lane == idx, -jnp.inf, work)
    exps = [jnp.exp(v - vals[0]) for v in vals]
    denom = exps[0] + exps[1] + exps[2] + exps[3]
    te_ref[...] = _lane_pack(idxs, jnp.int32)
    tw_ref[...] = _lane_pack([e / denom for e in exps], F32)


def _post_mixer(o, z, h0, w_a, w_b, w_o, ln_g, ln_b, wr_hi, wr_lo, b_r, tm=512):
    t = h0.shape[0]
    row = lambda i: (i, 0)
    const = lambda i: (0, 0)
    full_w = pl.BlockSpec((D_MODEL, D_MODEL), const)
    vec = pl.BlockSpec((1, D_MODEL), const)
    return pl.pallas_call(
        _post_mixer_kernel,
        grid=(t // tm,),
        in_specs=[
            pl.BlockSpec((tm, 2 * D_MODEL), row),
            pl.BlockSpec((tm, D_MODEL), lambda i: (i, Z_GATE_A // D_MODEL)),
            pl.BlockSpec((tm, D_MODEL), lambda i: (i, Z_GATE_B // D_MODEL)),
            pl.BlockSpec((tm, D_MODEL), row),
            full_w, full_w, full_w, vec, vec,
            pl.BlockSpec((D_MODEL, LANES), const),
            pl.BlockSpec((D_MODEL, LANES), const),
            pl.BlockSpec((1, LANES), const),
        ],
        out_specs=[
            pl.BlockSpec((tm, D_MODEL), row),
            pl.BlockSpec((tm, LANES), row),
            pl.BlockSpec((tm, LANES), row),
        ],
        out_shape=[
            jax.ShapeDtypeStruct((t, D_MODEL), F32),
            jax.ShapeDtypeStruct((t, LANES), jnp.int32),
            jax.ShapeDtypeStruct((t, LANES), F32),
        ],
        compiler_params=pltpu.CompilerParams(
            dimension_semantics=("parallel",), vmem_limit_bytes=VMEM_LIMIT),
        name="post_mixer",
    )(o, z, z, h0, w_a, w_b, w_o, ln_g, ln_b, wr_hi, wr_lo, b_r)


def _rank_kernel(te_ref, rank_ref, cnt_ref, carry_ref):
    @pl.when(pl.program_id(0) == 0)
    def _():
        carry_ref[...] = jnp.zeros_like(carry_ref)

    te = te_ref[...]
    tr = te.shape[0]
    lane = lax.broadcasted_iota(jnp.int32, (tr, LANES), 1)
    onehots = [te[:, k:k + 1] == lane for k in range(TOP_K)]
    cnt = jnp.zeros((tr, LANES), F32)
    for oh in onehots:
        cnt = cnt + oh.astype(F32)
    strict_lower = (lax.broadcasted_iota(jnp.int32, (tr, tr), 0)
                    > lax.broadcasted_iota(jnp.int32, (tr, tr), 1))
    earlier = _dot(strict_lower.astype(BF16), cnt.astype(BF16)) + carry_ref[...]
    ranks = [jnp.sum(jnp.where(oh, earlier, 0.0), axis=-1, keepdims=True) for oh in onehots]
    rank_ref[...] = _lane_pack(ranks, jnp.int32)
    carry_ref[...] = carry_ref[...] + jnp.sum(cnt, axis=0, keepdims=True)
    cnt_ref[...] = carry_ref[...]


def _rank(top_e, tr=512):
    t = top_e.shape[0]
    return pl.pallas_call(
        _rank_kernel,
        grid=(t // tr,),
        in_specs=[pl.BlockSpec((tr, LANES), lambda i: (i, 0))],
        out_specs=[
            pl.BlockSpec((tr, LANES), lambda i: (i, 0)),
            pl.BlockSpec((1, LANES), lambda i: (0, 0)),
        ],
        out_shape=[
            jax.ShapeDtypeStruct((t, LANES), jnp.int32),
            jax.ShapeDtypeStruct((1, LANES), F32),
        ],
        scratch_shapes=[pltpu.VMEM((1, LANES), F32)],
        compiler_params=pltpu.CompilerParams(dimension_semantics=("arbitrary",)),
        name="rank",
    )(top_e)


def _row(ref, r):
    return ref.at[pl.ds(r, 1), :]


def _wait_rows(ref_hbm, n, sem):
    pltpu.make_async_copy(ref_hbm.at[pl.ds(0, n), :], ref_hbm.at[pl.ds(0, n), :], sem).wait()


def _dispatch_kernel(pad_start_ref, pad_cnt_ref, nu_ref, dest_ref, h_ref, p_ref, wpg_ref, wpp_ref,
                     xs_hbm, res_ref, zero_ref, sem):
    n_tok = h_ref.shape[0]

    @pl.when(pl.program_id(0) == 0)
    def _zero_rows_without_token():
        zero_ref[...] = jnp.zeros_like(zero_ref)

        def wait_pad_groups(n):
            @pl.when(n > 0)
            def _():
                _wait_rows(xs_hbm, pl.multiple_of(n * PAD_GROUP, PAD_GROUP), sem)

        def per_expert(e, n_prev):
            def start(g, c):
                r0 = pl.multiple_of(pad_start_ref[e] + g * PAD_GROUP, PAD_GROUP)
                pltpu.make_async_copy(zero_ref.at[pl.ds(0, PAD_GROUP), :],
                                      xs_hbm.at[pl.ds(r0, PAD_GROUP), :], sem).start()
                return c

            lax.fori_loop(0, pad_cnt_ref[e], start, 0)
            wait_pad_groups(n_prev)
            return pad_cnt_ref[e]

        wait_pad_groups(lax.fori_loop(0, N_EXPERTS, per_expert, 0))

        def block_copy(blk):
            r0 = pl.multiple_of(blk * MOE_ROWS, MOE_ROWS)
            return pltpu.make_async_copy(zero_ref, xs_hbm.at[pl.ds(r0, MOE_ROWS), :], sem)

        def start_block(blk, c):
            block_copy(blk).start()
            return c

        def wait_block(blk, c):
            block_copy(blk).wait()
            return c

        n_blocks = xs_hbm.shape[0] // MOE_ROWS
        lax.fori_loop(nu_ref[0], n_blocks, start_block, 0)
        lax.fori_loop(nu_ref[0], n_blocks, wait_block, 0)

    def issue(r, carry):
        for k in range(TOP_K):
            pltpu.make_async_copy(_row(h_ref, r), _row(xs_hbm, dest_ref[r * TOP_K + k]), sem).start()
        return carry

    lax.fori_loop(0, n_tok, issue, 0)
    h1 = h_ref[...]
    ple = _sigmoid(_dot(h1.astype(BF16), wpg_ref[...])) * _dot(p_ref[...].astype(BF16), wpp_ref[...])
    res_ref[...] = DN_ALPHA * h1 + ple
    for _ in range(n_tok * TOP_K // DMA_BATCH):
        _wait_rows(xs_hbm, DMA_BATCH, sem)


def _dispatch(pad_start, pad_cnt, n_used, dest_flat, h1, p, w_pg, w_pp, n_rows, tok_per_step=256):
    t = h1.shape[0]
    row = lambda i, ps, pc, nu: (i, 0)
    const = lambda i, ps, pc, nu: (0, 0)
    grid_spec = pltpu.PrefetchScalarGridSpec(
        num_scalar_prefetch=3,
        grid=(t // tok_per_step,),
        in_specs=[
            pl.BlockSpec((tok_per_step * TOP_K,), lambda i, ps, pc, nu: (i,), memory_space=pltpu.SMEM),
            pl.BlockSpec((tok_per_step, D_MODEL), row),
            pl.BlockSpec((tok_per_step, PLE_DIM), row),
            pl.BlockSpec((D_MODEL, D_MODEL), const),
            pl.BlockSpec((PLE_DIM, D_MODEL), const),
        ],
        out_specs=[
            pl.BlockSpec(memory_space=pl.ANY),
            pl.BlockSpec((tok_per_step, D_MODEL), row),
        ],
        scratch_shapes=[pltpu.VMEM((MOE_ROWS, D_MODEL), F32), pltpu.SemaphoreType.DMA(())],
    )
    return pl.pallas_call(
        _dispatch_kernel,
        grid_spec=grid_spec,
        out_shape=[
            jax.ShapeDtypeStruct((n_rows, D_MODEL), F32),
            jax.ShapeDtypeStruct((t, D_MODEL), F32),
        ],
        compiler_params=pltpu.CompilerParams(
            dimension_semantics=("arbitrary",), vmem_limit_bytes=VMEM_LIMIT),
        name="dispatch",
    )(pad_start, pad_cnt, n_used, dest_flat, h1, p, w_pg, w_pp)


def _experts_kernel(be_ref, nu_ref, x_ref, wgu_ref, bgu_ref, wd_ref, bd_ref, y_ref, wgu_bf_ref, wd_bf_ref):
    i = pl.program_id(0)
    used = i < nu_ref[0]

    @pl.when((i == 0) | (be_ref[i] != be_ref[jnp.maximum(i - 1, 0)]))
    def _():
        wgu_bf_ref[...] = wgu_ref[0].astype(BF16)
        wd_bf_ref[...] = wd_ref[0].astype(BF16)

    @pl.when(used)
    def _():
        hgu = _dot(x_ref[...].astype(BF16), wgu_bf_ref[...]) + bgu_ref[0]
        gate = jnp.minimum(hgu[:, :D_FF], SWIGLU_LIMIT)
        up = jnp.clip(hgu[:, D_FF:], -SWIGLU_LIMIT, SWIGLU_LIMIT)
        act = (up + 1.0) * gate * _sigmoid(SWIGLU_ALPHA * gate)
        y_ref[...] = _dot(act.astype(BF16), wd_bf_ref[...]) + bd_ref[0]

    @pl.when(jnp.logical_not(used))
    def _():
        y_ref[...] = jnp.zeros_like(y_ref)


def _experts(block_e, n_used, xs, w_gu, b_gu, w_d, b_d):
    n_rows = xs.shape[0]
    n_blocks = n_rows // MOE_ROWS
    grid_spec = pltpu.PrefetchScalarGridSpec(
        num_scalar_prefetch=2,
        grid=(n_blocks,),
        in_specs=[
            pl.BlockSpec((MOE_ROWS, D_MODEL), lambda i, be, nu: (jnp.minimum(i, nu[0] - 1), 0)),
            pl.BlockSpec((1, D_MODEL, 2 * D_FF), lambda i, be, nu: (be[i], 0, 0)),
            pl.BlockSpec((1, 1, 2 * D_FF), lambda i, be, nu: (be[i], 0, 0)),
            pl.BlockSpec((1, D_FF, D_MODEL), lambda i, be, nu: (be[i], 0, 0)),
            pl.BlockSpec((1, 1, D_MODEL), lambda i, be, nu: (be[i], 0, 0)),
        ],
        out_specs=pl.BlockSpec((MOE_ROWS, D_MODEL), lambda i, be, nu: (i, 0)),
        scratch_shapes=[pltpu.VMEM((D_MODEL, 2 * D_FF), BF16), pltpu.VMEM((D_FF, D_MODEL), BF16)],
    )
    return pl.pallas_call(
        _experts_kernel,
        grid_spec=grid_spec,
        out_shape=jax.ShapeDtypeStruct((n_rows, D_MODEL), F32),
        compiler_params=pltpu.CompilerParams(
            dimension_semantics=("arbitrary",), vmem_limit_bytes=VMEM_LIMIT),
        name="experts",
    )(block_e, n_used, xs, w_gu, b_gu, w_d, b_d)


def _final_kernel(dest_ref, dest_next_ref, ys_hbm, tw_ref, res_ref, lg_ref, lb_ref, o_ref, g_ref, sem):
    tm = res_ref.shape[0]
    i = pl.program_id(0)
    slot = i % 2

    def gather(idx_ref, s):
        def issue(r, carry):
            for k in range(TOP_K):
                pltpu.make_async_copy(_row(ys_hbm, idx_ref[r * TOP_K + k]), _row(g_ref.at[s, k], r),
                                      sem.at[s]).start()
            return carry

        lax.fori_loop(0, tm, issue, 0)

    @pl.when(i == 0)
    def _():
        gather(dest_ref, 0)

    @pl.when(i + 1 < pl.num_programs(0))
    def _():
        gather(dest_next_ref, 1 - slot)

    for _ in range(tm * TOP_K // DMA_BATCH):
        _wait_rows(ys_hbm, DMA_BATCH, sem.at[slot])
    tw = tw_ref[...]
    acc = res_ref[...]
    for k in range(TOP_K):
        acc = acc + tw[:, k:k + 1] * g_ref[slot, k]
    o_ref[...] = _layer_norm(acc, lg_ref[...], lb_ref[...])


def _final(dest_flat, ys, top_w, res, ln_g, ln_b, tm=256):
    t = res.shape[0]
    n_steps = t // tm
    row = lambda i: (i, 0)
    const = lambda i: (0, 0)
    return pl.pallas_call(
        _final_kernel,
        grid=(n_steps,),
        in_specs=[
            pl.BlockSpec((tm * TOP_K,), lambda i: (i,), memory_space=pltpu.SMEM),
            pl.BlockSpec((tm * TOP_K,), lambda i: (jnp.minimum(i + 1, n_steps - 1),),
                         memory_space=pltpu.SMEM),
            pl.BlockSpec(memory_space=pl.ANY),
            pl.BlockSpec((tm, LANES), row),
            pl.BlockSpec((tm, D_MODEL), row),
            pl.BlockSpec((1, D_MODEL), const),
            pl.BlockSpec((1, D_MODEL), const),
        ],
        out_specs=pl.BlockSpec((tm, D_MODEL), row),
        out_shape=jax.ShapeDtypeStruct((t, D_MODEL), F32),
        scratch_shapes=[pltpu.VMEM((2, TOP_K, tm, D_MODEL), F32), pltpu.SemaphoreType.DMA((2,))],
        compiler_params=pltpu.CompilerParams(
            dimension_semantics=("arbitrary",), vmem_limit_bytes=VMEM_LIMIT),
        name="final",
    )(dest_flat, dest_flat, ys, top_w, res, ln_g, ln_b)


def kernel(x, p, emb_ln_g, emb_ln_b, hgrn_lb, w_in, w_gla_up, b_gla_up, norm_a_g, norm_b_g, w_proj_a, w_proj_b, w_out, ln_mix_g, ln_mix_b, w_router, b_router, w_gate_up, b_gate_up, w_down, b_down, w_ple_gate, w_ple_proj, ln_moe_g, ln_moe_b):
    bsz, seq, d = x.shape
    t = bsz * seq
    assert d == D_MODEL and seq % GLA_TILE == 0 and t % 1024 == 0
    assert w_in.shape[0] == DEPTH == 1
    vec = lambda a: a.reshape(1, -1).astype(F32)

    lb0 = jax.nn.softmax(hgrn_lb.astype(F32), axis=0)[0:1]
    w_in0 = w_in[0]
    w_main = jnp.concatenate([w_in0[:, :W_IN_LR], w_in0[:, W_IN_LR + GLA_RANK:]], axis=1).astype(BF16)
    w_lr = jnp.pad(w_in0[:, W_IN_LR:W_IN_LR + GLA_RANK], ((0, 0), (0, LANES - GLA_RANK))).astype(BF16)
    w_up = jnp.pad(w_gla_up[0], ((0, LANES - GLA_RANK), (0, 0))).astype(BF16)
    wr = jnp.pad(w_router[0].astype(F32), ((0, 0), (0, LANES - N_EXPERTS)))
    wr_hi = wr.astype(BF16)
    wr_lo = (wr - wr_hi.astype(F32)).astype(BF16)
    b_r = jnp.pad(b_router[0].astype(F32), (0, LANES - N_EXPERTS)).reshape(1, LANES)

    xt = x.reshape(t, d)
    h0, z, z_lr = _ln_inproj(xt, vec(emb_ln_g), vec(emb_ln_b), w_main, w_lr)
    o = _gla(z, z_lr, lb0, w_up, vec(b_gla_up[0]), vec(norm_a_g[0]), vec(norm_b_g[0]), bsz, seq)
    h1, top_e, top_w = _post_mixer(
        o, z, h0, w_proj_a[0].astype(BF16), w_proj_b[0].astype(BF16), w_out[0].astype(BF16),
        vec(ln_mix_g[0]), vec(ln_mix_b[0]), wr_hi, wr_lo, b_r)

    rank, cnt = _rank(top_e)
    counts = cnt[0, :N_EXPERTS].astype(jnp.int32)
    padded = (counts + MOE_ROWS - 1) // MOE_ROWS * MOE_ROWS
    pend = jnp.cumsum(padded)
    pstart = pend - padded
    n_blocks = t * TOP_K // MOE_ROWS + N_EXPERTS
    n_rows = n_blocks * MOE_ROWS
    n_used = pend[N_EXPERTS - 1:] // MOE_ROWS
    dest = (pstart[top_e[:, :TOP_K]] + rank[:, :TOP_K]).reshape(-1)
    block_pos = jnp.minimum(jnp.arange(n_blocks, dtype=jnp.int32), n_used - 1) * MOE_ROWS
    block_e = jnp.minimum(
        jnp.sum((pend[None, :] <= block_pos[:, None]).astype(jnp.int32), axis=1), N_EXPERTS - 1)

    pad_start = (pstart + counts) // PAD_GROUP * PAD_GROUP
    xs, res = _dispatch(pad_start, (pend - pad_start) // PAD_GROUP, n_used, dest, h1,
                        p[0].reshape(t, PLE_DIM), w_ple_gate[0].astype(BF16), w_ple_proj[0].astype(BF16),
                        n_rows)
    ys = _experts(block_e, n_used, xs, w_gate_up[0].astype(F32),
                  b_gate_up[0].reshape(N_EXPERTS, 1, 2 * D_FF).astype(F32),
                  w_down[0].astype(F32), b_down[0].reshape(N_EXPERTS, 1, d).astype(F32))
    out = _final(dest, ys, top_w, res, vec(ln_moe_g[0]), vec(ln_moe_b[0]))
    return out.reshape(bsz, seq, d)
```

```python
import jax
import jax.numpy as jnp
from jax import lax
from jax.experimental import pallas as pl
from jax.experimental.pallas import tpu as pltpu

F32 = jnp.float32
BF16 = jnp.bfloat16

D_MODEL = 1024
A_HEADS, A_DK, A_DV = 8, 128, 128
B_HEADS, B_DK, B_DV = 4, 128, 256
N_HEADS = A_HEADS + B_HEADS
HEAD_DK = 128
A_V = A_HEADS * A_DV
GLA_RANK = 16
GLA_TAU = 16.0
N_EXPERTS = 32
TOP_K = 4
D_FF = 1024
SWIGLU_LIMIT = 7.0
SWIGLU_ALPHA = 1.702
PLE_DIM = 256
LN_EPS = 1e-5
RMS_EPS = 1e-6
DEPTH = 1
DN_ALPHA = (2.0 * DEPTH) ** 0.25

LANES = 128
GLA_CHUNK = 64
GLA_TILE = 256
GLA_SAFE_LOG_DECAY = -60.0
MOE_ROWS = 512
DMA_BATCH = 128
F32_SUBLANES = 8
PAD_GROUP = F32_SUBLANES
VMEM_LIMIT = 56 * 1024 * 1024

Z_QA, Z_FA, Z_IA, Z_GA = 0, 1024, 2048, 3072
Z_QB, Z_KB, Z_VB, Z_RB = 4096, 4608, 5120, 6144
Z_MIXER = 7168
Z_GATE_A, Z_GATE_B = 7168, 8192
Z_COLS = 9216
W_IN_LR = 7168


def _layer_norm(x, g, b):
    mu = jnp.mean(x, axis=-1, keepdims=True)
    xc = x - mu
    var = jnp.mean(xc * xc, axis=-1, keepdims=True)
    return xc * lax.rsqrt(var + LN_EPS) * g + b


def _sigmoid(x):
    return jax.nn.sigmoid(x)


def _sigmoid_pair(x):
    e = jnp.exp(-jnp.abs(x))
    r = 1.0 / (1.0 + e)
    er = e * r
    pos = x >= 0.0
    return jnp.where(pos, r, er), jnp.where(pos, er, r)


def _dot(a, b):
    return jnp.dot(a, b, preferred_element_type=F32)


def _dot_nt(a, b):
    return lax.dot_general(a, b, (((1,), (1,)), ((), ())), preferred_element_type=F32)


def _dot_tn(a, b):
    return lax.dot_general(a, b, (((0,), (0,)), ((), ())), preferred_element_type=F32)


def _ln_inproj_kernel(x_ref, g_ref, b_ref, w_ref, wlr_ref, h_ref, z_ref, zlr_ref, hb_ref):
    @pl.when(pl.program_id(1) == 0)
    def _():
        h = _layer_norm(x_ref[...], g_ref[...], b_ref[...])
        h_ref[...] = h
        hb = h.astype(BF16)
        hb_ref[...] = hb
        zlr_ref[...] = _dot(hb, wlr_ref[...])

    z_ref[...] = _dot(hb_ref[...], w_ref[...])


def _ln_inproj(x, g, b, w_main, w_lr, tm=1024, tn=1536):
    t = x.shape[0]
    return pl.pallas_call(
        _ln_inproj_kernel,
        grid=(t // tm, Z_COLS // tn),
        in_specs=[
            pl.BlockSpec((tm, D_MODEL), lambda i, j: (i, 0)),
            pl.BlockSpec((1, D_MODEL), lambda i, j: (0, 0)),
            pl.BlockSpec((1, D_MODEL), lambda i, j: (0, 0)),
            pl.BlockSpec((D_MODEL, tn), lambda i, j: (0, j)),
            pl.BlockSpec((D_MODEL, LANES), lambda i, j: (0, 0)),
        ],
        out_specs=[
            pl.BlockSpec((tm, D_MODEL), lambda i, j: (i, 0)),
            pl.BlockSpec((tm, tn), lambda i, j: (i, j)),
            pl.BlockSpec((tm, LANES), lambda i, j: (i, 0)),
        ],
        out_shape=[
            jax.ShapeDtypeStruct((t, D_MODEL), F32),
            jax.ShapeDtypeStruct((t, Z_COLS), F32),
            jax.ShapeDtypeStruct((t, LANES), F32),
        ],
        scratch_shapes=[pltpu.VMEM((tm, D_MODEL), BF16)],
        compiler_params=pltpu.CompilerParams(
            dimension_semantics=("parallel", "arbitrary"), vmem_limit_bytes=VMEM_LIMIT),
        name="ln_inproj",
    )(x, g, b, w_main, w_lr)


def _chunk_tri(tile, chunk):
    i = jnp.arange(tile)[:, None]
    j = jnp.arange(tile)[None, :]
    return ((j <= i) & (i // chunk == j // chunk)).astype(BF16)


def _chunk_cumsum(tri, x):
    hi = x.astype(BF16)
    lo = (x - hi.astype(F32)).astype(BF16)
    return _dot(tri, hi) + _dot(tri, lo)


def _gla_prep(z_ref, lr_ref, lb_ref, wup_ref, bup_ref, tri_ref, qs_ref, q1_ref, ks_ref, bs_ref):
    low = None
    tri = tri_ref[...]

    def put(h, q, k, b):
        qs_ref[h] = q
        q1_ref[h] = (q * jnp.exp(b)).astype(BF16)
        ks_ref[h] = k
        bs_ref[h] = b

    for h in range(A_HEADS):
        cols = slice(h * A_DK, (h + 1) * A_DK)
        lb = lb_ref[:, cols]
        sig, sig_neg = _sigmoid_pair(z_ref[:, Z_FA + h * A_DK:Z_FA + (h + 1) * A_DK])
        b = _chunk_cumsum(tri, jnp.log(lb + (1.0 - lb) * sig))
        qz = z_ref[:, Z_QA + h * A_DK:Z_QA + (h + 1) * A_DK]
        put(h, qz * _sigmoid(qz) * (A_DK ** -0.5), (1.0 - lb) * sig_neg, b)
        low = b if low is None else jnp.minimum(low, b)
    u_all = _dot(lr_ref[...].astype(BF16), wup_ref[...]) + bup_ref[...]
    for hb in range(B_HEADS):
        u = u_all[:, hb * B_DK:(hb + 1) * B_DK]
        log_sig = jnp.minimum(u, 0.0) - jnp.log(1.0 + jnp.exp(-jnp.abs(u)))
        b = _chunk_cumsum(tri, log_sig / GLA_TAU)
        put(A_HEADS + hb, z_ref[:, Z_QB + hb * B_DK:Z_QB + (hb + 1) * B_DK] * (B_DK ** -0.5),
            z_ref[:, Z_KB + hb * B_DK:Z_KB + (hb + 1) * B_DK], b)
        low = jnp.minimum(low, b)
    return low


def _product_scores(q1_ref, ks_ref, bs_ref, sc_ref):
    c = GLA_CHUNK
    for h in range(N_HEADS):
        k1 = (ks_ref[h] * jnp.exp(-bs_ref[h])).astype(BF16)
        for ci in range(GLA_TILE // c):
            rows = slice(ci * c, (ci + 1) * c)
            sc_ref[h, ci] = _dot_nt(q1_ref[h, rows, :], k1[rows])


def _exact_scores(qs_ref, ks_ref, bs_ref, sc_ref):
    c = GLA_CHUNK
    n = GLA_TILE // c
    col = lax.broadcasted_iota(jnp.int32, (c, c), 1)

    def per_chunk(idx, carry):
        h = idx // n
        ci = idx - h * n
        r0 = pl.multiple_of(ci * c, c)
        qc = qs_ref[h, pl.ds(r0, c), :]
        bc = bs_ref[h, pl.ds(r0, c), :]

        def column(j, acc):
            kj = ks_ref[h, pl.ds(r0 + j, 1), :]
            bj = bs_ref[h, pl.ds(r0 + j, 1), :]
            term = qc * kj * jnp.exp(jnp.minimum(bc - bj, 0.0))
            return jnp.where(col == j, jnp.sum(term, axis=-1, keepdims=True), acc)

        sc_ref[h, ci] = lax.fori_loop(0, c, column, jnp.zeros((c, c), F32))
        return carry

    lax.fori_loop(0, N_HEADS * n, per_chunk, 0)


def _gla_outputs(z_ref, q1_ref, ks_ref, bs_ref, sc_ref, sta_ref, stb_ref, nga_ref, ngb_ref, o_ref):
    c = GLA_CHUNK
    causal = lax.broadcasted_iota(jnp.int32, (c, c), 0) >= lax.broadcasted_iota(jnp.int32, (c, c), 1)
    for ci in range(GLA_TILE // c):
        rows = slice(ci * c, (ci + 1) * c)
        for h in range(N_HEADS):
            if h < A_HEADS:
                dv, st_ref, norm_g = A_DV, sta_ref.at[h], nga_ref[...]
                v_off, g_off, o_off = Z_IA + h * A_DV, Z_GA + h * A_DV, h * A_DV
            else:
                hb = h - A_HEADS
                dv, st_ref, norm_g = B_DV, stb_ref.at[hb], ngb_ref[...]
                v_off, g_off, o_off = Z_VB + hb * B_DV, Z_RB + hb * B_DV, A_V + hb * B_DV
            st = st_ref[...]
            k = ks_ref[h, rows, :]
            bc = bs_ref[h, rows, :]
            b_last = bc[c - 1:c]
            k_dec = (k * jnp.exp(b_last - bc)).astype(BF16)
            vb = z_ref[rows, v_off:v_off + dv].astype(BF16)
            scores = jnp.where(causal, sc_ref[h, ci], 0.0).astype(BF16)
            o = _dot_nt(q1_ref[h, rows, :], st.astype(BF16)) + _dot(scores, vb)
            st_ref[...] = st * jnp.exp(b_last) + _dot_tn(vb, k_dec)
            on = o * lax.rsqrt(jnp.mean(o * o, axis=-1, keepdims=True) + RMS_EPS) * norm_g
            gc = z_ref[rows, g_off:g_off + dv]
            o_ref[rows, o_off:o_off + dv] = (on * (gc * _sigmoid(gc))).astype(o_ref.dtype)


def _gla_kernel(z_ref, lr_ref, lb_ref, wup_ref, bup_ref, tri_ref, nga_ref, ngb_ref, o_ref,
                sta_ref, stb_ref, qs_ref, q1_ref, ks_ref, bs_ref, sc_ref):
    @pl.when(pl.program_id(1) == 0)
    def _():
        sta_ref[...] = jnp.zeros_like(sta_ref)
        stb_ref[...] = jnp.zeros_like(stb_ref)

    low = _gla_prep(z_ref, lr_ref, lb_ref, wup_ref, bup_ref, tri_ref, qs_ref, q1_ref, ks_ref, bs_ref)
    safe = jnp.min(low) >= GLA_SAFE_LOG_DECAY

    @pl.when(safe)
    def _product_form():
        _product_scores(q1_ref, ks_ref, bs_ref, sc_ref)

    @pl.when(jnp.logical_not(safe))
    def _exact_form():
        _exact_scores(qs_ref, ks_ref, bs_ref, sc_ref)

    _gla_outputs(z_ref, q1_ref, ks_ref, bs_ref, sc_ref, sta_ref, stb_ref, nga_ref, ngb_ref, o_ref)


def _gla(z, z_lr, lb, w_up, b_up, norm_a, norm_b, bsz, seq):
    ts = GLA_TILE
    nt = seq // ts
    n_chunks = ts // GLA_CHUNK
    row = lambda b, t: (b * nt + t, 0)
    const = lambda b, t: (0, 0)
    per_head = pltpu.VMEM((N_HEADS, ts, HEAD_DK), F32)
    return pl.pallas_call(
        _gla_kernel,
        grid=(bsz, nt),
        in_specs=[
            pl.BlockSpec((ts, Z_MIXER), row),
            pl.BlockSpec((ts, LANES), row),
            pl.BlockSpec((1, A_HEADS * A_DK), const),
            pl.BlockSpec((LANES, B_HEADS * B_DK), const),
            pl.BlockSpec((1, B_HEADS * B_DK), const),
            pl.BlockSpec((ts, ts), const),
            pl.BlockSpec((1, A_DV), const),
            pl.BlockSpec((1, B_DV), const),
        ],
        out_specs=pl.BlockSpec((ts, 2 * D_MODEL), row),
        out_shape=jax.ShapeDtypeStruct((bsz * seq, 2 * D_MODEL), BF16),
        scratch_shapes=[
            pltpu.VMEM((A_HEADS, A_DV, A_DK), F32),
            pltpu.VMEM((B_HEADS, B_DV, B_DK), F32),
            per_head, pltpu.VMEM((N_HEADS, ts, HEAD_DK), BF16), per_head, per_head,
            pltpu.VMEM((N_HEADS, n_chunks, GLA_CHUNK, GLA_CHUNK), F32),
        ],
        compiler_params=pltpu.CompilerParams(
            dimension_semantics=("parallel", "arbitrary"), vmem_limit_bytes=VMEM_LIMIT),
        name="gla",
    )(z, z_lr, lb, w_up, b_up, _chunk_tri(ts, GLA_CHUNK), norm_a, norm_b)


def _lane_pack(cols, dtype):
    m = cols[0].shape[0]
    lane = lax.broadcasted_iota(jnp.int32, (m, LANES), 1)
    out = jnp.zeros((m, LANES), dtype)
    for k, cvals in enumerate(cols):
        out = jnp.where(lane == k, cvals.astype(dtype), out)
    return out


def _post_mixer_kernel(o_ref, ga_ref, gb_ref, h0_ref, wa_ref, wb_ref, wo_ref, lg_ref, lb_ref,
                       wrh_ref, wrl_ref, br_ref, h1_ref, te_ref, tw_ref):
    ya = _dot(o_ref[:, :D_MODEL], wa_ref[...])
    yb = _dot(o_ref[:, D_MODEL:], wb_ref[...])
    merged = _sigmoid(ga_ref[...]) * ya + _sigmoid(gb_ref[...]) * yb
    y = _dot(merged.astype(BF16), wo_ref[...])
    h1 = _layer_norm(DN_ALPHA * h0_ref[...] + y, lg_ref[...], lb_ref[...])
    h1_ref[...] = h1

    hh = h1.astype(BF16)
    hl = (h1 - hh.astype(F32)).astype(BF16)
    logits = _dot(hh, wrh_ref[...]) + (_dot(hl, wrh_ref[...]) + _dot(hh, wrl_ref[...])) + br_ref[...]
    lane = lax.broadcasted_iota(jnp.int32, logits.shape, 1)
    work = jnp.where(lane < N_EXPERTS, logits, -jnp.inf)
    vals, idxs = [], []
    for _ in range(TOP_K):
        m = jnp.max(work, axis=-1, keepdims=True)
        idx = jnp.min(jnp.where(work == m, lane, LANES), axis=-1, keepdims=True)
        vals.append(m)
        idxs.append(idx)
        work = jnp.where(lane == idx, -jnp.inf, work)
    exps = [jnp.exp(v - vals[0]) for v in vals]
    denom = exps[0] + exps[1] + exps[2] + exps[3]
    te_ref[...] = _lane_pack(idxs, jnp.int32)
    tw_ref[...] = _lane_pack([e / denom for e in exps], F32)


def _post_mixer(o, z, h0, w_a, w_b, w_o, ln_g, ln_b, wr_hi, wr_lo, b_r, tm=512):
    t = h0.shape[0]
    row = lambda i: (i, 0)
    const = lambda i: (0, 0)
    full_w = pl.BlockSpec((D_MODEL, D_MODEL), const)
    vec = pl.BlockSpec((1, D_MODEL), const)
    return pl.pallas_call(
        _post_mixer_kernel,
        grid=(t // tm,),
        in_specs=[
            pl.BlockSpec((tm, 2 * D_MODEL), row),
            pl.BlockSpec((tm, D_MODEL), lambda i: (i, Z_GATE_A // D_MODEL)),
            pl.BlockSpec((tm, D_MODEL), lambda i: (i, Z_GATE_B // D_MODEL)),
            pl.BlockSpec((tm, D_MODEL), row),
            full_w, full_w, full_w, vec, vec,
            pl.BlockSpec((D_MODEL, LANES), const),
            pl.BlockSpec((D_MODEL, LANES), const),
            pl.BlockSpec((1, LANES), const),
        ],
        out_specs=[
            pl.BlockSpec((tm, D_MODEL), row),
            pl.BlockSpec((tm, LANES), row),
            pl.BlockSpec((tm, LANES), row),
        ],
        out_shape=[
            jax.ShapeDtypeStruct((t, D_MODEL), F32),
            jax.ShapeDtypeStruct((t, LANES), jnp.int32),
            jax.ShapeDtypeStruct((t, LANES), F32),
        ],
        compiler_params=pltpu.CompilerParams(
            dimension_semantics=("parallel",), vmem_limit_bytes=VMEM_LIMIT),
        name="post_mixer",
    )(o, z, z, h0, w_a, w_b, w_o, ln_g, ln_b, wr_hi, wr_lo, b_r)


def _rank_kernel(te_ref, rank_ref, cnt_ref, carry_ref):
    @pl.when(pl.program_id(0) == 0)
    def _():
        carry_ref[...] = jnp.zeros_like(carry_ref)

    te = te_ref[...]
    tr = te.shape[0]
    lane = lax.broadcasted_iota(jnp.int32, (tr, LANES), 1)
    onehots = [te[:, k:k + 1] == lane for k in range(TOP_K)]
    cnt = jnp.zeros((tr, LANES), F32)
    for oh in onehots:
        cnt = cnt + oh.astype(F32)
    strict_lower = (lax.broadcasted_iota(jnp.int32, (tr, tr), 0)
                    > lax.broadcasted_iota(jnp.int32, (tr, tr), 1))
    earlier = _dot(strict_lower.astype(BF16), cnt.astype(BF16)) + carry_ref[...]
    ranks = [jnp.sum(jnp.where(oh, earlier, 0.0), axis=-1, keepdims=True) for oh in onehots]
    rank_ref[...] = _lane_pack(ranks, jnp.int32)
    carry_ref[...] = carry_ref[...] + jnp.sum(cnt, axis=0, keepdims=True)
    cnt_ref[...] = carry_ref[...]


def _rank(top_e, tr=512):
    t = top_e.shape[0]
    return pl.pallas_call(
        _rank_kernel,
        grid=(t // tr,),
        in_specs=[pl.BlockSpec((tr, LANES), lambda i: (i, 0))],
        out_specs=[
            pl.BlockSpec((tr, LANES), lambda i: (i, 0)),
            pl.BlockSpec((1, LANES), lambda i: (0, 0)),
        ],
        out_shape=[
            jax.ShapeDtypeStruct((t, LANES), jnp.int32),
            jax.ShapeDtypeStruct((1, LANES), F32),
        ],
        scratch_shapes=[pltpu.VMEM((1, LANES), F32)],
        compiler_params=pltpu.CompilerParams(dimension_semantics=("arbitrary",)),
        name="rank",
    )(top_e)


def _row(ref, r):
    return ref.at[pl.ds(r, 1), :]


def _wait_rows(ref_hbm, n, sem):
    pltpu.make_async_copy(ref_hbm.at[pl.ds(0, n), :], ref_hbm.at[pl.ds(0, n), :], sem).wait()


def _dispatch_kernel(pad_start_ref, pad_cnt_ref, nu_ref, dest_ref, h_ref, p_ref, wpg_ref, wpp_ref,
                     xs_hbm, res_ref, zero_ref, sem):
    n_tok = h_ref.shape[0]

    @pl.when(pl.program_id(0) == 0)
    def _zero_rows_without_token():
        zero_ref[...] = jnp.zeros_like(zero_ref)

        def wait_pad_groups(n):
            @pl.when(n > 0)
            def _():
                _wait_rows(xs_hbm, pl.multiple_of(n * PAD_GROUP, PAD_GROUP), sem)

        def per_expert(e, n_prev):
            def start(g, c):
                r0 = pl.multiple_of(pad_start_ref[e] + g * PAD_GROUP, PAD_GROUP)
                pltpu.make_async_copy(zero_ref.at[pl.ds(0, PAD_GROUP), :],
                                      xs_hbm.at[pl.ds(r0, PAD_GROUP), :], sem).start()
                return c

            lax.fori_loop(0, pad_cnt_ref[e], start, 0)
            wait_pad_groups(n_prev)
            return pad_cnt_ref[e]

        wait_pad_groups(lax.fori_loop(0, N_EXPERTS, per_expert, 0))

        def block_copy(blk):
            r0 = pl.multiple_of(blk * MOE_ROWS, MOE_ROWS)
            return pltpu.make_async_copy(zero_ref, xs_hbm.at[pl.ds(r0, MOE_ROWS), :], sem)

        def start_block(blk, c):
            block_copy(blk).start()
            return c

        def wait_block(blk, c):
            block_copy(blk).wait()
            return c

        n_blocks = xs_hbm.shape[0] // MOE_ROWS
        lax.fori_loop(nu_ref[0], n_blocks, start_block, 0)
        lax.fori_loop(nu_ref[0], n_blocks, wait_block, 0)

    def issue(r, carry):
        for k in range(TOP_K):
            pltpu.make_async_copy(_row(h_ref, r), _row(xs_hbm, dest_ref[r * TOP_K + k]), sem).start()
        return carry

    lax.fori_loop(0, n_tok, issue, 0)
    h1 = h_ref[...]
    ple = _sigmoid(_dot(h1.astype(BF16), wpg_ref[...])) * _dot(p_ref[...].astype(BF16), wpp_ref[...])
    res_ref[...] = DN_ALPHA * h1 + ple
    for _ in range(n_tok * TOP_K // DMA_BATCH):
        _wait_rows(xs_hbm, DMA_BATCH, sem)


def _dispatch(pad_start, pad_cnt, n_used, dest_flat, h1, p, w_pg, w_pp, n_rows, tok_per_step=256):
    t = h1.shape[0]
    row = lambda i, ps, pc, nu: (i, 0)
    const = lambda i, ps, pc, nu: (0, 0)
    grid_spec = pltpu.PrefetchScalarGridSpec(
        num_scalar_prefetch=3,
        grid=(t // tok_per_step,),
        in_specs=[
            pl.BlockSpec((tok_per_step * TOP_K,), lambda i, ps, pc, nu: (i,), memory_space=pltpu.SMEM),
            pl.BlockSpec((tok_per_step, D_MODEL), row),
            pl.BlockSpec((tok_per_step, PLE_DIM), row),
            pl.BlockSpec((D_MODEL, D_MODEL), const),
            pl.BlockSpec((PLE_DIM, D_MODEL), const),
        ],
        out_specs=[
            pl.BlockSpec(memory_space=pl.ANY),
            pl.BlockSpec((tok_per_step, D_MODEL), row),
        ],
        scratch_shapes=[pltpu.VMEM((MOE_ROWS, D_MODEL), F32), pltpu.SemaphoreType.DMA(())],
    )
    return pl.pallas_call(
        _dispatch_kernel,
        grid_spec=grid_spec,
        out_shape=[
            jax.ShapeDtypeStruct((n_rows, D_MODEL), F32),
            jax.ShapeDtypeStruct((t, D_MODEL), F32),
        ],
        compiler_params=pltpu.CompilerParams(
            dimension_semantics=("arbitrary",), vmem_limit_bytes=VMEM_LIMIT),
        name="dispatch",
    )(pad_start, pad_cnt, n_used, dest_flat, h1, p, w_pg, w_pp)


def _experts_kernel(be_ref, nu_ref, x_ref, wgu_ref, bgu_ref, wd_ref, bd_ref, y_ref, wgu_bf_ref, wd_bf_ref):
    i = pl.program_id(0)
    used = i < nu_ref[0]

    @pl.when((i == 0) | (be_ref[i] != be_ref[jnp.maximum(i - 1, 0)]))
    def _():
        wgu_bf_ref[...] = wgu_ref[0].astype(BF16)
        wd_bf_ref[...] = wd_ref[0].astype(BF16)

    @pl.when(used)
    def _():
        hgu = _dot(x_ref[...].astype(BF16), wgu_bf_ref[...]) + bgu_ref[0]
        gate = jnp.minimum(hgu[:, :D_FF], SWIGLU_LIMIT)
        up = jnp.clip(hgu[:, D_FF:], -SWIGLU_LIMIT, SWIGLU_LIMIT)
        act = (up + 1.0) * gate * _sigmoid(SWIGLU_ALPHA * gate)
        y_ref[...] = _dot(act.astype(BF16), wd_bf_ref[...]) + bd_ref[0]

    @pl.when(jnp.logical_not(used))
    def _():
        y_ref[...] = jnp.zeros_like(y_ref)


def _experts(block_e, n_used, xs, w_gu, b_gu, w_d, b_d):
    n_rows = xs.shape[0]
    n_blocks = n_rows // MOE_ROWS
    grid_spec = pltpu.PrefetchScalarGridSpec(
        num_scalar_prefetch=2,
        grid=(n_blocks,),
        in_specs=[
            pl.BlockSpec((MOE_ROWS, D_MODEL), lambda i, be, nu: (jnp.minimum(i, nu[0] - 1), 0)),
            pl.BlockSpec((1, D_MODEL, 2 * D_FF), lambda i, be, nu: (be[i], 0, 0)),
            pl.BlockSpec((1, 1, 2 * D_FF), lambda i, be, nu: (be[i], 0, 0)),
            pl.BlockSpec((1, D_FF, D_MODEL), lambda i, be, nu: (be[i], 0, 0)),
            pl.BlockSpec((1, 1, D_MODEL), lambda i, be, nu: (be[i], 0, 0)),
        ],
        out_specs=pl.BlockSpec((MOE_ROWS, D_MODEL), lambda i, be, nu: (i, 0)),
        scratch_shapes=[pltpu.VMEM((D_MODEL, 2 * D_FF), BF16), pltpu.VMEM((D_FF, D_MODEL), BF16)],
    )
    return pl.pallas_call(
        _experts_kernel,
        grid_spec=grid_spec,
        out_shape=jax.ShapeDtypeStruct((n_rows, D_MODEL), F32),
        compiler_params=pltpu.CompilerParams(
            dimension_semantics=("arbitrary",), vmem_limit_bytes=VMEM_LIMIT),
        name="experts",
    )(block_e, n_used, xs, w_gu, b_gu, w_d, b_d)


def _final_kernel(dest_ref, dest_next_ref, ys_hbm, tw_ref, res_ref, lg_ref, lb_ref, o_ref, g_ref, sem):
    tm = res_ref.shape[0]
    i = pl.program_id(0)
    slot = i % 2

    def gather(idx_ref, s):
        def issue(row_group, carry):
            r0 = pl.multiple_of(row_group * F32_SUBLANES, F32_SUBLANES)
            for j in range(F32_SUBLANES):
                for k in range(TOP_K):
                    pltpu.make_async_copy(_row(ys_hbm, idx_ref[(r0 + j) * TOP_K + k]),
                                          _row(g_ref.at[s, k], r0 + j), sem.at[s]).start()
            return carry

        lax.fori_loop(0, tm // F32_SUBLANES, issue, 0)

    @pl.when(i == 0)
    def _():
        gather(dest_ref, 0)

    @pl.when(i + 1 < pl.num_programs(0))
    def _():
        gather(dest_next_ref, 1 - slot)

    for _ in range(tm * TOP_K // DMA_BATCH):
        _wait_rows(ys_hbm, DMA_BATCH, sem.at[slot])
    tw = tw_ref[...]
    acc = res_ref[...]
    for k in range(TOP_K):
        acc = acc + tw[:, k:k + 1] * g_ref[slot, k]
    o_ref[...] = _layer_norm(acc, lg_ref[...], lb_ref[...])


def _final(dest_flat, ys, top_w, res, ln_g, ln_b, tm=256):
    t = res.shape[0]
    n_steps = t // tm
    row = lambda i: (i, 0)
    const = lambda i: (0, 0)
    return pl.pallas_call(
        _final_kernel,
        grid=(n_steps,),
        in_specs=[
            pl.BlockSpec((tm * TOP_K,), lambda i: (i,), memory_space=pltpu.SMEM),
            pl.BlockSpec((tm * TOP_K,), lambda i: (jnp.minimum(i + 1, n_steps - 1),),
                         memory_space=pltpu.SMEM),
            pl.BlockSpec(memory_space=pl.ANY),
            pl.BlockSpec((tm, LANES), row),
            pl.BlockSpec((tm, D_MODEL), row),
            pl.BlockSpec((1, D_MODEL), const),
            pl.BlockSpec((1, D_MODEL), const),
        ],
        out_specs=pl.BlockSpec((tm, D_MODEL), row),
        out_shape=jax.ShapeDtypeStruct((t, D_MODEL), F32),
        scratch_shapes=[pltpu.VMEM((2, TOP_K, tm, D_MODEL), F32), pltpu.SemaphoreType.DMA((2,))],
        compiler_params=pltpu.CompilerParams(
            dimension_semantics=("arbitrary",), vmem_limit_bytes=VMEM_LIMIT),
        name="final",
    )(dest_flat, dest_flat, ys, top_w, res, ln_g, ln_b)


def kernel(x, p, emb_ln_g, emb_ln_b, hgrn_lb, w_in, w_gla_up, b_gla_up, norm_a_g, norm_b_g, w_proj_a, w_proj_b, w_out, ln_mix_g, ln_mix_b, w_router, b_router, w_gate_up, b_gate_up, w_down, b_down, w_ple_gate, w_ple_proj, ln_moe_g, ln_moe_b):
    bsz, seq, d = x.shape
    t = bsz * seq
    assert d == D_MODEL and seq % GLA_TILE == 0 and t % 1024 == 0
    assert w_in.shape[0] == DEPTH == 1
    vec = lambda a: a.reshape(1, -1).astype(F32)

    lb0 = jax.nn.softmax(hgrn_lb.astype(F32), axis=0)[0:1]
    w_in0 = w_in[0]
    w_main = jnp.concatenate([w_in0[:, :W_IN_LR], w_in0[:, W_IN_LR + GLA_RANK:]], axis=1).astype(BF16)
    w_lr = jnp.pad(w_in0[:, W_IN_LR:W_IN_LR + GLA_RANK], ((0, 0), (0, LANES - GLA_RANK))).astype(BF16)
    w_up = jnp.pad(w_gla_up[0], ((0, LANES - GLA_RANK), (0, 0))).astype(BF16)
    wr = jnp.pad(w_router[0].astype(F32), ((0, 0), (0, LANES - N_EXPERTS)))
    wr_hi = wr.astype(BF16)
    wr_lo = (wr - wr_hi.astype(F32)).astype(BF16)
    b_r = jnp.pad(b_router[0].astype(F32), (0, LANES - N_EXPERTS)).reshape(1, LANES)

    xt = x.reshape(t, d)
    h0, z, z_lr = _ln_inproj(xt, vec(emb_ln_g), vec(emb_ln_b), w_main, w_lr)
    o = _gla(z, z_lr, lb0, w_up, vec(b_gla_up[0]), vec(norm_a_g[0]), vec(norm_b_g[0]), bsz, seq)
    h1, top_e, top_w = _post_mixer(
        o, z, h0, w_proj_a[0].astype(BF16), w_proj_b[0].astype(BF16), w_out[0].astype(BF16),
        vec(ln_mix_g[0]), vec(ln_mix_b[0]), wr_hi, wr_lo, b_r)

    rank, cnt = _rank(top_e)
    counts = cnt[0, :N_EXPERTS].astype(jnp.int32)
    padded = (counts + MOE_ROWS - 1) // MOE_ROWS * MOE_ROWS
    pend = jnp.cumsum(padded)
    pstart = pend - padded
    n_blocks = t * TOP_K // MOE_ROWS + N_EXPERTS
    n_rows = n_blocks * MOE_ROWS
    n_used = pend[N_EXPERTS - 1:] // MOE_ROWS
    dest = (pstart[top_e[:, :TOP_K]] + rank[:, :TOP_K]).reshape(-1)
    block_pos = jnp.minimum(jnp.arange(n_blocks, dtype=jnp.int32), n_used - 1) * MOE_ROWS
    block_e = jnp.minimum(
        jnp.sum((pend[None, :] <= block_pos[:, None]).astype(jnp.int32), axis=1), N_EXPERTS - 1)

    pad_start = (pstart + counts) // PAD_GROUP * PAD_GROUP
    xs, res = _dispatch(pad_start, (pend - pad_start) // PAD_GROUP, n_used, dest, h1,
                        p[0].reshape(t, PLE_DIM), w_ple_gate[0].astype(BF16), w_ple_proj[0].astype(BF16),
                        n_rows)
    ys = _experts(block_e, n_used, xs, w_gate_up[0].astype(F32),
                  b_gate_up[0].reshape(N_EXPERTS, 1, 2 * D_FF).astype(F32),
                  w_down[0].astype(F32), b_down[0].reshape(N_EXPERTS, 1, d).astype(F32))
    out = _final(dest, ys, top_w, res, vec(ln_moe_g[0]), vec(ln_moe_b[0]))
    return out.reshape(bsz, seq, d)
```

```python
import jax
import jax.numpy as jnp
from jax import lax
from jax.experimental import pallas as pl
from jax.experimental.pallas import tpu as pltpu

F32 = jnp.float32
BF16 = jnp.bfloat16

D_MODEL = 1024
A_HEADS, A_DK, A_DV = 8, 128, 128
B_HEADS, B_DK, B_DV = 4, 128, 256
N_HEADS = A_HEADS + B_HEADS
HEAD_DK = 128
A_V = A_HEADS * A_DV
GLA_RANK = 16
GLA_TAU = 16.0
N_EXPERTS = 32
TOP_K = 4
D_FF = 1024
SWIGLU_LIMIT = 7.0
SWIGLU_ALPHA = 1.702
PLE_DIM = 256
LN_EPS = 1e-5
RMS_EPS = 1e-6
DEPTH = 1
DN_ALPHA = (2.0 * DEPTH) ** 0.25

LANES = 128
GLA_CHUNK = 64
GLA_TILE = 256
GLA_SAFE_LOG_DECAY = -60.0
MOE_ROWS = 512
DMA_BATCH = 128
F32_SUBLANES = 8
PAD_GROUP = F32_SUBLANES
VMEM_LIMIT = 56 * 1024 * 1024

Z_QA, Z_FA, Z_IA, Z_GA = 0, 1024, 2048, 3072
Z_QB, Z_KB, Z_VB, Z_RB = 4096, 4608, 5120, 6144
Z_MIXER = 7168
Z_GATE_A, Z_GATE_B = 7168, 8192
Z_COLS = 9216
W_IN_LR = 7168


def _layer_norm(x, g, b):
    mu = jnp.mean(x, axis=-1, keepdims=True)
    xc = x - mu
    var = jnp.mean(xc * xc, axis=-1, keepdims=True)
    return xc * lax.rsqrt(var + LN_EPS) * g + b


def _sigmoid(x):
    return jax.nn.sigmoid(x)


def _sigmoid_pair(x):
    e = jnp.exp(-jnp.abs(x))
    r = 1.0 / (1.0 + e)
    er = e * r
    pos = x >= 0.0
    return jnp.where(pos, r, er), jnp.where(pos, er, r)


def _dot(a, b):
    return jnp.dot(a, b, preferred_element_type=F32)


def _dot_nt(a, b):
    return lax.dot_general(a, b, (((1,), (1,)), ((), ())), preferred_element_type=F32)


def _dot_tn(a, b):
    return lax.dot_general(a, b, (((0,), (0,)), ((), ())), preferred_element_type=F32)


def _ln_inproj_kernel(x_ref, g_ref, b_ref, w_ref, wlr_ref, h_ref, z_ref, zlr_ref, hb_ref):
    @pl.when(pl.program_id(1) == 0)
    def _():
        h = _layer_norm(x_ref[...], g_ref[...], b_ref[...])
        h_ref[...] = h
        hb = h.astype(BF16)
        hb_ref[...] = hb
        zlr_ref[...] = _dot(hb, wlr_ref[...])

    z_ref[...] = _dot(hb_ref[...], w_ref[...])


def _ln_inproj(x, g, b, w_main, w_lr, tm=1024, tn=2304):
    t = x.shape[0]
    return pl.pallas_call(
        _ln_inproj_kernel,
        grid=(t // tm, Z_COLS // tn),
        in_specs=[
            pl.BlockSpec((tm, D_MODEL), lambda i, j: (i, 0)),
            pl.BlockSpec((1, D_MODEL), lambda i, j: (0, 0)),
            pl.BlockSpec((1, D_MODEL), lambda i, j: (0, 0)),
            pl.BlockSpec((D_MODEL, tn), lambda i, j: (0, j)),
            pl.BlockSpec((D_MODEL, LANES), lambda i, j: (0, 0)),
        ],
        out_specs=[
            pl.BlockSpec((tm, D_MODEL), lambda i, j: (i, 0)),
            pl.BlockSpec((tm, tn), lambda i, j: (i, j)),
            pl.BlockSpec((tm, LANES), lambda i, j: (i, 0)),
        ],
        out_shape=[
            jax.ShapeDtypeStruct((t, D_MODEL), F32),
            jax.ShapeDtypeStruct((t, Z_COLS), F32),
            jax.ShapeDtypeStruct((t, LANES), F32),
        ],
        scratch_shapes=[pltpu.VMEM((tm, D_MODEL), BF16)],
        compiler_params=pltpu.CompilerParams(
            dimension_semantics=("parallel", "arbitrary"), vmem_limit_bytes=VMEM_LIMIT),
        name="ln_inproj",
    )(x, g, b, w_main, w_lr)


def _chunk_tri(tile, chunk):
    i = jnp.arange(tile)[:, None]
    j = jnp.arange(tile)[None, :]
    return ((j <= i) & (i // chunk == j // chunk)).astype(BF16)


def _chunk_cumsum(tri, x):
    hi = x.astype(BF16)
    lo = (x - hi.astype(F32)).astype(BF16)
    return _dot(tri, hi) + _dot(tri, lo)


def _gla_prep(z_ref, lr_ref, lb_ref, wup_ref, bup_ref, tri_ref, qs_ref, q1_ref, ks_ref, bs_ref):
    low = None
    tri = tri_ref[...]

    def put(h, q, k, b):
        qs_ref[h] = q
        q1_ref[h] = (q * jnp.exp(b)).astype(BF16)
        ks_ref[h] = k
        bs_ref[h] = b

    for h in range(A_HEADS):
        cols = slice(h * A_DK, (h + 1) * A_DK)
        lb = lb_ref[:, cols]
        sig, sig_neg = _sigmoid_pair(z_ref[:, Z_FA + h * A_DK:Z_FA + (h + 1) * A_DK])
        b = _chunk_cumsum(tri, jnp.log(lb + (1.0 - lb) * sig))
        qz = z_ref[:, Z_QA + h * A_DK:Z_QA + (h + 1) * A_DK]
        put(h, qz * _sigmoid(qz) * (A_DK ** -0.5), (1.0 - lb) * sig_neg, b)
        low = b if low is None else jnp.minimum(low, b)
    u_all = _dot(lr_ref[...].astype(BF16), wup_ref[...]) + bup_ref[...]
    for hb in range(B_HEADS):
        u = u_all[:, hb * B_DK:(hb + 1) * B_DK]
        log_sig = jnp.minimum(u, 0.0) - jnp.log(1.0 + jnp.exp(-jnp.abs(u)))
        b = _chunk_cumsum(tri, log_sig / GLA_TAU)
        put(A_HEADS + hb, z_ref[:, Z_QB + hb * B_DK:Z_QB + (hb + 1) * B_DK] * (B_DK ** -0.5),
            z_ref[:, Z_KB + hb * B_DK:Z_KB + (hb + 1) * B_DK], b)
        low = jnp.minimum(low, b)
    return low


def _product_scores(q1_ref, ks_ref, bs_ref, sc_ref):
    c = GLA_CHUNK
    for h in range(N_HEADS):
        k1 = (ks_ref[h] * jnp.exp(-bs_ref[h])).astype(BF16)
        for ci in range(GLA_TILE // c):
            rows = slice(ci * c, (ci + 1) * c)
            sc_ref[h, ci] = _dot_nt(q1_ref[h, rows, :], k1[rows])


def _exact_scores(qs_ref, ks_ref, bs_ref, sc_ref):
    c = GLA_CHUNK
    n = GLA_TILE // c
    col = lax.broadcasted_iota(jnp.int32, (c, c), 1)

    def per_chunk(idx, carry):
        h = idx // n
        ci = idx - h * n
        r0 = pl.multiple_of(ci * c, c)
        qc = qs_ref[h, pl.ds(r0, c), :]
        bc = bs_ref[h, pl.ds(r0, c), :]

        def column(j, acc):
            kj = ks_ref[h, pl.ds(r0 + j, 1), :]
            bj = bs_ref[h, pl.ds(r0 + j, 1), :]
            term = qc * kj * jnp.exp(jnp.minimum(bc - bj, 0.0))
            return jnp.where(col == j, jnp.sum(term, axis=-1, keepdims=True), acc)

        sc_ref[h, ci] = lax.fori_loop(0, c, column, jnp.zeros((c, c), F32))
        return carry

    lax.fori_loop(0, N_HEADS * n, per_chunk, 0)


def _gla_outputs(z_ref, q1_ref, ks_ref, bs_ref, sc_ref, sta_ref, stb_ref, nga_ref, ngb_ref, o_ref):
    c = GLA_CHUNK
    causal = lax.broadcasted_iota(jnp.int32, (c, c), 0) >= lax.broadcasted_iota(jnp.int32, (c, c), 1)
    for ci in range(GLA_TILE // c):
        rows = slice(ci * c, (ci + 1) * c)
        for h in range(N_HEADS):
            if h < A_HEADS:
                dv, st_ref, norm_g = A_DV, sta_ref.at[h], nga_ref[...]
                v_off, g_off, o_off = Z_IA + h * A_DV, Z_GA + h * A_DV, h * A_DV
            else:
                hb = h - A_HEADS
                dv, st_ref, norm_g = B_DV, stb_ref.at[hb], ngb_ref[...]
                v_off, g_off, o_off = Z_VB + hb * B_DV, Z_RB + hb * B_DV, A_V + hb * B_DV
            st = st_ref[...]
            k = ks_ref[h, rows, :]
            bc = bs_ref[h, rows, :]
            b_last = bc[c - 1:c]
            k_dec = (k * jnp.exp(b_last - bc)).astype(BF16)
            vb = z_ref[rows, v_off:v_off + dv].astype(BF16)
            scores = jnp.where(causal, sc_ref[h, ci], 0.0).astype(BF16)
            o = _dot_nt(q1_ref[h, rows, :], st.astype(BF16)) + _dot(scores, vb)
            st_ref[...] = st * jnp.exp(b_last) + _dot_tn(vb, k_dec)
            on = o * lax.rsqrt(jnp.mean(o * o, axis=-1, keepdims=True) + RMS_EPS) * norm_g
            gc = z_ref[rows, g_off:g_off + dv]
            o_ref[rows, o_off:o_off + dv] = (on * (gc * _sigmoid(gc))).astype(o_ref.dtype)


def _gla_kernel(z_ref, lr_ref, lb_ref, wup_ref, bup_ref, tri_ref, nga_ref, ngb_ref, o_ref,
                sta_ref, stb_ref, qs_ref, q1_ref, ks_ref, bs_ref, sc_ref):
    @pl.when(pl.program_id(1) == 0)
    def _():
        sta_ref[...] = jnp.zeros_like(sta_ref)
        stb_ref[...] = jnp.zeros_like(stb_ref)

    low = _gla_prep(z_ref, lr_ref, lb_ref, wup_ref, bup_ref, tri_ref, qs_ref, q1_ref, ks_ref, bs_ref)
    safe = jnp.min(low) >= GLA_SAFE_LOG_DECAY

    @pl.when(safe)
    def _product_form():
        _product_scores(q1_ref, ks_ref, bs_ref, sc_ref)

    @pl.when(jnp.logical_not(safe))
    def _exact_form():
        _exact_scores(qs_ref, ks_ref, bs_ref, sc_ref)

    _gla_outputs(z_ref, q1_ref, ks_ref, bs_ref, sc_ref, sta_ref, stb_ref, nga_ref, ngb_ref, o_ref)


def _gla(z, z_lr, lb, w_up, b_up, norm_a, norm_b, bsz, seq):
    ts = GLA_TILE
    nt = seq // ts
    n_chunks = ts // GLA_CHUNK
    row = lambda b, t: (b * nt + t, 0)
    const = lambda b, t: (0, 0)
    per_head = pltpu.VMEM((N_HEADS, ts, HEAD_DK), F32)
    return pl.pallas_call(
        _gla_kernel,
        grid=(bsz, nt),
        in_specs=[
            pl.BlockSpec((ts, Z_MIXER), row),
            pl.BlockSpec((ts, LANES), row),
            pl.BlockSpec((1, A_HEADS * A_DK), const),
            pl.BlockSpec((LANES, B_HEADS * B_DK), const),
            pl.BlockSpec((1, B_HEADS * B_DK), const),
            pl.BlockSpec((ts, ts), const),
            pl.BlockSpec((1, A_DV), const),
            pl.BlockSpec((1, B_DV), const),
        ],
        out_specs=pl.BlockSpec((ts, 2 * D_MODEL), row),
        out_shape=jax.ShapeDtypeStruct((bsz * seq, 2 * D_MODEL), BF16),
        scratch_shapes=[
            pltpu.VMEM((A_HEADS, A_DV, A_DK), F32),
            pltpu.VMEM((B_HEADS, B_DV, B_DK), F32),
            per_head, pltpu.VMEM((N_HEADS, ts, HEAD_DK), BF16), per_head, per_head,
            pltpu.VMEM((N_HEADS, n_chunks, GLA_CHUNK, GLA_CHUNK), F32),
        ],
        compiler_params=pltpu.CompilerParams(
            dimension_semantics=("parallel", "arbitrary"), vmem_limit_bytes=VMEM_LIMIT),
        name="gla",
    )(z, z_lr, lb, w_up, b_up, _chunk_tri(ts, GLA_CHUNK), norm_a, norm_b)


def _lane_pack(cols, dtype):
    m = cols[0].shape[0]
    lane = lax.broadcasted_iota(jnp.int32, (m, LANES), 1)
    out = jnp.zeros((m, LANES), dtype)
    for k, cvals in enumerate(cols):
        out = jnp.where(lane == k, cvals.astype(dtype), out)
    return out


def _post_mixer_kernel(o_ref, ga_ref, gb_ref, h0_ref, wa_ref, wb_ref, wo_ref, lg_ref, lb_ref,
                       wrh_ref, wrl_ref, br_ref, h1_ref, te_ref, tw_ref):
    ya = _dot(o_ref[:, :D_MODEL], wa_ref[...])
    yb = _dot(o_ref[:, D_MODEL:], wb_ref[...])
    merged = _sigmoid(ga_ref[...]) * ya + _sigmoid(gb_ref[...]) * yb
    y = _dot(merged.astype(BF16), wo_ref[...])
    h1 = _layer_norm(DN_ALPHA * h0_ref[...] + y, lg_ref[...], lb_ref[...])
    h1_ref[...] = h1

    hh = h1.astype(BF16)
    hl = (h1 - hh.astype(F32)).astype(BF16)
    logits = _dot(hh, wrh_ref[...]) + (_dot(hl, wrh_ref[...]) + _dot(hh, wrl_ref[...])) + br_ref[...]
    lane = lax.broadcasted_iota(jnp.int32, logits.shape, 1)
    work = jnp.where(lane < N_EXPERTS, logits, -jnp.inf)
    vals, idxs = [], []
    for _ in range(TOP_K):
        m = jnp.max(work, axis=-1, keepdims=True)
        idx = jnp.min(jnp.where(work == m, lane, LANES), axis=-1, keepdims=True)
        vals.append(m)
        idxs.append(idx)
        work = jnp.where(lane == idx, -jnp.inf, work)
    exps = [jnp.exp(v - vals[0]) for v in vals]
    denom = exps[0] + exps[1] + exps[2] + exps[3]
    te_ref[...] = _lane_pack(idxs, jnp.int32)
    tw_ref[...] = _lane_pack([e / denom for e in exps], F32)


def _post_mixer(o, z, h0, w_a, w_b, w_o, ln_g, ln_b, wr_hi, wr_lo, b_r, tm=512):
    t = h0.shape[0]
    row = lambda i: (i, 0)
    const = lambda i: (0, 0)
    full_w = pl.BlockSpec((D_MODEL, D_MODEL), const)
    vec = pl.BlockSpec((1, D_MODEL), const)
    return pl.pallas_call(
        _post_mixer_kernel,
        grid=(t // tm,),
        in_specs=[
            pl.BlockSpec((tm, 2 * D_MODEL), row),
            pl.BlockSpec((tm, D_MODEL), lambda i: (i, Z_GATE_A // D_MODEL)),
            pl.BlockSpec((tm, D_MODEL), lambda i: (i, Z_GATE_B // D_MODEL)),
            pl.BlockSpec((tm, D_MODEL), row),
            full_w, full_w, full_w, vec, vec,
            pl.BlockSpec((D_MODEL, LANES), const),
            pl.BlockSpec((D_MODEL, LANES), const),
            pl.BlockSpec((1, LANES), const),
        ],
        out_specs=[
            pl.BlockSpec((tm, D_MODEL), row),
            pl.BlockSpec((tm, LANES), row),
            pl.BlockSpec((tm, LANES), row),
        ],
        out_shape=[
            jax.ShapeDtypeStruct((t, D_MODEL), F32),
            jax.ShapeDtypeStruct((t, LANES), jnp.int32),
            jax.ShapeDtypeStruct((t, LANES), F32),
        ],
        compiler_params=pltpu.CompilerParams(
            dimension_semantics=("parallel",), vmem_limit_bytes=VMEM_LIMIT),
        name="post_mixer",
    )(o, z, z, h0, w_a, w_b, w_o, ln_g, ln_b, wr_hi, wr_lo, b_r)


def _rank_kernel(te_ref, rank_ref, cnt_ref, carry_ref):
    @pl.when(pl.program_id(0) == 0)
    def _():
        carry_ref[...] = jnp.zeros_like(carry_ref)

    te = te_ref[...]
    tr = te.shape[0]
    lane = lax.broadcasted_iota(jnp.int32, (tr, LANES), 1)
    onehots = [te[:, k:k + 1] == lane for k in range(TOP_K)]
    cnt = jnp.zeros((tr, LANES), F32)
    for oh in onehots:
        cnt = cnt + oh.astype(F32)
    strict_lower = (lax.broadcasted_iota(jnp.int32, (tr, tr), 0)
                    > lax.broadcasted_iota(jnp.int32, (tr, tr), 1))
    earlier = _dot(strict_lower.astype(BF16), cnt.astype(BF16)) + carry_ref[...]
    ranks = [jnp.sum(jnp.where(oh, earlier, 0.0), axis=-1, keepdims=True) for oh in onehots]
    rank_ref[...] = _lane_pack(ranks, jnp.int32)
    carry_ref[...] = carry_ref[...] + jnp.sum(cnt, axis=0, keepdims=True)
    cnt_ref[...] = carry_ref[...]


def _rank(top_e, tr=512):
    t = top_e.shape[0]
    return pl.pallas_call(
        _rank_kernel,
        grid=(t // tr,),
        in_specs=[pl.BlockSpec((tr, LANES), lambda i: (i, 0))],
        out_specs=[
            pl.BlockSpec((tr, LANES), lambda i: (i, 0)),
            pl.BlockSpec((1, LANES), lambda i: (0, 0)),
        ],
        out_shape=[
            jax.ShapeDtypeStruct((t, LANES), jnp.int32),
            jax.ShapeDtypeStruct((1, LANES), F32),
        ],
        scratch_shapes=[pltpu.VMEM((1, LANES), F32)],
        compiler_params=pltpu.CompilerParams(dimension_semantics=("arbitrary",)),
        name="rank",
    )(top_e)


def _row(ref, r):
    return ref.at[pl.ds(r, 1), :]


def _wait_rows(ref_hbm, n, sem):
    pltpu.make_async_copy(ref_hbm.at[pl.ds(0, n), :], ref_hbm.at[pl.ds(0, n), :], sem).wait()


def _dispatch_kernel(pad_start_ref, pad_cnt_ref, nu_ref, dest_ref, h_ref, p_ref, wpg_ref, wpp_ref,
                     xs_hbm, res_ref, zero_ref, sem):
    n_tok = h_ref.shape[0]

    @pl.when(pl.program_id(0) == 0)
    def _zero_rows_without_token():
        zero_ref[...] = jnp.zeros_like(zero_ref)

        def wait_pad_groups(n):
            @pl.when(n > 0)
            def _():
                _wait_rows(xs_hbm, pl.multiple_of(n * PAD_GROUP, PAD_GROUP), sem)

        def per_expert(e, n_prev):
            def start(g, c):
                r0 = pl.multiple_of(pad_start_ref[e] + g * PAD_GROUP, PAD_GROUP)
                pltpu.make_async_copy(zero_ref.at[pl.ds(0, PAD_GROUP), :],
                                      xs_hbm.at[pl.ds(r0, PAD_GROUP), :], sem).start()
                return c

            lax.fori_loop(0, pad_cnt_ref[e], start, 0)
            wait_pad_groups(n_prev)
            return pad_cnt_ref[e]

        wait_pad_groups(lax.fori_loop(0, N_EXPERTS, per_expert, 0))

        def block_copy(blk):
            r0 = pl.multiple_of(blk * MOE_ROWS, MOE_ROWS)
            return pltpu.make_async_copy(zero_ref, xs_hbm.at[pl.ds(r0, MOE_ROWS), :], sem)

        def start_block(blk, c):
            block_copy(blk).start()
            return c

        def wait_block(blk, c):
            block_copy(blk).wait()
            return c

        n_blocks = xs_hbm.shape[0] // MOE_ROWS
        lax.fori_loop(nu_ref[0], n_blocks, start_block, 0)
        lax.fori_loop(nu_ref[0], n_blocks, wait_block, 0)

    for r in range(n_tok):
        for k in range(TOP_K):
            pltpu.make_async_copy(_row(h_ref, r), _row(xs_hbm, dest_ref[r * TOP_K + k]), sem).start()
    h1 = h_ref[...]
    ple = _sigmoid(_dot(h1.astype(BF16), wpg_ref[...])) * _dot(p_ref[...].astype(BF16), wpp_ref[...])
    res_ref[...] = DN_ALPHA * h1 + ple
    for _ in range(n_tok * TOP_K // DMA_BATCH):
        _wait_rows(xs_hbm, DMA_BATCH, sem)


def _dispatch(pad_start, pad_cnt, n_used, dest_flat, h1, p, w_pg, w_pp, n_rows, tok_per_step=256):
    t = h1.shape[0]
    row = lambda i, ps, pc, nu: (i, 0)
    const = lambda i, ps, pc, nu: (0, 0)
    grid_spec = pltpu.PrefetchScalarGridSpec(
        num_scalar_prefetch=3,
        grid=(t // tok_per_step,),
        in_specs=[
            pl.BlockSpec((tok_per_step * TOP_K,), lambda i, ps, pc, nu: (i,), memory_space=pltpu.SMEM),
            pl.BlockSpec((tok_per_step, D_MODEL), row),
            pl.BlockSpec((tok_per_step, PLE_DIM), row),
            pl.BlockSpec((D_MODEL, D_MODEL), const),
            pl.BlockSpec((PLE_DIM, D_MODEL), const),
        ],
        out_specs=[
            pl.BlockSpec(memory_space=pl.ANY),
            pl.BlockSpec((tok_per_step, D_MODEL), row),
        ],
        scratch_shapes=[pltpu.VMEM((MOE_ROWS, D_MODEL), F32), pltpu.SemaphoreType.DMA(())],
    )
    return pl.pallas_call(
        _dispatch_kernel,
        grid_spec=grid_spec,
        out_shape=[
            jax.ShapeDtypeStruct((n_rows, D_MODEL), F32),
            jax.ShapeDtypeStruct((t, D_MODEL), F32),
        ],
        compiler_params=pltpu.CompilerParams(
            dimension_semantics=("arbitrary",), vmem_limit_bytes=VMEM_LIMIT),
        name="dispatch",
    )(pad_start, pad_cnt, n_used, dest_flat, h1, p, w_pg, w_pp)


def _experts_kernel(be_ref, nu_ref, x_ref, wgu_ref, bgu_ref, wd_ref, bd_ref, y_ref, wgu_bf_ref, wd_bf_ref):
    i = pl.program_id(0)
    used = i < nu_ref[0]

    @pl.when((i == 0) | (be_ref[i] != be_ref[jnp.maximum(i - 1, 0)]))
    def _():
        wgu_bf_ref[...] = wgu_ref[0].astype(BF16)
        wd_bf_ref[...] = wd_ref[0].astype(BF16)

    @pl.when(used)
    def _():
        hgu = _dot(x_ref[...].astype(BF16), wgu_bf_ref[...]) + bgu_ref[0]
        gate = jnp.minimum(hgu[:, :D_FF], SWIGLU_LIMIT)
        up = jnp.clip(hgu[:, D_FF:], -SWIGLU_LIMIT, SWIGLU_LIMIT)
        act = (up + 1.0) * gate * _sigmoid(SWIGLU_ALPHA * gate)
        y_ref[...] = _dot(act.astype(BF16), wd_bf_ref[...]) + bd_ref[0]

    @pl.when(jnp.logical_not(used))
    def _():
        y_ref[...] = jnp.zeros_like(y_ref)


def _experts(block_e, n_used, xs, w_gu, b_gu, w_d, b_d):
    n_rows = xs.shape[0]
    n_blocks = n_rows // MOE_ROWS
    grid_spec = pltpu.PrefetchScalarGridSpec(
        num_scalar_prefetch=2,
        grid=(n_blocks,),
        in_specs=[
            pl.BlockSpec((MOE_ROWS, D_MODEL), lambda i, be, nu: (jnp.minimum(i, nu[0] - 1), 0)),
            pl.BlockSpec((1, D_MODEL, 2 * D_FF), lambda i, be, nu: (be[i], 0, 0)),
            pl.BlockSpec((1, 1, 2 * D_FF), lambda i, be, nu: (be[i], 0, 0)),
            pl.BlockSpec((1, D_FF, D_MODEL), lambda i, be, nu: (be[i], 0, 0)),
            pl.BlockSpec((1, 1, D_MODEL), lambda i, be, nu: (be[i], 0, 0)),
        ],
        out_specs=pl.BlockSpec((MOE_ROWS, D_MODEL), lambda i, be, nu: (i, 0)),
        scratch_shapes=[pltpu.VMEM((D_MODEL, 2 * D_FF), BF16), pltpu.VMEM((D_FF, D_MODEL), BF16)],
    )
    return pl.pallas_call(
        _experts_kernel,
        grid_spec=grid_spec,
        out_shape=jax.ShapeDtypeStruct((n_rows, D_MODEL), F32),
        compiler_params=pltpu.CompilerParams(
            dimension_semantics=("arbitrary",), vmem_limit_bytes=VMEM_LIMIT),
        name="experts",
    )(block_e, n_used, xs, w_gu, b_gu, w_d, b_d)


def _final_kernel(dest_ref, dest_next_ref, ys_hbm, tw_ref, res_ref, lg_ref, lb_ref, o_ref, g_ref, sem):
    tm = res_ref.shape[0]
    i = pl.program_id(0)
    slot = i % 2

    def gather(idx_ref, s):
        def issue(row_group, carry):
            r0 = pl.multiple_of(row_group * F32_SUBLANES, F32_SUBLANES)
            for j in range(F32_SUBLANES):
                for k in range(TOP_K):
                    pltpu.make_async_copy(_row(ys_hbm, idx_ref[(r0 + j) * TOP_K + k]),
                                          _row(g_ref.at[s, k], r0 + j), sem.at[s]).start()
            return carry

        lax.fori_loop(0, tm // F32_SUBLANES, issue, 0)

    @pl.when(i == 0)
    def _():
        gather(dest_ref, 0)

    @pl.when(i + 1 < pl.num_programs(0))
    def _():
        gather(dest_next_ref, 1 - slot)

    for _ in range(tm * TOP_K // DMA_BATCH):
        _wait_rows(ys_hbm, DMA_BATCH, sem.at[slot])
    tw = tw_ref[...]
    acc = res_ref[...]
    for k in range(TOP_K):
        acc = acc + tw[:, k:k + 1] * g_ref[slot, k]
    o_ref[...] = _layer_norm(acc, lg_ref[...], lb_ref[...])


def _final(dest_flat, ys, top_w, res, ln_g, ln_b, tm=256):
    t = res.shape[0]
    n_steps = t // tm
    row = lambda i: (i, 0)
    const = lambda i: (0, 0)
    return pl.pallas_call(
        _final_kernel,
        grid=(n_steps,),
        in_specs=[
            pl.BlockSpec((tm * TOP_K,), lambda i: (i,), memory_space=pltpu.SMEM),
            pl.BlockSpec((tm * TOP_K,), lambda i: (jnp.minimum(i + 1, n_steps - 1),),
                         memory_space=pltpu.SMEM),
            pl.BlockSpec(memory_space=pl.ANY),
            pl.BlockSpec((tm, LANES), row),
            pl.BlockSpec((tm, D_MODEL), row),
            pl.BlockSpec((1, D_MODEL), const),
            pl.BlockSpec((1, D_MODEL), const),
        ],
        out_specs=pl.BlockSpec((tm, D_MODEL), row),
        out_shape=jax.ShapeDtypeStruct((t, D_MODEL), F32),
        scratch_shapes=[pltpu.VMEM((2, TOP_K, tm, D_MODEL), F32), pltpu.SemaphoreType.DMA((2,))],
        compiler_params=pltpu.CompilerParams(
            dimension_semantics=("arbitrary",), vmem_limit_bytes=VMEM_LIMIT),
        name="final",
    )(dest_flat, dest_flat, ys, top_w, res, ln_g, ln_b)


def kernel(x, p, emb_ln_g, emb_ln_b, hgrn_lb, w_in, w_gla_up, b_gla_up, norm_a_g, norm_b_g, w_proj_a, w_proj_b, w_out, ln_mix_g, ln_mix_b, w_router, b_router, w_gate_up, b_gate_up, w_down, b_down, w_ple_gate, w_ple_proj, ln_moe_g, ln_moe_b):
    bsz, seq, d = x.shape
    t = bsz * seq
    assert d == D_MODEL and seq % GLA_TILE == 0 and t % 1024 == 0
    assert w_in.shape[0] == DEPTH == 1
    vec = lambda a: a.reshape(1, -1).astype(F32)

    lb0 = jax.nn.softmax(hgrn_lb.astype(F32), axis=0)[0:1]
    w_in0 = w_in[0]
    w_main = jnp.concatenate([w_in0[:, :W_IN_LR], w_in0[:, W_IN_LR + GLA_RANK:]], axis=1).astype(BF16)
    w_lr = jnp.pad(w_in0[:, W_IN_LR:W_IN_LR + GLA_RANK], ((0, 0), (0, LANES - GLA_RANK))).astype(BF16)
    w_up = jnp.pad(w_gla_up[0], ((0, LANES - GLA_RANK), (0, 0))).astype(BF16)
    wr = jnp.pad(w_router[0].astype(F32), ((0, 0), (0, LANES - N_EXPERTS)))
    wr_hi = wr.astype(BF16)
    wr_lo = (wr - wr_hi.astype(F32)).astype(BF16)
    b_r = jnp.pad(b_router[0].astype(F32), (0, LANES - N_EXPERTS)).reshape(1, LANES)

    xt = x.reshape(t, d)
    h0, z, z_lr = _ln_inproj(xt, vec(emb_ln_g), vec(emb_ln_b), w_main, w_lr)
    o = _gla(z, z_lr, lb0, w_up, vec(b_gla_up[0]), vec(norm_a_g[0]), vec(norm_b_g[0]), bsz, seq)
    h1, top_e, top_w = _post_mixer(
        o, z, h0, w_proj_a[0].astype(BF16), w_proj_b[0].astype(BF16), w_out[0].astype(BF16),
        vec(ln_mix_g[0]), vec(ln_mix_b[0]), wr_hi, wr_lo, b_r)

    rank, cnt = _rank(top_e)
    counts = cnt[0, :N_EXPERTS].astype(jnp.int32)
    padded = (counts + MOE_ROWS - 1) // MOE_ROWS * MOE_ROWS
    pend = jnp.cumsum(padded)
    pstart = pend - padded
    n_blocks = t * TOP_K // MOE_ROWS + N_EXPERTS
    n_rows = n_blocks * MOE_ROWS
    n_used = pend[N_EXPERTS - 1:] // MOE_ROWS
    dest = (pstart[top_e[:, :TOP_K]] + rank[:, :TOP_K]).reshape(-1)
    block_pos = jnp.minimum(jnp.arange(n_blocks, dtype=jnp.int32), n_used - 1) * MOE_ROWS
    block_e = jnp.minimum(
        jnp.sum((pend[None, :] <= block_pos[:, None]).astype(jnp.int32), axis=1), N_EXPERTS - 1)

    pad_start = (pstart + counts) // PAD_GROUP * PAD_GROUP
    xs, res = _dispatch(pad_start, (pend - pad_start) // PAD_GROUP, n_used, dest, h1,
                        p[0].reshape(t, PLE_DIM), w_ple_gate[0].astype(BF16), w_ple_proj[0].astype(BF16),
                        n_rows)
    ys = _experts(block_e, n_used, xs, w_gate_up[0].astype(F32),
                  b_gate_up[0].reshape(N_EXPERTS, 1, 2 * D_FF).astype(F32),
                  w_down[0].astype(F32), b_down[0].reshape(N_EXPERTS, 1, d).astype(F32))
    out = _final(dest, ys, top_w, res, vec(ln_moe_g[0]), vec(ln_moe_b[0]))
    return out.reshape(bsz, seq, d)
```

```python
import jax
import jax.numpy as jnp
from jax import lax
from jax.experimental import pallas as pl
from jax.experimental.pallas import tpu as pltpu

F32 = jnp.float32
BF16 = jnp.bfloat16

D_MODEL = 1024
A_HEADS, A_DK, A_DV = 8, 128, 128
B_HEADS, B_DK, B_DV = 4, 128, 256
N_HEADS = A_HEADS + B_HEADS
HEAD_DK = 128
A_V = A_HEADS * A_DV
GLA_RANK = 16
GLA_TAU = 16.0
N_EXPERTS = 32
TOP_K = 4
D_FF = 1024
SWIGLU_LIMIT = 7.0
SWIGLU_ALPHA = 1.702
PLE_DIM = 256
LN_EPS = 1e-5
RMS_EPS = 1e-6
DEPTH = 1
DN_ALPHA = (2.0 * DEPTH) ** 0.25

LANES = 128
GLA_CHUNK = 64
GLA_TILE = 256
GLA_SAFE_LOG_DECAY = -60.0
MOE_ROWS = 512
DMA_BATCH = 128
DMA_QUEUES = 2
F32_SUBLANES = 8
PAD_GROUP = F32_SUBLANES
VMEM_LIMIT = 56 * 1024 * 1024

Z_QA, Z_FA, Z_IA, Z_GA = 0, 1024, 2048, 3072
Z_QB, Z_KB, Z_VB, Z_RB = 4096, 4608, 5120, 6144
Z_MIXER = 7168
Z_GATE_A, Z_GATE_B = 7168, 8192
Z_COLS = 9216
W_IN_LR = 7168


def _layer_norm(x, g, b):
    mu = jnp.mean(x, axis=-1, keepdims=True)
    xc = x - mu
    var = jnp.mean(xc * xc, axis=-1, keepdims=True)
    return xc * lax.rsqrt(var + LN_EPS) * g + b


def _sigmoid(x):
    return jax.nn.sigmoid(x)


def _sigmoid_pair(x):
    e = jnp.exp(-jnp.abs(x))
    r = 1.0 / (1.0 + e)
    er = e * r
    pos = x >= 0.0
    return jnp.where(pos, r, er), jnp.where(pos, er, r)


def _dot(a, b):
    return jnp.dot(a, b, preferred_element_type=F32)


def _dot_nt(a, b):
    return lax.dot_general(a, b, (((1,), (1,)), ((), ())), preferred_element_type=F32)


def _dot_tn(a, b):
    return lax.dot_general(a, b, (((0,), (0,)), ((), ())), preferred_element_type=F32)


def _ln_inproj_kernel(x_ref, g_ref, b_ref, w_ref, wlr_ref, h_ref, z_ref, zlr_ref, hb_ref):
    @pl.when(pl.program_id(1) == 0)
    def _():
        h = _layer_norm(x_ref[...], g_ref[...], b_ref[...])
        h_ref[...] = h
        hb = h.astype(BF16)
        hb_ref[...] = hb
        zlr_ref[...] = _dot(hb, wlr_ref[...])

    z_ref[...] = _dot(hb_ref[...], w_ref[...])


def _ln_inproj(x, g, b, w_main, w_lr, tm=1024, tn=2304):
    t = x.shape[0]
    return pl.pallas_call(
        _ln_inproj_kernel,
        grid=(t // tm, Z_COLS // tn),
        in_specs=[
            pl.BlockSpec((tm, D_MODEL), lambda i, j: (i, 0)),
            pl.BlockSpec((1, D_MODEL), lambda i, j: (0, 0)),
            pl.BlockSpec((1, D_MODEL), lambda i, j: (0, 0)),
            pl.BlockSpec((D_MODEL, tn), lambda i, j: (0, j)),
            pl.BlockSpec((D_MODEL, LANES), lambda i, j: (0, 0)),
        ],
        out_specs=[
            pl.BlockSpec((tm, D_MODEL), lambda i, j: (i, 0)),
            pl.BlockSpec((tm, tn), lambda i, j: (i, j)),
            pl.BlockSpec((tm, LANES), lambda i, j: (i, 0)),
        ],
        out_shape=[
            jax.ShapeDtypeStruct((t, D_MODEL), F32),
            jax.ShapeDtypeStruct((t, Z_COLS), F32),
            jax.ShapeDtypeStruct((t, LANES), F32),
        ],
        scratch_shapes=[pltpu.VMEM((tm, D_MODEL), BF16)],
        compiler_params=pltpu.CompilerParams(
            dimension_semantics=("parallel", "arbitrary"), vmem_limit_bytes=VMEM_LIMIT),
        name="ln_inproj",
    )(x, g, b, w_main, w_lr)


def _chunk_tri(tile, chunk):
    i = jnp.arange(tile)[:, None]
    j = jnp.arange(tile)[None, :]
    return ((j <= i) & (i // chunk == j // chunk)).astype(BF16)


def _chunk_cumsum(tri, x):
    hi = x.astype(BF16)
    lo = (x - hi.astype(F32)).astype(BF16)
    return _dot(tri, hi) + _dot(tri, lo)


def _gla_prep(z_ref, lr_ref, lb_ref, wup_ref, bup_ref, tri_ref, qs_ref, q1_ref, ks_ref, bs_ref):
    low = None
    tri = tri_ref[...]

    def put(h, q, k, b):
        qs_ref[h] = q
        q1_ref[h] = (q * jnp.exp(b)).astype(BF16)
        ks_ref[h] = k
        bs_ref[h] = b

    for h in range(A_HEADS):
        cols = slice(h * A_DK, (h + 1) * A_DK)
        lb = lb_ref[:, cols]
        sig, sig_neg = _sigmoid_pair(z_ref[:, Z_FA + h * A_DK:Z_FA + (h + 1) * A_DK])
        b = _chunk_cumsum(tri, jnp.log(lb + (1.0 - lb) * sig))
        qz = z_ref[:, Z_QA + h * A_DK:Z_QA + (h + 1) * A_DK]
        put(h, qz * _sigmoid(qz) * (A_DK ** -0.5), (1.0 - lb) * sig_neg, b)
        low = b if low is None else jnp.minimum(low, b)
    u_all = _dot(lr_ref[...].astype(BF16), wup_ref[...]) + bup_ref[...]
    for hb in range(B_HEADS):
        u = u_all[:, hb * B_DK:(hb + 1) * B_DK]
        log_sig = jnp.minimum(u, 0.0) - jnp.log(1.0 + jnp.exp(-jnp.abs(u)))
        b = _chunk_cumsum(tri, log_sig / GLA_TAU)
        put(A_HEADS + hb, z_ref[:, Z_QB + hb * B_DK:Z_QB + (hb + 1) * B_DK] * (B_DK ** -0.5),
            z_ref[:, Z_KB + hb * B_DK:Z_KB + (hb + 1) * B_DK], b)
        low = jnp.minimum(low, b)
    return low


def _product_scores(q1_ref, ks_ref, bs_ref, sc_ref):
    c = GLA_CHUNK
    for h in range(N_HEADS):
        k1 = (ks_ref[h] * jnp.exp(-bs_ref[h])).astype(BF16)
        for ci in range(GLA_TILE // c):
            rows = slice(ci * c, (ci + 1) * c)
            sc_ref[h, ci] = _dot_nt(q1_ref[h, rows, :], k1[rows])


def _exact_scores(qs_ref, ks_ref, bs_ref, sc_ref):
    c = GLA_CHUNK
    n = GLA_TILE // c
    col = lax.broadcasted_iota(jnp.int32, (c, c), 1)

    def per_chunk(idx, carry):
        h = idx // n
        ci = idx - h * n
        r0 = pl.multiple_of(ci * c, c)
        qc = qs_ref[h, pl.ds(r0, c), :]
        bc = bs_ref[h, pl.ds(r0, c), :]

        def column(j, acc):
            kj = ks_ref[h, pl.ds(r0 + j, 1), :]
            bj = bs_ref[h, pl.ds(r0 + j, 1), :]
            term = qc * kj * jnp.exp(jnp.minimum(bc - bj, 0.0))
            return jnp.where(col == j, jnp.sum(term, axis=-1, keepdims=True), acc)

        sc_ref[h, ci] = lax.fori_loop(0, c, column, jnp.zeros((c, c), F32))
        return carry

    lax.fori_loop(0, N_HEADS * n, per_chunk, 0)


def _gla_outputs(z_ref, q1_ref, ks_ref, bs_ref, sc_ref, sta_ref, stb_ref, nga_ref, ngb_ref, o_ref):
    c = GLA_CHUNK
    causal = lax.broadcasted_iota(jnp.int32, (c, c), 0) >= lax.broadcasted_iota(jnp.int32, (c, c), 1)
    for ci in range(GLA_TILE // c):
        rows = slice(ci * c, (ci + 1) * c)
        for h in range(N_HEADS):
            if h < A_HEADS:
                dv, st_ref, norm_g = A_DV, sta_ref.at[h], nga_ref[...]
                v_off, g_off, o_off = Z_IA + h * A_DV, Z_GA + h * A_DV, h * A_DV
            else:
                hb = h - A_HEADS
                dv, st_ref, norm_g = B_DV, stb_ref.at[hb], ngb_ref[...]
                v_off, g_off, o_off = Z_VB + hb * B_DV, Z_RB + hb * B_DV, A_V + hb * B_DV
            st = st_ref[...]
            k = ks_ref[h, rows, :]
            bc = bs_ref[h, rows, :]
            b_last = bc[c - 1:c]
            k_dec = (k * jnp.exp(b_last - bc)).astype(BF16)
            vb = z_ref[rows, v_off:v_off + dv].astype(BF16)
            scores = jnp.where(causal, sc_ref[h, ci], 0.0).astype(BF16)
            o = _dot_nt(q1_ref[h, rows, :], st.astype(BF16)) + _dot(scores, vb)
            st_ref[...] = st * jnp.exp(b_last) + _dot_tn(vb, k_dec)
            on = o * lax.rsqrt(jnp.mean(o * o, axis=-1, keepdims=True) + RMS_EPS) * norm_g
            gc = z_ref[rows, g_off:g_off + dv]
            o_ref[rows, o_off:o_off + dv] = (on * (gc * _sigmoid(gc))).astype(o_ref.dtype)


def _gla_kernel(z_ref, lr_ref, lb_ref, wup_ref, bup_ref, tri_ref, nga_ref, ngb_ref, o_ref,
                sta_ref, stb_ref, qs_ref, q1_ref, ks_ref, bs_ref, sc_ref):
    @pl.when(pl.program_id(1) == 0)
    def _():
        sta_ref[...] = jnp.zeros_like(sta_ref)
        stb_ref[...] = jnp.zeros_like(stb_ref)

    low = _gla_prep(z_ref, lr_ref, lb_ref, wup_ref, bup_ref, tri_ref, qs_ref, q1_ref, ks_ref, bs_ref)
    safe = jnp.min(low) >= GLA_SAFE_LOG_DECAY

    @pl.when(safe)
    def _product_form():
        _product_scores(q1_ref, ks_ref, bs_ref, sc_ref)

    @pl.when(jnp.logical_not(safe))
    def _exact_form():
        _exact_scores(qs_ref, ks_ref, bs_ref, sc_ref)

    _gla_outputs(z_ref, q1_ref, ks_ref, bs_ref, sc_ref, sta_ref, stb_ref, nga_ref, ngb_ref, o_ref)


def _gla(z, z_lr, lb, w_up, b_up, norm_a, norm_b, bsz, seq):
    ts = GLA_TILE
    nt = seq // ts
    n_chunks = ts // GLA_CHUNK
    row = lambda b, t: (b * nt + t, 0)
    const = lambda b, t: (0, 0)
    per_head = pltpu.VMEM((N_HEADS, ts, HEAD_DK), F32)
    return pl.pallas_call(
        _gla_kernel,
        grid=(bsz, nt),
        in_specs=[
            pl.BlockSpec((ts, Z_MIXER), row),
            pl.BlockSpec((ts, LANES), row),
            pl.BlockSpec((1, A_HEADS * A_DK), const),
            pl.BlockSpec((LANES, B_HEADS * B_DK), const),
            pl.BlockSpec((1, B_HEADS * B_DK), const),
            pl.BlockSpec((ts, ts), const),
            pl.BlockSpec((1, A_DV), const),
            pl.BlockSpec((1, B_DV), const),
        ],
        out_specs=pl.BlockSpec((ts, 2 * D_MODEL), row),
        out_shape=jax.ShapeDtypeStruct((bsz * seq, 2 * D_MODEL), BF16),
        scratch_shapes=[
            pltpu.VMEM((A_HEADS, A_DV, A_DK), F32),
            pltpu.VMEM((B_HEADS, B_DV, B_DK), F32),
            per_head, pltpu.VMEM((N_HEADS, ts, HEAD_DK), BF16), per_head, per_head,
            pltpu.VMEM((N_HEADS, n_chunks, GLA_CHUNK, GLA_CHUNK), F32),
        ],
        compiler_params=pltpu.CompilerParams(
            dimension_semantics=("parallel", "arbitrary"), vmem_limit_bytes=VMEM_LIMIT),
        name="gla",
    )(z, z_lr, lb, w_up, b_up, _chunk_tri(ts, GLA_CHUNK), norm_a, norm_b)


def _lane_pack(cols, dtype):
    m = cols[0].shape[0]
    lane = lax.broadcasted_iota(jnp.int32, (m, LANES), 1)
    out = jnp.zeros((m, LANES), dtype)
    for k, cvals in enumerate(cols):
        out = jnp.where(lane == k, cvals.astype(dtype), out)
    return out


def _post_mixer_kernel(o_ref, ga_ref, gb_ref, h0_ref, wa_ref, wb_ref, wo_ref, lg_ref, lb_ref,
                       wrh_ref, wrl_ref, br_ref, h1_ref, te_ref, tw_ref):
    ya = _dot(o_ref[:, :D_MODEL], wa_ref[...])
    yb = _dot(o_ref[:, D_MODEL:], wb_ref[...])
    merged = _sigmoid(ga_ref[...]) * ya + _sigmoid(gb_ref[...]) * yb
    y = _dot(merged.astype(BF16), wo_ref[...])
    h1 = _layer_norm(DN_ALPHA * h0_ref[...] + y, lg_ref[...], lb_ref[...])
    h1_ref[...] = h1

    hh = h1.astype(BF16)
    hl = (h1 - hh.astype(F32)).astype(BF16)
    logits = _dot(hh, wrh_ref[...]) + (_dot(hl, wrh_ref[...]) + _dot(hh, wrl_ref[...])) + br_ref[...]
    lane = lax.broadcasted_iota(jnp.int32, logits.shape, 1)
    work = jnp.where(lane < N_EXPERTS, logits, -jnp.inf)
    vals, idxs = [], []
    for _ in range(TOP_K):
        m = jnp.max(work, axis=-1, keepdims=True)
        idx = jnp.min(jnp.where(work == m, lane, LANES), axis=-1, keepdims=True)
        vals.append(m)
        idxs.append(idx)
        work = jnp.where(lane == idx, -jnp.inf, work)
    exps = [jnp.exp(v - vals[0]) for v in vals]
    denom = exps[0] + exps[1] + exps[2] + exps[3]
    te_ref[...] = _lane_pack(idxs, jnp.int32)
    tw_ref[...] = _lane_pack([e / denom for e in exps], F32)


def _post_mixer(o, z, h0, w_a, w_b, w_o, ln_g, ln_b, wr_hi, wr_lo, b_r, tm=512):
    t = h0.shape[0]
    row = lambda i: (i, 0)
    const = lambda i: (0, 0)
    full_w = pl.BlockSpec((D_MODEL, D_MODEL), const)
    vec = pl.BlockSpec((1, D_MODEL), const)
    return pl.pallas_call(
        _post_mixer_kernel,
        grid=(t // tm,),
        in_specs=[
            pl.BlockSpec((tm, 2 * D_MODEL), row),
            pl.BlockSpec((tm, D_MODEL), lambda i: (i, Z_GATE_A // D_MODEL)),
            pl.BlockSpec((tm, D_MODEL), lambda i: (i, Z_GATE_B // D_MODEL)),
            pl.BlockSpec((tm, D_MODEL), row),
            full_w, full_w, full_w, vec, vec,
            pl.BlockSpec((D_MODEL, LANES), const),
            pl.BlockSpec((D_MODEL, LANES), const),
            pl.BlockSpec((1, LANES), const),
        ],
        out_specs=[
            pl.BlockSpec((tm, D_MODEL), row),
            pl.BlockSpec((tm, LANES), row),
            pl.BlockSpec((tm, LANES), row),
        ],
        out_shape=[
            jax.ShapeDtypeStruct((t, D_MODEL), F32),
            jax.ShapeDtypeStruct((t, LANES), jnp.int32),
            jax.ShapeDtypeStruct((t, LANES), F32),
        ],
        compiler_params=pltpu.CompilerParams(
            dimension_semantics=("parallel",), vmem_limit_bytes=VMEM_LIMIT),
        name="post_mixer",
    )(o, z, z, h0, w_a, w_b, w_o, ln_g, ln_b, wr_hi, wr_lo, b_r)


def _rank_kernel(te_ref, rank_ref, cnt_ref, carry_ref):
    @pl.when(pl.program_id(0) == 0)
    def _():
        carry_ref[...] = jnp.zeros_like(carry_ref)

    te = te_ref[...]
    tr = te.shape[0]
    lane = lax.broadcasted_iota(jnp.int32, (tr, LANES), 1)
    onehots = [te[:, k:k + 1] == lane for k in range(TOP_K)]
    cnt = jnp.zeros((tr, LANES), F32)
    for oh in onehots:
        cnt = cnt + oh.astype(F32)
    strict_lower = (lax.broadcasted_iota(jnp.int32, (tr, tr), 0)
                    > lax.broadcasted_iota(jnp.int32, (tr, tr), 1))
    earlier = _dot(strict_lower.astype(BF16), cnt.astype(BF16)) + carry_ref[...]
    ranks = [jnp.sum(jnp.where(oh, earlier, 0.0), axis=-1, keepdims=True) for oh in onehots]
    rank_ref[...] = _lane_pack(ranks, jnp.int32)
    carry_ref[...] = carry_ref[...] + jnp.sum(cnt, axis=0, keepdims=True)
    cnt_ref[...] = carry_ref[...]


def _rank(top_e, tr=512):
    t = top_e.shape[0]
    return pl.pallas_call(
        _rank_kernel,
        grid=(t // tr,),
        in_specs=[pl.BlockSpec((tr, LANES), lambda i: (i, 0))],
        out_specs=[
            pl.BlockSpec((tr, LANES), lambda i: (i, 0)),
            pl.BlockSpec((1, LANES), lambda i: (0, 0)),
        ],
        out_shape=[
            jax.ShapeDtypeStruct((t, LANES), jnp.int32),
            jax.ShapeDtypeStruct((1, LANES), F32),
        ],
        scratch_shapes=[pltpu.VMEM((1, LANES), F32)],
        compiler_params=pltpu.CompilerParams(dimension_semantics=("arbitrary",)),
        name="rank",
    )(top_e)


def _row(ref, r):
    return ref.at[pl.ds(r, 1), :]


def _wait_rows(ref_hbm, n, sem):
    pltpu.make_async_copy(ref_hbm.at[pl.ds(0, n), :], ref_hbm.at[pl.ds(0, n), :], sem).wait()


def _dispatch_kernel(pad_start_ref, pad_cnt_ref, nu_ref, dest_ref, h_ref, p_ref, wpg_ref, wpp_ref,
                     xs_hbm, res_ref, zero_ref, sem):
    n_tok = h_ref.shape[0]

    @pl.when(pl.program_id(0) == 0)
    def _zero_rows_without_token():
        zero_ref[...] = jnp.zeros_like(zero_ref)

        def wait_pad_groups(n):
            @pl.when(n > 0)
            def _():
                _wait_rows(xs_hbm, pl.multiple_of(n * PAD_GROUP, PAD_GROUP), sem)

        def per_expert(e, n_prev):
            def start(g, c):
                r0 = pl.multiple_of(pad_start_ref[e] + g * PAD_GROUP, PAD_GROUP)
                pltpu.make_async_copy(zero_ref.at[pl.ds(0, PAD_GROUP), :],
                                      xs_hbm.at[pl.ds(r0, PAD_GROUP), :], sem).start()
                return c

            lax.fori_loop(0, pad_cnt_ref[e], start, 0)
            wait_pad_groups(n_prev)
            return pad_cnt_ref[e]

        wait_pad_groups(lax.fori_loop(0, N_EXPERTS, per_expert, 0))

        def block_copy(blk):
            r0 = pl.multiple_of(blk * MOE_ROWS, MOE_ROWS)
            return pltpu.make_async_copy(zero_ref, xs_hbm.at[pl.ds(r0, MOE_ROWS), :], sem)

        def start_block(blk, c):
            block_copy(blk).start()
            return c

        def wait_block(blk, c):
            block_copy(blk).wait()
            return c

        n_blocks = xs_hbm.shape[0] // MOE_ROWS
        lax.fori_loop(nu_ref[0], n_blocks, start_block, 0)
        lax.fori_loop(nu_ref[0], n_blocks, wait_block, 0)

    for r in range(n_tok):
        for k in range(TOP_K):
            pltpu.make_async_copy(_row(h_ref, r), _row(xs_hbm, dest_ref[r * TOP_K + k]), sem).start(
                priority=k % DMA_QUEUES)
    h1 = h_ref[...]
    ple = _sigmoid(_dot(h1.astype(BF16), wpg_ref[...])) * _dot(p_ref[...].astype(BF16), wpp_ref[...])
    res_ref[...] = DN_ALPHA * h1 + ple
    for _ in range(n_tok * TOP_K // DMA_BATCH):
        _wait_rows(xs_hbm, DMA_BATCH, sem)


def _dispatch(pad_start, pad_cnt, n_used, dest_flat, h1, p, w_pg, w_pp, n_rows, tok_per_step=256):
    t = h1.shape[0]
    row = lambda i, ps, pc, nu: (i, 0)
    const = lambda i, ps, pc, nu: (0, 0)
    grid_spec = pltpu.PrefetchScalarGridSpec(
        num_scalar_prefetch=3,
        grid=(t // tok_per_step,),
        in_specs=[
            pl.BlockSpec((tok_per_step * TOP_K,), lambda i, ps, pc, nu: (i,), memory_space=pltpu.SMEM),
            pl.BlockSpec((tok_per_step, D_MODEL), row),
            pl.BlockSpec((tok_per_step, PLE_DIM), row),
            pl.BlockSpec((D_MODEL, D_MODEL), const),
            pl.BlockSpec((PLE_DIM, D_MODEL), const),
        ],
        out_specs=[
            pl.BlockSpec(memory_space=pl.ANY),
            pl.BlockSpec((tok_per_step, D_MODEL), row),
        ],
        scratch_shapes=[pltpu.VMEM((MOE_ROWS, D_MODEL), F32), pltpu.SemaphoreType.DMA(())],
    )
    return pl.pallas_call(
        _dispatch_kernel,
        grid_spec=grid_spec,
        out_shape=[
            jax.ShapeDtypeStruct((n_rows, D_MODEL), F32),
            jax.ShapeDtypeStruct((t, D_MODEL), F32),
        ],
        compiler_params=pltpu.CompilerParams(
            dimension_semantics=("arbitrary",), vmem_limit_bytes=VMEM_LIMIT),
        name="dispatch",
    )(pad_start, pad_cnt, n_used, dest_flat, h1, p, w_pg, w_pp)


def _experts_kernel(be_ref, nu_ref, x_ref, wgu_ref, bgu_ref, wd_ref, bd_ref, y_ref, wgu_bf_ref, wd_bf_ref):
    i = pl.program_id(0)
    used = i < nu_ref[0]

    @pl.when((i == 0) | (be_ref[i] != be_ref[jnp.maximum(i - 1, 0)]))
    def _():
        wgu_bf_ref[...] = wgu_ref[0].astype(BF16)
        wd_bf_ref[...] = wd_ref[0].astype(BF16)

    @pl.when(used)
    def _():
        hgu = _dot(x_ref[...].astype(BF16), wgu_bf_ref[...]) + bgu_ref[0]
        gate = jnp.minimum(hgu[:, :D_FF], SWIGLU_LIMIT)
        up = jnp.clip(hgu[:, D_FF:], -SWIGLU_LIMIT, SWIGLU_LIMIT)
        act = (up + 1.0) * gate * _sigmoid(SWIGLU_ALPHA * gate)
        y_ref[...] = _dot(act.astype(BF16), wd_bf_ref[...]) + bd_ref[0]

    @pl.when(jnp.logical_not(used))
    def _():
        y_ref[...] = jnp.zeros_like(y_ref)


def _experts(block_e, n_used, xs, w_gu, b_gu, w_d, b_d):
    n_rows = xs.shape[0]
    n_blocks = n_rows // MOE_ROWS
    grid_spec = pltpu.PrefetchScalarGridSpec(
        num_scalar_prefetch=2,
        grid=(n_blocks,),
        in_specs=[
            pl.BlockSpec((MOE_ROWS, D_MODEL), lambda i, be, nu: (jnp.minimum(i, nu[0] - 1), 0)),
            pl.BlockSpec((1, D_MODEL, 2 * D_FF), lambda i, be, nu: (be[i], 0, 0)),
            pl.BlockSpec((1, 1, 2 * D_FF), lambda i, be, nu: (be[i], 0, 0)),
            pl.BlockSpec((1, D_FF, D_MODEL), lambda i, be, nu: (be[i], 0, 0)),
            pl.BlockSpec((1, 1, D_MODEL), lambda i, be, nu: (be[i], 0, 0)),
        ],
        out_specs=pl.BlockSpec((MOE_ROWS, D_MODEL), lambda i, be, nu: (i, 0)),
        scratch_shapes=[pltpu.VMEM((D_MODEL, 2 * D_FF), BF16), pltpu.VMEM((D_FF, D_MODEL), BF16)],
    )
    return pl.pallas_call(
        _experts_kernel,
        grid_spec=grid_spec,
        out_shape=jax.ShapeDtypeStruct((n_rows, D_MODEL), F32),
        compiler_params=pltpu.CompilerParams(
            dimension_semantics=("arbitrary",), vmem_limit_bytes=VMEM_LIMIT),
        name="experts",
    )(block_e, n_used, xs, w_gu, b_gu, w_d, b_d)


def _final_kernel(dest_ref, dest_next_ref, ys_hbm, tw_ref, res_ref, lg_ref, lb_ref, o_ref, g_ref, sem):
    tm = res_ref.shape[0]
    i = pl.program_id(0)
    slot = i % 2

    def gather(idx_ref, s):
        for r in range(tm):
            for k in range(TOP_K):
                pltpu.make_async_copy(_row(ys_hbm, idx_ref[r * TOP_K + k]), _row(g_ref.at[s, k], r),
                                      sem.at[s]).start(priority=k % DMA_QUEUES)

    @pl.when(i == 0)
    def _():
        gather(dest_ref, 0)

    @pl.when(i + 1 < pl.num_programs(0))
    def _():
        gather(dest_next_ref, 1 - slot)

    for _ in range(tm * TOP_K // DMA_BATCH):
        _wait_rows(ys_hbm, DMA_BATCH, sem.at[slot])
    tw = tw_ref[...]
    acc = res_ref[...]
    for k in range(TOP_K):
        acc = acc + tw[:, k:k + 1] * g_ref[slot, k]
    o_ref[...] = _layer_norm(acc, lg_ref[...], lb_ref[...])


def _final(dest_flat, ys, top_w, res, ln_g, ln_b, tm=256):
    t = res.shape[0]
    n_steps = t // tm
    row = lambda i: (i, 0)
    const = lambda i: (0, 0)
    return pl.pallas_call(
        _final_kernel,
        grid=(n_steps,),
        in_specs=[
            pl.BlockSpec((tm * TOP_K,), lambda i: (i,), memory_space=pltpu.SMEM),
            pl.BlockSpec((tm * TOP_K,), lambda i: (jnp.minimum(i + 1, n_steps - 1),),
                         memory_space=pltpu.SMEM),
            pl.BlockSpec(memory_space=pl.ANY),
            pl.BlockSpec((tm, LANES), row),
            pl.BlockSpec((tm, D_MODEL), row),
            pl.BlockSpec((1, D_MODEL), const),
            pl.BlockSpec((1, D_MODEL), const),
        ],
        out_specs=pl.BlockSpec((tm, D_MODEL), row),
        out_shape=jax.ShapeDtypeStruct((t, D_MODEL), F32),
        scratch_shapes=[pltpu.VMEM((2, TOP_K, tm, D_MODEL), F32), pltpu.SemaphoreType.DMA((2,))],
        compiler_params=pltpu.CompilerParams(
            dimension_semantics=("arbitrary",), vmem_limit_bytes=VMEM_LIMIT),
        name="final",
    )(dest_flat, dest_flat, ys, top_w, res, ln_g, ln_b)


def kernel(x, p, emb_ln_g, emb_ln_b, hgrn_lb, w_in, w_gla_up, b_gla_up, norm_a_g, norm_b_g, w_proj_a, w_proj_b, w_out, ln_mix_g, ln_mix_b, w_router, b_router, w_gate_up, b_gate_up, w_down, b_down, w_ple_gate, w_ple_proj, ln_moe_g, ln_moe_b):
    bsz, seq, d = x.shape
    t = bsz * seq
    assert d == D_MODEL and seq % GLA_TILE == 0 and t % 1024 == 0
    assert w_in.shape[0] == DEPTH == 1
    vec = lambda a: a.reshape(1, -1).astype(F32)

    lb0 = jax.nn.softmax(hgrn_lb.astype(F32), axis=0)[0:1]
    w_in0 = w_in[0]
    w_main = jnp.concatenate([w_in0[:, :W_IN_LR], w_in0[:, W_IN_LR + GLA_RANK:]], axis=1).astype(BF16)
    w_lr = jnp.pad(w_in0[:, W_IN_LR:W_IN_LR + GLA_RANK], ((0, 0), (0, LANES - GLA_RANK))).astype(BF16)
    w_up = jnp.pad(w_gla_up[0], ((0, LANES - GLA_RANK), (0, 0))).astype(BF16)
    wr = jnp.pad(w_router[0].astype(F32), ((0, 0), (0, LANES - N_EXPERTS)))
    wr_hi = wr.astype(BF16)
    wr_lo = (wr - wr_hi.astype(F32)).astype(BF16)
    b_r = jnp.pad(b_router[0].astype(F32), (0, LANES - N_EXPERTS)).reshape(1, LANES)

    xt = x.reshape(t, d)
    h0, z, z_lr = _ln_inproj(xt, vec(emb_ln_g), vec(emb_ln_b), w_main, w_lr)
    o = _gla(z, z_lr, lb0, w_up, vec(b_gla_up[0]), vec(norm_a_g[0]), vec(norm_b_g[0]), bsz, seq)
    h1, top_e, top_w = _post_mixer(
        o, z, h0, w_proj_a[0].astype(BF16), w_proj_b[0].astype(BF16), w_out[0].astype(BF16),
        vec(ln_mix_g[0]), vec(ln_mix_b[0]), wr_hi, wr_lo, b_r)

    rank, cnt = _rank(top_e)
    counts = cnt[0, :N_EXPERTS].astype(jnp.int32)
    padded = (counts + MOE_ROWS - 1) // MOE_ROWS * MOE_ROWS
    pend = jnp.cumsum(padded)
    pstart = pend - padded
    n_blocks = t * TOP_K // MOE_ROWS + N_EXPERTS
    n_rows = n_blocks * MOE_ROWS
    n_used = pend[N_EXPERTS - 1:] // MOE_ROWS
    dest = (pstart[top_e[:, :TOP_K]] + rank[:, :TOP_K]).reshape(-1)
    block_pos = jnp.minimum(jnp.arange(n_blocks, dtype=jnp.int32), n_used - 1) * MOE_ROWS
    block_e = jnp.minimum(
        jnp.sum((pend[None, :] <= block_pos[:, None]).astype(jnp.int32), axis=1), N_EXPERTS - 1)

    pad_start = (pstart + counts) // PAD_GROUP * PAD_GROUP
    xs, res = _dispatch(pad_start, (pend - pad_start) // PAD_GROUP, n_used, dest, h1,
                        p[0].reshape(t, PLE_DIM), w_ple_gate[0].astype(BF16), w_ple_proj[0].astype(BF16),
                        n_rows)
    ys = _experts(block_e, n_used, xs, w_gate_up[0].astype(F32),
                  b_gate_up[0].reshape(N_EXPERTS, 1, 2 * D_FF).astype(F32),
                  w_down[0].astype(F32), b_down[0].reshape(N_EXPERTS, 1, d).astype(F32))
    out = _final(dest, ys, top_w, res, vec(ln_moe_g[0]), vec(ln_moe_b[0]))
    return out.reshape(bsz, seq, d)
```

```python
import jax
import jax.numpy as jnp
from jax import lax
from jax.experimental import pallas as pl
from jax.experimental.pallas import tpu as pltpu

F32 = jnp.float32
BF16 = jnp.bfloat16

D_MODEL = 1024
A_HEADS, A_DK, A_DV = 8, 128, 128
B_HEADS, B_DK, B_DV = 4, 128, 256
N_HEADS = A_HEADS + B_HEADS
HEAD_DK = 128
A_V = A_HEADS * A_DV
GLA_RANK = 16
GLA_TAU = 16.0
N_EXPERTS = 32
TOP_K = 4
D_FF = 1024
SWIGLU_LIMIT = 7.0
SWIGLU_ALPHA = 1.702
PLE_DIM = 256
LN_EPS = 1e-5
RMS_EPS = 1e-6
DEPTH = 1
DN_ALPHA = (2.0 * DEPTH) ** 0.25

LANES = 128
GLA_CHUNK = 64
GLA_TILE = 256
GLA_SAFE_LOG_DECAY = -60.0
MOE_ROWS = 512
DMA_BATCH = 128
DMA_QUEUES = 2
F32_SUBLANES = 8
PAD_GROUP = F32_SUBLANES
VMEM_LIMIT = 56 * 1024 * 1024

Z_QA, Z_FA, Z_IA, Z_GA = 0, 1024, 2048, 3072
Z_QB, Z_KB, Z_VB, Z_RB = 4096, 4608, 5120, 6144
Z_MIXER = 7168
Z_GATE_A, Z_GATE_B = 7168, 8192
Z_COLS = 9216
W_IN_LR = 7168


def _layer_norm(x, g, b):
    mu = jnp.mean(x, axis=-1, keepdims=True)
    xc = x - mu
    var = jnp.mean(xc * xc, axis=-1, keepdims=True)
    return xc * lax.rsqrt(var + LN_EPS) * g + b


def _sigmoid(x):
    return jax.nn.sigmoid(x)


def _sigmoid_pair(x):
    e = jnp.exp(-jnp.abs(x))
    r = 1.0 / (1.0 + e)
    er = e * r
    pos = x >= 0.0
    return jnp.where(pos, r, er), jnp.where(pos, er, r)


def _dot(a, b):
    return jnp.dot(a, b, preferred_element_type=F32)


def _dot_nt(a, b):
    return lax.dot_general(a, b, (((1,), (1,)), ((), ())), preferred_element_type=F32)


def _dot_tn(a, b):
    return lax.dot_general(a, b, (((0,), (0,)), ((), ())), preferred_element_type=F32)


def _ln_inproj_kernel(x_ref, g_ref, b_ref, w_ref, wlr_ref, h_ref, z_ref, zlr_ref, hb_ref):
    @pl.when(pl.program_id(1) == 0)
    def _():
        h = _layer_norm(x_ref[...], g_ref[...], b_ref[...])
        h_ref[...] = h
        hb = h.astype(BF16)
        hb_ref[...] = hb
        zlr_ref[...] = _dot(hb, wlr_ref[...])

    z_ref[...] = _dot(hb_ref[...], w_ref[...])


def _ln_inproj(x, g, b, w_main, w_lr, tm=1024, tn=2304):
    t = x.shape[0]
    return pl.pallas_call(
        _ln_inproj_kernel,
        grid=(t // tm, Z_COLS // tn),
        in_specs=[
            pl.BlockSpec((tm, D_MODEL), lambda i, j: (i, 0)),
            pl.BlockSpec((1, D_MODEL), lambda i, j: (0, 0)),
            pl.BlockSpec((1, D_MODEL), lambda i, j: (0, 0)),
            pl.BlockSpec((D_MODEL, tn), lambda i, j: (0, j)),
            pl.BlockSpec((D_MODEL, LANES), lambda i, j: (0, 0)),
        ],
        out_specs=[
            pl.BlockSpec((tm, D_MODEL), lambda i, j: (i, 0)),
            pl.BlockSpec((tm, tn), lambda i, j: (i, j)),
            pl.BlockSpec((tm, LANES), lambda i, j: (i, 0)),
        ],
        out_shape=[
            jax.ShapeDtypeStruct((t, D_MODEL), F32),
            jax.ShapeDtypeStruct((t, Z_COLS), F32),
            jax.ShapeDtypeStruct((t, LANES), F32),
        ],
        scratch_shapes=[pltpu.VMEM((tm, D_MODEL), BF16)],
        compiler_params=pltpu.CompilerParams(
            dimension_semantics=("parallel", "arbitrary"), vmem_limit_bytes=VMEM_LIMIT),
        name="ln_inproj",
    )(x, g, b, w_main, w_lr)


def _chunk_tri(tile, chunk):
    i = jnp.arange(tile)[:, None]
    j = jnp.arange(tile)[None, :]
    return ((j <= i) & (i // chunk == j // chunk)).astype(BF16)


def _chunk_cumsum(tri, x):
    hi = x.astype(BF16)
    lo = (x - hi.astype(F32)).astype(BF16)
    return _dot(tri, hi) + _dot(tri, lo)


def _gla_prep(z_ref, lr_ref, lb_ref, wup_ref, bup_ref, tri_ref, qs_ref, q1_ref, ks_ref, bs_ref):
    low = None
    tri = tri_ref[...]

    def put(h, q, k, b):
        qs_ref[h] = q
        q1_ref[h] = (q * jnp.exp(b)).astype(BF16)
        ks_ref[h] = k
        bs_ref[h] = b

    for h in range(A_HEADS):
        cols = slice(h * A_DK, (h + 1) * A_DK)
        lb = lb_ref[:, cols]
        sig, sig_neg = _sigmoid_pair(z_ref[:, Z_FA + h * A_DK:Z_FA + (h + 1) * A_DK])
        b = _chunk_cumsum(tri, jnp.log(lb + (1.0 - lb) * sig))
        qz = z_ref[:, Z_QA + h * A_DK:Z_QA + (h + 1) * A_DK]
        put(h, qz * _sigmoid(qz) * (A_DK ** -0.5), (1.0 - lb) * sig_neg, b)
        low = b if low is None else jnp.minimum(low, b)
    u_all = _dot(lr_ref[...].astype(BF16), wup_ref[...]) + bup_ref[...]
    for hb in range(B_HEADS):
        u = u_all[:, hb * B_DK:(hb + 1) * B_DK]
        log_sig = jnp.minimum(u, 0.0) - jnp.log(1.0 + jnp.exp(-jnp.abs(u)))
        b = _chunk_cumsum(tri, log_sig / GLA_TAU)
        put(A_HEADS + hb, z_ref[:, Z_QB + hb * B_DK:Z_QB + (hb + 1) * B_DK] * (B_DK ** -0.5),
            z_ref[:, Z_KB + hb * B_DK:Z_KB + (hb + 1) * B_DK], b)
        low = jnp.minimum(low, b)
    return low


def _product_scores(q1_ref, ks_ref, bs_ref, sc_ref):
    c = GLA_CHUNK
    for h in range(N_HEADS):
        k1 = (ks_ref[h] * jnp.exp(-bs_ref[h])).astype(BF16)
        for ci in range(GLA_TILE // c):
            rows = slice(ci * c, (ci + 1) * c)
            sc_ref[h, ci] = _dot_nt(q1_ref[h, rows, :], k1[rows])


def _exact_scores(qs_ref, ks_ref, bs_ref, sc_ref):
    c = GLA_CHUNK
    n = GLA_TILE // c
    col = lax.broadcasted_iota(jnp.int32, (c, c), 1)

    def per_chunk(idx, carry):
        h = idx // n
        ci = idx - h * n
        r0 = pl.multiple_of(ci * c, c)
        qc = qs_ref[h, pl.ds(r0, c), :]
        bc = bs_ref[h, pl.ds(r0, c), :]

        def column(j, acc):
            kj = ks_ref[h, pl.ds(r0 + j, 1), :]
            bj = bs_ref[h, pl.ds(r0 + j, 1), :]
            term = qc * kj * jnp.exp(jnp.minimum(bc - bj, 0.0))
            return jnp.where(col == j, jnp.sum(term, axis=-1, keepdims=True), acc)

        sc_ref[h, ci] = lax.fori_loop(0, c, column, jnp.zeros((c, c), F32))
        return carry

    lax.fori_loop(0, N_HEADS * n, per_chunk, 0)


def _gla_outputs(z_ref, q1_ref, ks_ref, bs_ref, sc_ref, sta_ref, stb_ref, nga_ref, ngb_ref, o_ref):
    c = GLA_CHUNK
    causal = lax.broadcasted_iota(jnp.int32, (c, c), 0) >= lax.broadcasted_iota(jnp.int32, (c, c), 1)
    for ci in range(GLA_TILE // c):
        rows = slice(ci * c, (ci + 1) * c)
        for h in range(N_HEADS):
            if h < A_HEADS:
                dv, st_ref, norm_g = A_DV, sta_ref.at[h], nga_ref[...]
                v_off, g_off, o_off = Z_IA + h * A_DV, Z_GA + h * A_DV, h * A_DV
            else:
                hb = h - A_HEADS
                dv, st_ref, norm_g = B_DV, stb_ref.at[hb], ngb_ref[...]
                v_off, g_off, o_off = Z_VB + hb * B_DV, Z_RB + hb * B_DV, A_V + hb * B_DV
            st = st_ref[...]
            k = ks_ref[h, rows, :]
            bc = bs_ref[h, rows, :]
            b_last = bc[c - 1:c]
            k_dec = (k * jnp.exp(b_last - bc)).astype(BF16)
            vb = z_ref[rows, v_off:v_off + dv].astype(BF16)
            scores = jnp.where(causal, sc_ref[h, ci], 0.0).astype(BF16)
            o = _dot_nt(q1_ref[h, rows, :], st.astype(BF16)) + _dot(scores, vb)
            st_ref[...] = st * jnp.exp(b_last) + _dot_tn(vb, k_dec)
            on = o * lax.rsqrt(jnp.mean(o * o, axis=-1, keepdims=True) + RMS_EPS) * norm_g
            gc = z_ref[rows, g_off:g_off + dv]
            o_ref[rows, o_off:o_off + dv] = (on * (gc * _sigmoid(gc))).astype(o_ref.dtype)


def _gla_kernel(z_ref, lr_ref, lb_ref, wup_ref, bup_ref, tri_ref, nga_ref, ngb_ref, o_ref,
                sta_ref, stb_ref, qs_ref, q1_ref, ks_ref, bs_ref, sc_ref):
    @pl.when(pl.program_id(1) == 0)
    def _():
        sta_ref[...] = jnp.zeros_like(sta_ref)
        stb_ref[...] = jnp.zeros_like(stb_ref)

    low = _gla_prep(z_ref, lr_ref, lb_ref, wup_ref, bup_ref, tri_ref, qs_ref, q1_ref, ks_ref, bs_ref)
    safe = jnp.min(low) >= GLA_SAFE_LOG_DECAY

    @pl.when(safe)
    def _product_form():
        _product_scores(q1_ref, ks_ref, bs_ref, sc_ref)

    @pl.when(jnp.logical_not(safe))
    def _exact_form():
        _exact_scores(qs_ref, ks_ref, bs_ref, sc_ref)

    _gla_outputs(z_ref, q1_ref, ks_ref, bs_ref, sc_ref, sta_ref, stb_ref, nga_ref, ngb_ref, o_ref)


def _gla(z, z_lr, lb, w_up, b_up, norm_a, norm_b, bsz, seq):
    ts = GLA_TILE
    nt = seq // ts
    n_chunks = ts // GLA_CHUNK
    row = lambda b, t: (b * nt + t, 0)
    const = lambda b, t: (0, 0)
    per_head = pltpu.VMEM((N_HEADS, ts, HEAD_DK), F32)
    return pl.pallas_call(
        _gla_kernel,
        grid=(bsz, nt),
        in_specs=[
            pl.BlockSpec((ts, Z_MIXER), row),
            pl.BlockSpec((ts, LANES), row),
            pl.BlockSpec((1, A_HEADS * A_DK), const),
            pl.BlockSpec((LANES, B_HEADS * B_DK), const),
            pl.BlockSpec((1, B_HEADS * B_DK), const),
            pl.BlockSpec((ts, ts), const),
            pl.BlockSpec((1, A_DV), const),
            pl.BlockSpec((1, B_DV), const),
        ],
        out_specs=pl.BlockSpec((ts, 2 * D_MODEL), row),
        out_shape=jax.ShapeDtypeStruct((bsz * seq, 2 * D_MODEL), BF16),
        scratch_shapes=[
            pltpu.VMEM((A_HEADS, A_DV, A_DK), F32),
            pltpu.VMEM((B_HEADS, B_DV, B_DK), F32),
            per_head, pltpu.VMEM((N_HEADS, ts, HEAD_DK), BF16), per_head, per_head,
            pltpu.VMEM((N_HEADS, n_chunks, GLA_CHUNK, GLA_CHUNK), F32),
        ],
        compiler_params=pltpu.CompilerParams(
            dimension_semantics=("parallel", "arbitrary"), vmem_limit_bytes=VMEM_LIMIT),
        name="gla",
    )(z, z_lr, lb, w_up, b_up, _chunk_tri(ts, GLA_CHUNK), norm_a, norm_b)


def _lane_pack(cols, dtype):
    m = cols[0].shape[0]
    lane = lax.broadcasted_iota(jnp.int32, (m, LANES), 1)
    out = jnp.zeros((m, LANES), dtype)
    for k, cvals in enumerate(cols):
        out = jnp.where(lane == k, cvals.astype(dtype), out)
    return out


def _post_mixer_kernel(o_ref, ga_ref, gb_ref, h0_ref, wa_ref, wb_ref, wo_ref, lg_ref, lb_ref,
                       wrh_ref, wrl_ref, br_ref, h1_ref, te_ref, tw_ref):
    ya = _dot(o_ref[:, :D_MODEL], wa_ref[...])
    yb = _dot(o_ref[:, D_MODEL:], wb_ref[...])
    merged = _sigmoid(ga_ref[...]) * ya + _sigmoid(gb_ref[...]) * yb
    y = _dot(merged.astype(BF16), wo_ref[...])
    h1 = _layer_norm(DN_ALPHA * h0_ref[...] + y, lg_ref[...], lb_ref[...])
    h1_ref[...] = h1

    hh = h1.astype(BF16)
    hl = (h1 - hh.astype(F32)).astype(BF16)
    logits = _dot(hh, wrh_ref[...]) + (_dot(hl, wrh_ref[...]) + _dot(hh, wrl_ref[...])) + br_ref[...]
    lane = lax.broadcasted_iota(jnp.int32, logits.shape, 1)
    work = jnp.where(lane < N_EXPERTS, logits, -jnp.inf)
    vals, idxs = [], []
    for _ in range(TOP_K):
        m = jnp.max(work, axis=-1, keepdims=True)
        idx = jnp.min(jnp.where(work == m, lane, LANES), axis=-1, keepdims=True)
        vals.append(m)
        idxs.append(idx)
        work = jnp.where(lane == idx, -jnp.inf, work)
    exps = [jnp.exp(v - vals[0]) for v in vals]
    denom = exps[0] + exps[1] + exps[2] + exps[3]
    te_ref[...] = _lane_pack(idxs, jnp.int32)
    tw_ref[...] = _lane_pack([e / denom for e in exps], F32)


def _post_mixer(o, z, h0, w_a, w_b, w_o, ln_g, ln_b, wr_hi, wr_lo, b_r, tm=512):
    t = h0.shape[0]
    row = lambda i: (i, 0)
    const = lambda i: (0, 0)
    full_w = pl.BlockSpec((D_MODEL, D_MODEL), const)
    vec = pl.BlockSpec((1, D_MODEL), const)
    return pl.pallas_call(
        _post_mixer_kernel,
        grid=(t // tm,),
        in_specs=[
            pl.BlockSpec((tm, 2 * D_MODEL), row),
            pl.BlockSpec((tm, D_MODEL), lambda i: (i, Z_GATE_A // D_MODEL)),
            pl.BlockSpec((tm, D_MODEL), lambda i: (i, Z_GATE_B // D_MODEL)),
            pl.BlockSpec((tm, D_MODEL), row),
            full_w, full_w, full_w, vec, vec,
            pl.BlockSpec((D_MODEL, LANES), const),
            pl.BlockSpec((D_MODEL, LANES), const),
            pl.BlockSpec((1, LANES), const),
        ],
        out_specs=[
            pl.BlockSpec((tm, D_MODEL), row),
            pl.BlockSpec((tm, LANES), row),
            pl.BlockSpec((tm, LANES), row),
        ],
        out_shape=[
            jax.ShapeDtypeStruct((t, D_MODEL), F32),
            jax.ShapeDtypeStruct((t, LANES), jnp.int32),
            jax.ShapeDtypeStruct((t, LANES), F32),
        ],
        compiler_params=pltpu.CompilerParams(
            dimension_semantics=("parallel",), vmem_limit_bytes=VMEM_LIMIT),
        name="post_mixer",
    )(o, z, z, h0, w_a, w_b, w_o, ln_g, ln_b, wr_hi, wr_lo, b_r)


def _rank_kernel(te_ref, rank_ref, cnt_ref, carry_ref):
    @pl.when(pl.program_id(0) == 0)
    def _():
        carry_ref[...] = jnp.zeros_like(carry_ref)

    te = te_ref[...]
    tr = te.shape[0]
    lane = lax.broadcasted_iota(jnp.int32, (tr, LANES), 1)
    onehots = [te[:, k:k + 1] == lane for k in range(TOP_K)]
    cnt = jnp.zeros((tr, LANES), F32)
    for oh in onehots:
        cnt = cnt + oh.astype(F32)
    strict_lower = (lax.broadcasted_iota(jnp.int32, (tr, tr), 0)
                    > lax.broadcasted_iota(jnp.int32, (tr, tr), 1))
    earlier = _dot(strict_lower.astype(BF16), cnt.astype(BF16)) + carry_ref[...]
    ranks = [jnp.sum(jnp.where(oh, earlier, 0.0), axis=-1, keepdims=True) for oh in onehots]
    rank_ref[...] = _lane_pack(ranks, jnp.int32)
    carry_ref[...] = carry_ref[...] + jnp.sum(cnt, axis=0, keepdims=True)
    cnt_ref[...] = carry_ref[...]


def _rank(top_e, tr=512):
    t = top_e.shape[0]
    return pl.pallas_call(
        _rank_kernel,
        grid=(t // tr,),
        in_specs=[pl.BlockSpec((tr, LANES), lambda i: (i, 0))],
        out_specs=[
            pl.BlockSpec((tr, LANES), lambda i: (i, 0)),
            pl.BlockSpec((1, LANES), lambda i: (0, 0)),
        ],
        out_shape=[
            jax.ShapeDtypeStruct((t, LANES), jnp.int32),
            jax.ShapeDtypeStruct((1, LANES), F32),
        ],
        scratch_shapes=[pltpu.VMEM((1, LANES), F32)],
        compiler_params=pltpu.CompilerParams(dimension_semantics=("arbitrary",)),
        name="rank",
    )(top_e)


def _row(ref, r):
    return ref.at[pl.ds(r, 1), :]


def _wait_rows(ref_hbm, n, sem):
    pltpu.make_async_copy(ref_hbm.at[pl.ds(0, n), :], ref_hbm.at[pl.ds(0, n), :], sem).wait()


def _dispatch_kernel(pad_start_ref, pad_cnt_ref, nu_ref, dest_ref, h_ref, p_ref, wpg_ref, wpp_ref,
                     xs_hbm, res_ref, zero_ref, sem):
    n_tok = h_ref.shape[0]

    @pl.when(pl.program_id(0) == 0)
    def _zero_rows_without_token():
        zero_ref[...] = jnp.zeros_like(zero_ref)

        def wait_pad_groups(n):
            @pl.when(n > 0)
            def _():
                _wait_rows(xs_hbm, pl.multiple_of(n * PAD_GROUP, PAD_GROUP), sem)

        def per_expert(e, n_prev):
            def start(g, c):
                r0 = pl.multiple_of(pad_start_ref[e] + g * PAD_GROUP, PAD_GROUP)
                pltpu.make_async_copy(zero_ref.at[pl.ds(0, PAD_GROUP), :],
                                      xs_hbm.at[pl.ds(r0, PAD_GROUP), :], sem).start()
                return c

            lax.fori_loop(0, pad_cnt_ref[e], start, 0)
            wait_pad_groups(n_prev)
            return pad_cnt_ref[e]

        wait_pad_groups(lax.fori_loop(0, N_EXPERTS, per_expert, 0))

        def block_copy(blk):
            r0 = pl.multiple_of(blk * MOE_ROWS, MOE_ROWS)
            return pltpu.make_async_copy(zero_ref, xs_hbm.at[pl.ds(r0, MOE_ROWS), :], sem)

        def start_block(blk, c):
            block_copy(blk).start()
            return c

        def wait_block(blk, c):
            block_copy(blk).wait()
            return c

        n_blocks = xs_hbm.shape[0] // MOE_ROWS
        lax.fori_loop(nu_ref[0], n_blocks, start_block, 0)
        lax.fori_loop(nu_ref[0], n_blocks, wait_block, 0)

    for r in range(n_tok):
        for k in range(TOP_K):
            pltpu.make_async_copy(_row(h_ref, r), _row(xs_hbm, dest_ref[r * TOP_K + k]), sem).start(
                priority=k % DMA_QUEUES)
    h1 = h_ref[...]
    ple = _sigmoid(_dot(h1.astype(BF16), wpg_ref[...])) * _dot(p_ref[...].astype(BF16), wpp_ref[...])
    res_ref[...] = DN_ALPHA * h1 + ple
    for _ in range(n_tok * TOP_K // DMA_BATCH):
        _wait_rows(xs_hbm, DMA_BATCH, sem)


def _dispatch(pad_start, pad_cnt, n_used, dest_flat, h1, p, w_pg, w_pp, n_rows, tok_per_step=256):
    t = h1.shape[0]
    row = lambda i, ps, pc, nu: (i, 0)
    const = lambda i, ps, pc, nu: (0, 0)
    grid_spec = pltpu.PrefetchScalarGridSpec(
        num_scalar_prefetch=3,
        grid=(t // tok_per_step,),
        in_specs=[
            pl.BlockSpec((tok_per_step * TOP_K,), lambda i, ps, pc, nu: (i,), memory_space=pltpu.SMEM),
            pl.BlockSpec((tok_per_step, D_MODEL), row),
            pl.BlockSpec((tok_per_step, PLE_DIM), row),
            pl.BlockSpec((D_MODEL, D_MODEL), const),
            pl.BlockSpec((PLE_DIM, D_MODEL), const),
        ],
        out_specs=[
            pl.BlockSpec(memory_space=pl.ANY),
            pl.BlockSpec((tok_per_step, D_MODEL), row),
        ],
        scratch_shapes=[pltpu.VMEM((MOE_ROWS, D_MODEL), F32), pltpu.SemaphoreType.DMA(())],
    )
    return pl.pallas_call(
        _dispatch_kernel,
        grid_spec=grid_spec,
        out_shape=[
            jax.ShapeDtypeStruct((n_rows, D_MODEL), F32),
            jax.ShapeDtypeStruct((t, D_MODEL), F32),
        ],
        compiler_params=pltpu.CompilerParams(
            dimension_semantics=("arbitrary",), vmem_limit_bytes=VMEM_LIMIT),
        name="dispatch",
    )(pad_start, pad_cnt, n_used, dest_flat, h1, p, w_pg, w_pp)


def _experts_kernel(be_ref, nu_ref, x_ref, wgu_ref, bgu_ref, wd_ref, bd_ref, y_ref, wgu_bf_ref, wd_bf_ref):
    i = pl.program_id(0)
    used = i < nu_ref[0]

    @pl.when((i == 0) | (be_ref[i] != be_ref[jnp.maximum(i - 1, 0)]))
    def _():
        wgu_bf_ref[...] = wgu_ref[0].astype(BF16)
        wd_bf_ref[...] = wd_ref[0].astype(BF16)

    @pl.when(used)
    def _():
        hgu = _dot(x_ref[...].astype(BF16), wgu_bf_ref[...]) + bgu_ref[0]
        gate = jnp.minimum(hgu[:, :D_FF], SWIGLU_LIMIT)
        up = jnp.clip(hgu[:, D_FF:], -SWIGLU_LIMIT, SWIGLU_LIMIT)
        act = (up + 1.0) * gate * _sigmoid(SWIGLU_ALPHA * gate)
        y_ref[...] = _dot(act.astype(BF16), wd_bf_ref[...]) + bd_ref[0]

    @pl.when(jnp.logical_not(used))
    def _():
        y_ref[...] = jnp.zeros_like(y_ref)


def _experts(block_e, n_used, xs, w_gu, b_gu, w_d, b_d):
    n_rows = xs.shape[0]
    n_blocks = n_rows // MOE_ROWS
    grid_spec = pltpu.PrefetchScalarGridSpec(
        num_scalar_prefetch=2,
        grid=(n_blocks,),
        in_specs=[
            pl.BlockSpec((MOE_ROWS, D_MODEL), lambda i, be, nu: (jnp.minimum(i, nu[0] - 1), 0)),
            pl.BlockSpec((1, D_MODEL, 2 * D_FF), lambda i, be, nu: (be[i], 0, 0)),
            pl.BlockSpec((1, 1, 2 * D_FF), lambda i, be, nu: (be[i], 0, 0)),
            pl.BlockSpec((1, D_FF, D_MODEL), lambda i, be, nu: (be[i], 0, 0)),
            pl.BlockSpec((1, 1, D_MODEL), lambda i, be, nu: (be[i], 0, 0)),
        ],
        out_specs=pl.BlockSpec((MOE_ROWS, D_MODEL), lambda i, be, nu: (i, 0)),
        scratch_shapes=[pltpu.VMEM((D_MODEL, 2 * D_FF), BF16), pltpu.VMEM((D_FF, D_MODEL), BF16)],
    )
    return pl.pallas_call(
        _experts_kernel,
        grid_spec=grid_spec,
        out_shape=jax.ShapeDtypeStruct((n_rows, D_MODEL), F32),
        compiler_params=pltpu.CompilerParams(
            dimension_semantics=("arbitrary",), vmem_limit_bytes=VMEM_LIMIT),
        name="experts",
    )(block_e, n_used, xs, w_gu, b_gu, w_d, b_d)


GATHER_SLOTS = 3


def _final_kernel(dest0_ref, dest1_ref, dest2_ref, ys_hbm, tw_ref, res_ref, lg_ref, lb_ref, o_ref, g_ref, sem):
    tm = res_ref.shape[0]
    i = pl.program_id(0)
    n = pl.num_programs(0)
    slot = i % GATHER_SLOTS

    def gather(idx_ref, s):
        for r in range(tm):
            for k in range(TOP_K):
                pltpu.make_async_copy(_row(ys_hbm, idx_ref[r * TOP_K + k]), _row(g_ref.at[s, k], r),
                                      sem.at[s]).start(priority=k % DMA_QUEUES)

    def combine():
        tw = tw_ref[...]
        acc = res_ref[...]
        for k in range(TOP_K):
            acc = acc + tw[:, k:k + 1] * g_ref[slot, k]
        o_ref[...] = _layer_norm(acc, lg_ref[...], lb_ref[...])

    @pl.when(i == 0)
    def _():
        gather(dest0_ref, 0)

        @pl.when(n > 1)
        def _():
            gather(dest1_ref, 1)

    for _ in range(tm * TOP_K // DMA_BATCH):
        _wait_rows(ys_hbm, DMA_BATCH, sem.at[slot])

    @pl.when(i + 2 < n)
    def _():
        combine()
        gather(dest2_ref, (i + 2) % GATHER_SLOTS)

    @pl.when(i + 2 >= n)
    def _():
        combine()


def _final(dest_flat, ys, top_w, res, ln_g, ln_b, tm=256):
    t = res.shape[0]
    n_steps = t // tm
    row = lambda i: (i, 0)
    const = lambda i: (0, 0)

    def dest_tile(ahead):
        return pl.BlockSpec((tm * TOP_K,), lambda i: (jnp.minimum(i + ahead, n_steps - 1),),
                            memory_space=pltpu.SMEM)

    return pl.pallas_call(
        _final_kernel,
        grid=(n_steps,),
        in_specs=[
            dest_tile(0), dest_tile(1), dest_tile(2),
            pl.BlockSpec(memory_space=pl.ANY),
            pl.BlockSpec((tm, LANES), row),
            pl.BlockSpec((tm, D_MODEL), row),
            pl.BlockSpec((1, D_MODEL), const),
            pl.BlockSpec((1, D_MODEL), const),
        ],
        out_specs=pl.BlockSpec((tm, D_MODEL), row),
        out_shape=jax.ShapeDtypeStruct((t, D_MODEL), F32),
        scratch_shapes=[pltpu.VMEM((GATHER_SLOTS, TOP_K, tm, D_MODEL), F32),
                        pltpu.SemaphoreType.DMA((GATHER_SLOTS,))],
        compiler_params=pltpu.CompilerParams(
            dimension_semantics=("arbitrary",), vmem_limit_bytes=VMEM_LIMIT),
        name="final",
    )(dest_flat, dest_flat, dest_flat, ys, top_w, res, ln_g, ln_b)


def kernel(x, p, emb_ln_g, emb_ln_b, hgrn_lb, w_in, w_gla_up, b_gla_up, norm_a_g, norm_b_g, w_proj_a, w_proj_b, w_out, ln_mix_g, ln_mix_b, w_router, b_router, w_gate_up, b_gate_up, w_down, b_down, w_ple_gate, w_ple_proj, ln_moe_g, ln_moe_b):
    bsz, seq, d = x.shape
    t = bsz * seq
    assert d == D_MODEL and seq % GLA_TILE == 0 and t % 1024 == 0
    assert w_in.shape[0] == DEPTH == 1
    vec = lambda a: a.reshape(1, -1).astype(F32)

    lb0 = jax.nn.softmax(hgrn_lb.astype(F32), axis=0)[0:1]
    w_in0 = w_in[0]
    w_main = jnp.concatenate([w_in0[:, :W_IN_LR], w_in0[:, W_IN_LR + GLA_RANK:]], axis=1).astype(BF16)
    w_lr = jnp.pad(w_in0[:, W_IN_LR:W_IN_LR + GLA_RANK], ((0, 0), (0, LANES - GLA_RANK))).astype(BF16)
    w_up = jnp.pad(w_gla_up[0], ((0, LANES - GLA_RANK), (0, 0))).astype(BF16)
    wr = jnp.pad(w_router[0].astype(F32), ((0, 0), (0, LANES - N_EXPERTS)))
    wr_hi = wr.astype(BF16)
    wr_lo = (wr - wr_hi.astype(F32)).astype(BF16)
    b_r = jnp.pad(b_router[0].astype(F32), (0, LANES - N_EXPERTS)).reshape(1, LANES)

    xt = x.reshape(t, d)
    h0, z, z_lr = _ln_inproj(xt, vec(emb_ln_g), vec(emb_ln_b), w_main, w_lr)
    o = _gla(z, z_lr, lb0, w_up, vec(b_gla_up[0]), vec(norm_a_g[0]), vec(norm_b_g[0]), bsz, seq)
    h1, top_e, top_w = _post_mixer(
        o, z, h0, w_proj_a[0].astype(BF16), w_proj_b[0].astype(BF16), w_out[0].astype(BF16),
        vec(ln_mix_g[0]), vec(ln_mix_b[0]), wr_hi, wr_lo, b_r)

    rank, cnt = _rank(top_e)
    counts = cnt[0, :N_EXPERTS].astype(jnp.int32)
    padded = (counts + MOE_ROWS - 1) // MOE_ROWS * MOE_ROWS
    pend = jnp.cumsum(padded)
    pstart = pend - padded
    n_blocks = t * TOP_K // MOE_ROWS + N_EXPERTS
    n_rows = n_blocks * MOE_ROWS
    n_used = pend[N_EXPERTS - 1:] // MOE_ROWS
    dest = (pstart[top_e[:, :TOP_K]] + rank[:, :TOP_K]).reshape(-1)
    block_pos = jnp.minimum(jnp.arange(n_blocks, dtype=jnp.int32), n_used - 1) * MOE_ROWS
    block_e = jnp.minimum(
        jnp.sum((pend[None, :] <= block_pos[:, None]).astype(jnp.int32), axis=1), N_EXPERTS - 1)

    pad_start = (pstart + counts) // PAD_GROUP * PAD_GROUP
    xs, res = _dispatch(pad_start, (pend - pad_start) // PAD_GROUP, n_used, dest, h1,
                        p[0].reshape(t, PLE_DIM), w_ple_gate[0].astype(BF16), w_ple_proj[0].astype(BF16),
                        n_rows)
    ys = _experts(block_e, n_used, xs, w_gate_up[0].astype(F32),
                  b_gate_up[0].reshape(N_EXPERTS, 1, 2 * D_FF).astype(F32),
                  w_down[0].astype(F32), b_down[0].reshape(N_EXPERTS, 1, d).astype(F32))
    out = _final(dest, ys, top_w, res, vec(ln_moe_g[0]), vec(ln_moe_b[0]))
    return out.reshape(bsz, seq, d)
```

```python
import jax
import jax.numpy as jnp
from jax import lax
from jax.experimental import pallas as pl
from jax.experimental.pallas import tpu as pltpu

F32 = jnp.float32
BF16 = jnp.bfloat16

D_MODEL = 1024
A_HEADS, A_DK, A_DV = 8, 128, 128
B_HEADS, B_DK, B_DV = 4, 128, 256
N_HEADS = A_HEADS + B_HEADS
HEAD_DK = 128
A_V = A_HEADS * A_DV
GLA_RANK = 16
GLA_TAU = 16.0
N_EXPERTS = 32
TOP_K = 4
D_FF = 1024
SWIGLU_LIMIT = 7.0
SWIGLU_ALPHA = 1.702
PLE_DIM = 256
LN_EPS = 1e-5
RMS_EPS = 1e-6
DEPTH = 1
DN_ALPHA = (2.0 * DEPTH) ** 0.25

LANES = 128
GLA_CHUNK = 64
GLA_TILE = 256
GLA_SAFE_LOG_DECAY = -60.0
MOE_ROWS = 512
DMA_BATCH = 128
DMA_QUEUES = 2
F32_SUBLANES = 8
PAD_GROUP = F32_SUBLANES
VMEM_LIMIT = 56 * 1024 * 1024

Z_QA, Z_FA, Z_IA, Z_GA = 0, 1024, 2048, 3072
Z_QB, Z_KB, Z_VB, Z_RB = 4096, 4608, 5120, 6144
Z_MIXER = 7168
Z_GATE_A, Z_GATE_B = 7168, 8192
Z_COLS = 9216
W_IN_LR = 7168


def _layer_norm(x, g, b):
    mu = jnp.mean(x, axis=-1, keepdims=True)
    xc = x - mu
    var = jnp.mean(xc * xc, axis=-1, keepdims=True)
    return xc * lax.rsqrt(var + LN_EPS) * g + b


def _sigmoid(x):
    return jax.nn.sigmoid(x)


def _sigmoid_pair(x):
    e = jnp.exp(-jnp.abs(x))
    r = 1.0 / (1.0 + e)
    er = e * r
    pos = x >= 0.0
    return jnp.where(pos, r, er), jnp.where(pos, er, r)


def _dot(a, b):
    return jnp.dot(a, b, preferred_element_type=F32)


def _dot_nt(a, b):
    return lax.dot_general(a, b, (((1,), (1,)), ((), ())), preferred_element_type=F32)


def _dot_tn(a, b):
    return lax.dot_general(a, b, (((0,), (0,)), ((), ())), preferred_element_type=F32)


def _ln_inproj_kernel(x_ref, g_ref, b_ref, w_ref, wlr_ref, h_ref, z_ref, zlr_ref, hb_ref):
    @pl.when(pl.program_id(1) == 0)
    def _():
        h = _layer_norm(x_ref[...], g_ref[...], b_ref[...])
        h_ref[...] = h
        hb = h.astype(BF16)
        hb_ref[...] = hb
        zlr_ref[...] = _dot(hb, wlr_ref[...])

    z_ref[...] = _dot(hb_ref[...], w_ref[...])


def _ln_inproj(x, g, b, w_main, w_lr, tm=1024, tn=2304):
    t = x.shape[0]
    return pl.pallas_call(
        _ln_inproj_kernel,
        grid=(t // tm, Z_COLS // tn),
        in_specs=[
            pl.BlockSpec((tm, D_MODEL), lambda i, j: (i, 0)),
            pl.BlockSpec((1, D_MODEL), lambda i, j: (0, 0)),
            pl.BlockSpec((1, D_MODEL), lambda i, j: (0, 0)),
            pl.BlockSpec((D_MODEL, tn), lambda i, j: (0, j)),
            pl.BlockSpec((D_MODEL, LANES), lambda i, j: (0, 0)),
        ],
        out_specs=[
            pl.BlockSpec((tm, D_MODEL), lambda i, j: (i, 0)),
            pl.BlockSpec((tm, tn), lambda i, j: (i, j)),
            pl.BlockSpec((tm, LANES), lambda i, j: (i, 0)),
        ],
        out_shape=[
            jax.ShapeDtypeStruct((t, D_MODEL), F32),
            jax.ShapeDtypeStruct((t, Z_COLS), F32),
            jax.ShapeDtypeStruct((t, LANES), F32),
        ],
        scratch_shapes=[pltpu.VMEM((tm, D_MODEL), BF16)],
        compiler_params=pltpu.CompilerParams(
            dimension_semantics=("parallel", "arbitrary"), vmem_limit_bytes=VMEM_LIMIT),
        name="ln_inproj",
    )(x, g, b, w_main, w_lr)


def _chunk_tri(tile, chunk):
    i = jnp.arange(tile)[:, None]
    j = jnp.arange(tile)[None, :]
    return ((j <= i) & (i // chunk == j // chunk)).astype(BF16)


def _chunk_cumsum(tri, x):
    hi = x.astype(BF16)
    lo = (x - hi.astype(F32)).astype(BF16)
    return _dot(tri, hi) + _dot(tri, lo)


def _gla_prep(z_ref, lr_ref, lb_ref, wup_ref, bup_ref, tri_ref, qs_ref, q1_ref, ks_ref, bs_ref):
    low = None
    tri = tri_ref[...]

    def put(h, q, k, b):
        qs_ref[h] = q
        q1_ref[h] = (q * jnp.exp(b)).astype(BF16)
        ks_ref[h] = k
        bs_ref[h] = b

    for h in range(A_HEADS):
        cols = slice(h * A_DK, (h + 1) * A_DK)
        lb = lb_ref[:, cols]
        sig, sig_neg = _sigmoid_pair(z_ref[:, Z_FA + h * A_DK:Z_FA + (h + 1) * A_DK])
        b = _chunk_cumsum(tri, jnp.log(lb + (1.0 - lb) * sig))
        qz = z_ref[:, Z_QA + h * A_DK:Z_QA + (h + 1) * A_DK]
        put(h, qz * _sigmoid(qz) * (A_DK ** -0.5), (1.0 - lb) * sig_neg, b)
        low = b if low is None else jnp.minimum(low, b)
    u_all = _dot(lr_ref[...].astype(BF16), wup_ref[...]) + bup_ref[...]
    for hb in range(B_HEADS):
        u = u_all[:, hb * B_DK:(hb + 1) * B_DK]
        log_sig = jnp.minimum(u, 0.0) - jnp.log(1.0 + jnp.exp(-jnp.abs(u)))
        b = _chunk_cumsum(tri, log_sig / GLA_TAU)
        put(A_HEADS + hb, z_ref[:, Z_QB + hb * B_DK:Z_QB + (hb + 1) * B_DK] * (B_DK ** -0.5),
            z_ref[:, Z_KB + hb * B_DK:Z_KB + (hb + 1) * B_DK], b)
        low = jnp.minimum(low, b)
    return low


def _product_scores(q1_ref, ks_ref, bs_ref, sc_ref):
    c = GLA_CHUNK
    for h in range(N_HEADS):
        k1 = (ks_ref[h] * jnp.exp(-bs_ref[h])).astype(BF16)
        for ci in range(GLA_TILE // c):
            rows = slice(ci * c, (ci + 1) * c)
            sc_ref[h, ci] = _dot_nt(q1_ref[h, rows, :], k1[rows])


def _exact_scores(qs_ref, ks_ref, bs_ref, sc_ref):
    c = GLA_CHUNK
    n = GLA_TILE // c
    col = lax.broadcasted_iota(jnp.int32, (c, c), 1)

    def per_chunk(idx, carry):
        h = idx // n
        ci = idx - h * n
        r0 = pl.multiple_of(ci * c, c)
        qc = qs_ref[h, pl.ds(r0, c), :]
        bc = bs_ref[h, pl.ds(r0, c), :]

        def column(j, acc):
            kj = ks_ref[h, pl.ds(r0 + j, 1), :]
            bj = bs_ref[h, pl.ds(r0 + j, 1), :]
            term = qc * kj * jnp.exp(jnp.minimum(bc - bj, 0.0))
            return jnp.where(col == j, jnp.sum(term, axis=-1, keepdims=True), acc)

        sc_ref[h, ci] = lax.fori_loop(0, c, column, jnp.zeros((c, c), F32))
        return carry

    lax.fori_loop(0, N_HEADS * n, per_chunk, 0)


def _gla_outputs(z_ref, q1_ref, ks_ref, bs_ref, sc_ref, sta_ref, stb_ref, nga_ref, ngb_ref, o_ref):
    c = GLA_CHUNK
    causal = lax.broadcasted_iota(jnp.int32, (c, c), 0) >= lax.broadcasted_iota(jnp.int32, (c, c), 1)
    for ci in range(GLA_TILE // c):
        rows = slice(ci * c, (ci + 1) * c)
        for h in range(N_HEADS):
            if h < A_HEADS:
                dv, st_ref, norm_g = A_DV, sta_ref.at[h], nga_ref[...]
                v_off, g_off, o_off = Z_IA + h * A_DV, Z_GA + h * A_DV, h * A_DV
            else:
                hb = h - A_HEADS
                dv, st_ref, norm_g = B_DV, stb_ref.at[hb], ngb_ref[...]
                v_off, g_off, o_off = Z_VB + hb * B_DV, Z_RB + hb * B_DV, A_V + hb * B_DV
            st = st_ref[...]
            k = ks_ref[h, rows, :]
            bc = bs_ref[h, rows, :]
            b_last = bc[c - 1:c]
            k_dec = (k * jnp.exp(b_last - bc)).astype(BF16)
            vb = z_ref[rows, v_off:v_off + dv].astype(BF16)
            scores = jnp.where(causal, sc_ref[h, ci], 0.0).astype(BF16)
            o = _dot_nt(q1_ref[h, rows, :], st.astype(BF16)) + _dot(scores, vb)
            st_ref[...] = st * jnp.exp(b_last) + _dot_tn(vb, k_dec)
            on = o * lax.rsqrt(jnp.mean(o * o, axis=-1, keepdims=True) + RMS_EPS) * norm_g
            gc = z_ref[rows, g_off:g_off + dv]
            o_ref[rows, o_off:o_off + dv] = (on * (gc * _sigmoid(gc))).astype(o_ref.dtype)


def _gla_kernel(z_ref, lr_ref, lb_ref, wup_ref, bup_ref, tri_ref, nga_ref, ngb_ref, o_ref,
                sta_ref, stb_ref, qs_ref, q1_ref, ks_ref, bs_ref, sc_ref):
    @pl.when(pl.program_id(1) == 0)
    def _():
        sta_ref[...] = jnp.zeros_like(sta_ref)
        stb_ref[...] = jnp.zeros_like(stb_ref)

    low = _gla_prep(z_ref, lr_ref, lb_ref, wup_ref, bup_ref, tri_ref, qs_ref, q1_ref, ks_ref, bs_ref)
    safe = jnp.min(low) >= GLA_SAFE_LOG_DECAY

    @pl.when(safe)
    def _product_form():
        _product_scores(q1_ref, ks_ref, bs_ref, sc_ref)

    @pl.when(jnp.logical_not(safe))
    def _exact_form():
        _exact_scores(qs_ref, ks_ref, bs_ref, sc_ref)

    _gla_outputs(z_ref, q1_ref, ks_ref, bs_ref, sc_ref, sta_ref, stb_ref, nga_ref, ngb_ref, o_ref)


def _gla(z, z_lr, lb, w_up, b_up, norm_a, norm_b, bsz, seq):
    ts = GLA_TILE
    nt = seq // ts
    n_chunks = ts // GLA_CHUNK
    row = lambda b, t: (b * nt + t, 0)
    const = lambda b, t: (0, 0)
    per_head = pltpu.VMEM((N_HEADS, ts, HEAD_DK), F32)
    return pl.pallas_call(
        _gla_kernel,
        grid=(bsz, nt),
        in_specs=[
            pl.BlockSpec((ts, Z_MIXER), row),
            pl.BlockSpec((ts, LANES), row),
            pl.BlockSpec((1, A_HEADS * A_DK), const),
            pl.BlockSpec((LANES, B_HEADS * B_DK), const),
            pl.BlockSpec((1, B_HEADS * B_DK), const),
            pl.BlockSpec((ts, ts), const),
            pl.BlockSpec((1, A_DV), const),
            pl.BlockSpec((1, B_DV), const),
        ],
        out_specs=pl.BlockSpec((ts, 2 * D_MODEL), row),
        out_shape=jax.ShapeDtypeStruct((bsz * seq, 2 * D_MODEL), BF16),
        scratch_shapes=[
            pltpu.VMEM((A_HEADS, A_DV, A_DK), F32),
            pltpu.VMEM((B_HEADS, B_DV, B_DK), F32),
            per_head, pltpu.VMEM((N_HEADS, ts, HEAD_DK), BF16), per_head, per_head,
            pltpu.VMEM((N_HEADS, n_chunks, GLA_CHUNK, GLA_CHUNK), F32),
        ],
        compiler_params=pltpu.CompilerParams(
            dimension_semantics=("parallel", "arbitrary"), vmem_limit_bytes=VMEM_LIMIT),
        name="gla",
    )(z, z_lr, lb, w_up, b_up, _chunk_tri(ts, GLA_CHUNK), norm_a, norm_b)


def _lane_pack(cols, dtype):
    m = cols[0].shape[0]
    lane = lax.broadcasted_iota(jnp.int32, (m, LANES), 1)
    out = jnp.zeros((m, LANES), dtype)
    for k, cvals in enumerate(cols):
        out = jnp.where(lane == k, cvals.astype(dtype), out)
    return out


def _post_mixer_kernel(o_ref, ga_ref, gb_ref, h0_ref, wa_ref, wb_ref, wo_ref, lg_ref, lb_ref,
                       wrh_ref, wrl_ref, br_ref, h1_ref, te_ref, tw_ref):
    ya = _dot(o_ref[:, :D_MODEL], wa_ref[...])
    yb = _dot(o_ref[:, D_MODEL:], wb_ref[...])
    merged = _sigmoid(ga_ref[...]) * ya + _sigmoid(gb_ref[...]) * yb
    y = _dot(merged.astype(BF16), wo_ref[...])
    h1 = _layer_norm(DN_ALPHA * h0_ref[...] + y, lg_ref[...], lb_ref[...])
    h1_ref[...] = h1

    hh = h1.astype(BF16)
    hl = (h1 - hh.astype(F32)).astype(BF16)
    logits = _dot(hh, wrh_ref[...]) + (_dot(hl, wrh_ref[...]) + _dot(hh, wrl_ref[...])) + br_ref[...]
    lane = lax.broadcasted_iota(jnp.int32, logits.shape, 1)
    work = jnp.where(lane < N_EXPERTS, logits, -jnp.inf)
    vals, idxs = [], []
    for _ in range(TOP_K):
        m = jnp.max(work, axis=-1, keepdims=True)
        idx = jnp.min(jnp.where(work == m, lane, LANES), axis=-1, keepdims=True)
        vals.append(m)
        idxs.append(idx)
        work = jnp.where(lane == idx, -jnp.inf, work)
    exps = [jnp.exp(v - vals[0]) for v in vals]
    denom = exps[0] + exps[1] + exps[2] + exps[3]
    te_ref[...] = _lane_pack(idxs, jnp.int32)
    tw_ref[...] = _lane_pack([e / denom for e in exps], F32)


def _post_mixer(o, z, h0, w_a, w_b, w_o, ln_g, ln_b, wr_hi, wr_lo, b_r, tm=512):
    t = h0.shape[0]
    row = lambda i: (i, 0)
    const = lambda i: (0, 0)
    full_w = pl.BlockSpec((D_MODEL, D_MODEL), const)
    vec = pl.BlockSpec((1, D_MODEL), const)
    return pl.pallas_call(
        _post_mixer_kernel,
        grid=(t // tm,),
        in_specs=[
            pl.BlockSpec((tm, 2 * D_MODEL), row),
            pl.BlockSpec((tm, D_MODEL), lambda i: (i, Z_GATE_A // D_MODEL)),
            pl.BlockSpec((tm, D_MODEL), lambda i: (i, Z_GATE_B // D_MODEL)),
            pl.BlockSpec((tm, D_MODEL), row),
            full_w, full_w, full_w, vec, vec,
            pl.BlockSpec((D_MODEL, LANES), const),
            pl.BlockSpec((D_MODEL, LANES), const),
            pl.BlockSpec((1, LANES), const),
        ],
        out_specs=[
            pl.BlockSpec((tm, D_MODEL), row),
            pl.BlockSpec((tm, LANES), row),
            pl.BlockSpec((tm, LANES), row),
        ],
        out_shape=[
            jax.ShapeDtypeStruct((t, D_MODEL), F32),
            jax.ShapeDtypeStruct((t, LANES), jnp.int32),
            jax.ShapeDtypeStruct((t, LANES), F32),
        ],
        compiler_params=pltpu.CompilerParams(
            dimension_semantics=("parallel",), vmem_limit_bytes=VMEM_LIMIT),
        name="post_mixer",
    )(o, z, z, h0, w_a, w_b, w_o, ln_g, ln_b, wr_hi, wr_lo, b_r)


def _rank_kernel(te_ref, rank_ref, cnt_ref, carry_ref):
    @pl.when(pl.program_id(0) == 0)
    def _():
        carry_ref[...] = jnp.zeros_like(carry_ref)

    te = te_ref[...]
    tr = te.shape[0]
    lane = lax.broadcasted_iota(jnp.int32, (tr, LANES), 1)
    onehots = [te[:, k:k + 1] == lane for k in range(TOP_K)]
    cnt = jnp.zeros((tr, LANES), F32)
    for oh in onehots:
        cnt = cnt + oh.astype(F32)
    strict_lower = (lax.broadcasted_iota(jnp.int32, (tr, tr), 0)
                    > lax.broadcasted_iota(jnp.int32, (tr, tr), 1))
    earlier = _dot(strict_lower.astype(BF16), cnt.astype(BF16)) + carry_ref[...]
    ranks = [jnp.sum(jnp.where(oh, earlier, 0.0), axis=-1, keepdims=True) for oh in onehots]
    rank_ref[...] = _lane_pack(ranks, jnp.int32)
    carry_ref[...] = carry_ref[...] + jnp.sum(cnt, axis=0, keepdims=True)
    cnt_ref[...] = carry_ref[...]


def _rank(top_e, tr=512):
    t = top_e.shape[0]
    return pl.pallas_call(
        _rank_kernel,
        grid=(t // tr,),
        in_specs=[pl.BlockSpec((tr, LANES), lambda i: (i, 0))],
        out_specs=[
            pl.BlockSpec((tr, LANES), lambda i: (i, 0)),
            pl.BlockSpec((1, LANES), lambda i: (0, 0)),
        ],
        out_shape=[
            jax.ShapeDtypeStruct((t, LANES), jnp.int32),
            jax.ShapeDtypeStruct((1, LANES), F32),
        ],
        scratch_shapes=[pltpu.VMEM((1, LANES), F32)],
        compiler_params=pltpu.CompilerParams(dimension_semantics=("arbitrary",)),
        name="rank",
    )(top_e)


def _row(ref, r):
    return ref.at[pl.ds(r, 1), :]


def _wait_rows(ref_hbm, n, sem):
    pltpu.make_async_copy(ref_hbm.at[pl.ds(0, n), :], ref_hbm.at[pl.ds(0, n), :], sem).wait()


def _dispatch_kernel(pad_start_ref, pad_cnt_ref, nu_ref, dest_ref, h_ref, p_ref, wpg_ref, wpp_ref,
                     xs_hbm, res_ref, zero_ref, sem):
    n_tok = h_ref.shape[0]

    @pl.when(pl.program_id(0) == 0)
    def _zero_rows_without_token():
        zero_ref[...] = jnp.zeros_like(zero_ref)

        def wait_pad_groups(n):
            @pl.when(n > 0)
            def _():
                _wait_rows(xs_hbm, pl.multiple_of(n * PAD_GROUP, PAD_GROUP), sem)

        def per_expert(e, n_prev):
            def start(g, c):
                r0 = pl.multiple_of(pad_start_ref[e] + g * PAD_GROUP, PAD_GROUP)
                pltpu.make_async_copy(zero_ref.at[pl.ds(0, PAD_GROUP), :],
                                      xs_hbm.at[pl.ds(r0, PAD_GROUP), :], sem).start()
                return c

            lax.fori_loop(0, pad_cnt_ref[e], start, 0)
            wait_pad_groups(n_prev)
            return pad_cnt_ref[e]

        wait_pad_groups(lax.fori_loop(0, N_EXPERTS, per_expert, 0))

        def block_copy(blk):
            r0 = pl.multiple_of(blk * MOE_ROWS, MOE_ROWS)
            return pltpu.make_async_copy(zero_ref, xs_hbm.at[pl.ds(r0, MOE_ROWS), :], sem)

        def start_block(blk, c):
            block_copy(blk).start()
            return c

        def wait_block(blk, c):
            block_copy(blk).wait()
            return c

        n_blocks = xs_hbm.shape[0] // MOE_ROWS
        lax.fori_loop(nu_ref[0], n_blocks, start_block, 0)
        lax.fori_loop(nu_ref[0], n_blocks, wait_block, 0)

    for r in range(n_tok):
        for k in range(TOP_K):
            pltpu.make_async_copy(_row(h_ref, r), _row(xs_hbm, dest_ref[r * TOP_K + k]), sem).start(
                priority=k % DMA_QUEUES)
    h1 = h_ref[...]
    ple = _sigmoid(_dot(h1.astype(BF16), wpg_ref[...])) * _dot(p_ref[...].astype(BF16), wpp_ref[...])
    res_ref[...] = DN_ALPHA * h1 + ple
    for _ in range(n_tok * TOP_K // DMA_BATCH):
        _wait_rows(xs_hbm, DMA_BATCH, sem)


def _dispatch(pad_start, pad_cnt, n_used, dest_flat, h1, p, w_pg, w_pp, n_rows, tok_per_step=256):
    t = h1.shape[0]
    row = lambda i, ps, pc, nu: (i, 0)
    const = lambda i, ps, pc, nu: (0, 0)
    grid_spec = pltpu.PrefetchScalarGridSpec(
        num_scalar_prefetch=3,
        grid=(t // tok_per_step,),
        in_specs=[
            pl.BlockSpec((tok_per_step * TOP_K,), lambda i, ps, pc, nu: (i,), memory_space=pltpu.SMEM),
            pl.BlockSpec((tok_per_step, D_MODEL), row),
            pl.BlockSpec((tok_per_step, PLE_DIM), row),
            pl.BlockSpec((D_MODEL, D_MODEL), const),
            pl.BlockSpec((PLE_DIM, D_MODEL), const),
        ],
        out_specs=[
            pl.BlockSpec(memory_space=pl.ANY),
            pl.BlockSpec((tok_per_step, D_MODEL), row),
        ],
        scratch_shapes=[pltpu.VMEM((MOE_ROWS, D_MODEL), F32), pltpu.SemaphoreType.DMA(())],
    )
    return pl.pallas_call(
        _dispatch_kernel,
        grid_spec=grid_spec,
        out_shape=[
            jax.ShapeDtypeStruct((n_rows, D_MODEL), F32),
            jax.ShapeDtypeStruct((t, D_MODEL), F32),
        ],
        compiler_params=pltpu.CompilerParams(
            dimension_semantics=("arbitrary",), vmem_limit_bytes=VMEM_LIMIT),
        name="dispatch",
    )(pad_start, pad_cnt, n_used, dest_flat, h1, p, w_pg, w_pp)


def _experts_kernel(be_ref, nu_ref, nv_ref, x_ref, wgu_ref, bgu_ref, wd_ref, bd_ref, y_ref,
                    wgu_bf_ref, wd_bf_ref):
    del nu_ref
    i = pl.program_id(0)
    n_valid = nv_ref[i]
    half = MOE_ROWS // 2

    @pl.when((i == 0) | (be_ref[i] != be_ref[jnp.maximum(i - 1, 0)]))
    def _():
        wgu_bf_ref[...] = wgu_ref[0].astype(BF16)
        wd_bf_ref[...] = wd_ref[0].astype(BF16)

    def mlp(rows):
        hgu = _dot(x_ref[rows, :].astype(BF16), wgu_bf_ref[...]) + bgu_ref[0]
        gate = jnp.minimum(hgu[:, :D_FF], SWIGLU_LIMIT)
        up = jnp.clip(hgu[:, D_FF:], -SWIGLU_LIMIT, SWIGLU_LIMIT)
        act = (up + 1.0) * gate * _sigmoid(SWIGLU_ALPHA * gate)
        y_ref[rows, :] = _dot(act.astype(BF16), wd_bf_ref[...]) + bd_ref[0]

    @pl.when(n_valid > half)
    def _():
        mlp(slice(0, MOE_ROWS))

    @pl.when((n_valid > 0) & (n_valid <= half))
    def _():
        mlp(slice(0, half))
        y_ref[half:, :] = jnp.zeros((MOE_ROWS - half, D_MODEL), F32)

    @pl.when(n_valid == 0)
    def _():
        y_ref[...] = jnp.zeros_like(y_ref)


def _experts(block_e, n_used, n_valid, xs, w_gu, b_gu, w_d, b_d):
    n_rows = xs.shape[0]
    n_blocks = n_rows // MOE_ROWS
    grid_spec = pltpu.PrefetchScalarGridSpec(
        num_scalar_prefetch=3,
        grid=(n_blocks,),
        in_specs=[
            pl.BlockSpec((MOE_ROWS, D_MODEL), lambda i, be, nu, nv: (jnp.minimum(i, nu[0] - 1), 0)),
            pl.BlockSpec((1, D_MODEL, 2 * D_FF), lambda i, be, nu, nv: (be[i], 0, 0)),
            pl.BlockSpec((1, 1, 2 * D_FF), lambda i, be, nu, nv: (be[i], 0, 0)),
            pl.BlockSpec((1, D_FF, D_MODEL), lambda i, be, nu, nv: (be[i], 0, 0)),
            pl.BlockSpec((1, 1, D_MODEL), lambda i, be, nu, nv: (be[i], 0, 0)),
        ],
        out_specs=pl.BlockSpec((MOE_ROWS, D_MODEL), lambda i, be, nu, nv: (i, 0)),
        scratch_shapes=[pltpu.VMEM((D_MODEL, 2 * D_FF), BF16), pltpu.VMEM((D_FF, D_MODEL), BF16)],
    )
    return pl.pallas_call(
        _experts_kernel,
        grid_spec=grid_spec,
        out_shape=jax.ShapeDtypeStruct((n_rows, D_MODEL), F32),
        compiler_params=pltpu.CompilerParams(
            dimension_semantics=("arbitrary",), vmem_limit_bytes=VMEM_LIMIT),
        name="experts",
    )(block_e, n_used, n_valid, xs, w_gu, b_gu, w_d, b_d)


GATHER_SLOTS = 3


def _final_kernel(dest0_ref, dest1_ref, dest2_ref, ys_hbm, tw_ref, res_ref, lg_ref, lb_ref, o_ref, g_ref, sem):
    tm = res_ref.shape[0]
    i = pl.program_id(0)
    n = pl.num_programs(0)
    slot = i % GATHER_SLOTS

    def gather(idx_ref, s):
        for r in range(tm):
            for k in range(TOP_K):
                pltpu.make_async_copy(_row(ys_hbm, idx_ref[r * TOP_K + k]), _row(g_ref.at[s, k], r),
                                      sem.at[s]).start(priority=k % DMA_QUEUES)

    def combine():
        tw = tw_ref[...]
        acc = res_ref[...]
        for k in range(TOP_K):
            acc = acc + tw[:, k:k + 1] * g_ref[slot, k]
        o_ref[...] = _layer_norm(acc, lg_ref[...], lb_ref[...])

    @pl.when(i == 0)
    def _():
        gather(dest0_ref, 0)

        @pl.when(n > 1)
        def _():
            gather(dest1_ref, 1)

    for _ in range(tm * TOP_K // DMA_BATCH):
        _wait_rows(ys_hbm, DMA_BATCH, sem.at[slot])

    @pl.when(i + 2 < n)
    def _():
        combine()
        gather(dest2_ref, (i + 2) % GATHER_SLOTS)

    @pl.when(i + 2 >= n)
    def _():
        combine()


def _final(dest_flat, ys, top_w, res, ln_g, ln_b, tm=256):
    t = res.shape[0]
    n_steps = t // tm
    row = lambda i: (i, 0)
    const = lambda i: (0, 0)

    def dest_tile(ahead):
        return pl.BlockSpec((tm * TOP_K,), lambda i: (jnp.minimum(i + ahead, n_steps - 1),),
                            memory_space=pltpu.SMEM)

    return pl.pallas_call(
        _final_kernel,
        grid=(n_steps,),
        in_specs=[
            dest_tile(0), dest_tile(1), dest_tile(2),
            pl.BlockSpec(memory_space=pl.ANY),
            pl.BlockSpec((tm, LANES), row),
            pl.BlockSpec((tm, D_MODEL), row),
            pl.BlockSpec((1, D_MODEL), const),
            pl.BlockSpec((1, D_MODEL), const),
        ],
        out_specs=pl.BlockSpec((tm, D_MODEL), row),
        out_shape=jax.ShapeDtypeStruct((t, D_MODEL), F32),
        scratch_shapes=[pltpu.VMEM((GATHER_SLOTS, TOP_K, tm, D_MODEL), F32),
                        pltpu.SemaphoreType.DMA((GATHER_SLOTS,))],
        compiler_params=pltpu.CompilerParams(
            dimension_semantics=("arbitrary",), vmem_limit_bytes=VMEM_LIMIT),
        name="final",
    )(dest_flat, dest_flat, dest_flat, ys, top_w, res, ln_g, ln_b)


def kernel(x, p, emb_ln_g, emb_ln_b, hgrn_lb, w_in, w_gla_up, b_gla_up, norm_a_g, norm_b_g, w_proj_a, w_proj_b, w_out, ln_mix_g, ln_mix_b, w_router, b_router, w_gate_up, b_gate_up, w_down, b_down, w_ple_gate, w_ple_proj, ln_moe_g, ln_moe_b):
    bsz, seq, d = x.shape
    t = bsz * seq
    assert d == D_MODEL and seq % GLA_TILE == 0 and t % 1024 == 0
    assert w_in.shape[0] == DEPTH == 1
    vec = lambda a: a.reshape(1, -1).astype(F32)

    lb0 = jax.nn.softmax(hgrn_lb.astype(F32), axis=0)[0:1]
    w_in0 = w_in[0]
    w_main = jnp.concatenate([w_in0[:, :W_IN_LR], w_in0[:, W_IN_LR + GLA_RANK:]], axis=1).astype(BF16)
    w_lr = jnp.pad(w_in0[:, W_IN_LR:W_IN_LR + GLA_RANK], ((0, 0), (0, LANES - GLA_RANK))).astype(BF16)
    w_up = jnp.pad(w_gla_up[0], ((0, LANES - GLA_RANK), (0, 0))).astype(BF16)
    wr = jnp.pad(w_router[0].astype(F32), ((0, 0), (0, LANES - N_EXPERTS)))
    wr_hi = wr.astype(BF16)
    wr_lo = (wr - wr_hi.astype(F32)).astype(BF16)
    b_r = jnp.pad(b_router[0].astype(F32), (0, LANES - N_EXPERTS)).reshape(1, LANES)

    xt = x.reshape(t, d)
    h0, z, z_lr = _ln_inproj(xt, vec(emb_ln_g), vec(emb_ln_b), w_main, w_lr)
    o = _gla(z, z_lr, lb0, w_up, vec(b_gla_up[0]), vec(norm_a_g[0]), vec(norm_b_g[0]), bsz, seq)
    h1, top_e, top_w = _post_mixer(
        o, z, h0, w_proj_a[0].astype(BF16), w_proj_b[0].astype(BF16), w_out[0].astype(BF16),
        vec(ln_mix_g[0]), vec(ln_mix_b[0]), wr_hi, wr_lo, b_r)

    rank, cnt = _rank(top_e)
    counts = cnt[0, :N_EXPERTS].astype(jnp.int32)
    padded = (counts + MOE_ROWS - 1) // MOE_ROWS * MOE_ROWS
    pend = jnp.cumsum(padded)
    pstart = pend - padded
    n_blocks = t * TOP_K // MOE_ROWS + N_EXPERTS
    n_rows = n_blocks * MOE_ROWS
    n_used = pend[N_EXPERTS - 1:] // MOE_ROWS
    dest = (pstart[top_e[:, :TOP_K]] + rank[:, :TOP_K]).reshape(-1)
    block_pos = jnp.minimum(jnp.arange(n_blocks, dtype=jnp.int32), n_used - 1) * MOE_ROWS
    block_e = jnp.minimum(
        jnp.sum((pend[None, :] <= block_pos[:, None]).astype(jnp.int32), axis=1), N_EXPERTS - 1)

    pad_start = (pstart + counts) // PAD_GROUP * PAD_GROUP
    xs, res = _dispatch(pad_start, (pend - pad_start) // PAD_GROUP, n_used, dest, h1,
                        p[0].reshape(t, PLE_DIM), w_ple_gate[0].astype(BF16), w_ple_proj[0].astype(BF16),
                        n_rows)
    block_ids = jnp.arange(n_blocks, dtype=jnp.int32)
    n_valid = jnp.where(block_ids < n_used,
                        jnp.clip((pstart + counts)[block_e] - block_ids * MOE_ROWS, 0, MOE_ROWS), 0)
    ys = _experts(block_e, n_used, n_valid, xs, w_gate_up[0].astype(F32),
                  b_gate_up[0].reshape(N_EXPERTS, 1, 2 * D_FF).astype(F32),
                  w_down[0].astype(F32), b_down[0].reshape(N_EXPERTS, 1, d).astype(F32))
    out = _final(dest, ys, top_w, res, vec(ln_moe_g[0]), vec(ln_moe_b[0]))
    return out.reshape(bsz, seq, d)
```

```python
import jax
import jax.numpy as jnp
from jax import lax
from jax.experimental import pallas as pl
from jax.experimental.pallas import tpu as pltpu

F32 = jnp.float32
BF16 = jnp.bfloat16

D_MODEL = 1024
A_HEADS, A_DK, A_DV = 8, 128, 128
B_HEADS, B_DK, B_DV = 4, 128, 256
N_HEADS = A_HEADS + B_HEADS
HEAD_DK = 128
A_V = A_HEADS * A_DV
GLA_RANK = 16
GLA_TAU = 16.0
N_EXPERTS = 32
TOP_K = 4
D_FF = 1024
SWIGLU_LIMIT = 7.0
SWIGLU_ALPHA = 1.702
PLE_DIM = 256
LN_EPS = 1e-5
RMS_EPS = 1e-6
DEPTH = 1
DN_ALPHA = (2.0 * DEPTH) ** 0.25

LANES = 128
GLA_CHUNK = 64
GLA_TILE = 256
GLA_SAFE_LOG_DECAY = -60.0
MOE_ROWS = 512
DMA_BATCH = 128
DMA_QUEUES = 2
F32_SUBLANES = 8
PAD_GROUP = F32_SUBLANES
VMEM_LIMIT = 56 * 1024 * 1024

Z_QA, Z_FA, Z_IA, Z_GA = 0, 1024, 2048, 3072
Z_QB, Z_KB, Z_VB, Z_RB = 4096, 4608, 5120, 6144
Z_MIXER = 7168
Z_GATE_A, Z_GATE_B = 7168, 8192
Z_COLS = 9216
W_IN_LR = 7168


def _layer_norm(x, g, b):
    mu = jnp.mean(x, axis=-1, keepdims=True)
    xc = x - mu
    var = jnp.mean(xc * xc, axis=-1, keepdims=True)
    return xc * lax.rsqrt(var + LN_EPS) * g + b


def _sigmoid(x):
    return jax.nn.sigmoid(x)


def _sigmoid_pair(x):
    e = jnp.exp(-jnp.abs(x))
    r = 1.0 / (1.0 + e)
    er = e * r
    pos = x >= 0.0
    return jnp.where(pos, r, er), jnp.where(pos, er, r)


def _dot(a, b):
    return jnp.dot(a, b, preferred_element_type=F32)


def _dot_nt(a, b):
    return lax.dot_general(a, b, (((1,), (1,)), ((), ())), preferred_element_type=F32)


def _dot_tn(a, b):
    return lax.dot_general(a, b, (((0,), (0,)), ((), ())), preferred_element_type=F32)


def _ln_inproj_kernel(x_ref, g_ref, b_ref, w_ref, wlr_ref, z_ref, zlr_ref, hb_ref):
    @pl.when(pl.program_id(1) == 0)
    def _():
        hb = _layer_norm(x_ref[...], g_ref[...], b_ref[...]).astype(BF16)
        hb_ref[...] = hb
        zlr_ref[...] = _dot(hb, wlr_ref[...])

    z_ref[...] = _dot(hb_ref[...], w_ref[...])


def _ln_inproj(x, g, b, w_main, w_lr, tm=2048, tn=768):
    t = x.shape[0]
    tm = min(tm, t)
    assert t % tm == 0
    return pl.pallas_call(
        _ln_inproj_kernel,
        grid=(t // tm, Z_COLS // tn),
        in_specs=[
            pl.BlockSpec((tm, D_MODEL), lambda i, j: (i, 0)),
            pl.BlockSpec((1, D_MODEL), lambda i, j: (0, 0)),
            pl.BlockSpec((1, D_MODEL), lambda i, j: (0, 0)),
            pl.BlockSpec((D_MODEL, tn), lambda i, j: (0, j)),
            pl.BlockSpec((D_MODEL, LANES), lambda i, j: (0, 0)),
        ],
        out_specs=[
            pl.BlockSpec((tm, tn), lambda i, j: (i, j)),
            pl.BlockSpec((tm, LANES), lambda i, j: (i, 0)),
        ],
        out_shape=[
            jax.ShapeDtypeStruct((t, Z_COLS), F32),
            jax.ShapeDtypeStruct((t, LANES), F32),
        ],
        scratch_shapes=[pltpu.VMEM((tm, D_MODEL), BF16)],
        compiler_params=pltpu.CompilerParams(
            dimension_semantics=("parallel", "arbitrary"), vmem_limit_bytes=VMEM_LIMIT),
        name="ln_inproj",
    )(x, g, b, w_main, w_lr)


def _chunk_tri(tile, chunk):
    i = jnp.arange(tile)[:, None]
    j = jnp.arange(tile)[None, :]
    return ((j <= i) & (i // chunk == j // chunk)).astype(BF16)


def _chunk_cumsum(tri, x):
    hi = x.astype(BF16)
    lo = (x - hi.astype(F32)).astype(BF16)
    return _dot(tri, hi) + _dot(tri, lo)


def _gla_prep(z_ref, lr_ref, lb_ref, wup_ref, bup_ref, tri_ref, qs_ref, q1_ref, ks_ref, bs_ref):
    low = None
    tri = tri_ref[...]

    def put(h, q, k, b):
        qs_ref[h] = q
        q1_ref[h] = (q * jnp.exp(b)).astype(BF16)
        ks_ref[h] = k
        bs_ref[h] = b

    for h in range(A_HEADS):
        cols = slice(h * A_DK, (h + 1) * A_DK)
        lb = lb_ref[:, cols]
        sig, sig_neg = _sigmoid_pair(z_ref[:, Z_FA + h * A_DK:Z_FA + (h + 1) * A_DK])
        b = _chunk_cumsum(tri, jnp.log(lb + (1.0 - lb) * sig))
        qz = z_ref[:, Z_QA + h * A_DK:Z_QA + (h + 1) * A_DK]
        put(h, qz * _sigmoid(qz) * (A_DK ** -0.5), (1.0 - lb) * sig_neg, b)
        low = b if low is None else jnp.minimum(low, b)
    u_all = _dot(lr_ref[...].astype(BF16), wup_ref[...]) + bup_ref[...]
    for hb in range(B_HEADS):
        u = u_all[:, hb * B_DK:(hb + 1) * B_DK]
        log_sig = jnp.minimum(u, 0.0) - jnp.log(1.0 + jnp.exp(-jnp.abs(u)))
        b = _chunk_cumsum(tri, log_sig / GLA_TAU)
        put(A_HEADS + hb, z_ref[:, Z_QB + hb * B_DK:Z_QB + (hb + 1) * B_DK] * (B_DK ** -0.5),
            z_ref[:, Z_KB + hb * B_DK:Z_KB + (hb + 1) * B_DK], b)
        low = jnp.minimum(low, b)
    return low


def _product_scores(q1_ref, ks_ref, bs_ref, sc_ref):
    c = GLA_CHUNK
    for h in range(N_HEADS):
        k1 = (ks_ref[h] * jnp.exp(-bs_ref[h])).astype(BF16)
        for ci in range(GLA_TILE // c):
            rows = slice(ci * c, (ci + 1) * c)
            sc_ref[h, ci] = _dot_nt(q1_ref[h, rows, :], k1[rows])


def _exact_scores(qs_ref, ks_ref, bs_ref, sc_ref):
    c = GLA_CHUNK
    n = GLA_TILE // c
    col = lax.broadcasted_iota(jnp.int32, (c, c), 1)

    def per_chunk(idx, carry):
        h = idx // n
        ci = idx - h * n
        r0 = pl.multiple_of(ci * c, c)
        qc = qs_ref[h, pl.ds(r0, c), :]
        bc = bs_ref[h, pl.ds(r0, c), :]

        def column(j, acc):
            kj = ks_ref[h, pl.ds(r0 + j, 1), :]
            bj = bs_ref[h, pl.ds(r0 + j, 1), :]
            term = qc * kj * jnp.exp(jnp.minimum(bc - bj, 0.0))
            return jnp.where(col == j, jnp.sum(term, axis=-1, keepdims=True), acc)

        sc_ref[h, ci] = lax.fori_loop(0, c, column, jnp.zeros((c, c), F32))
        return carry

    lax.fori_loop(0, N_HEADS * n, per_chunk, 0)


def _gla_outputs(z_ref, q1_ref, ks_ref, bs_ref, sc_ref, sta_ref, stb_ref, nga_ref, ngb_ref, o_ref):
    c = GLA_CHUNK
    causal = lax.broadcasted_iota(jnp.int32, (c, c), 0) >= lax.broadcasted_iota(jnp.int32, (c, c), 1)
    for ci in range(GLA_TILE // c):
        rows = slice(ci * c, (ci + 1) * c)
        for h in range(N_HEADS):
            if h < A_HEADS:
                dv, st_ref, norm_g = A_DV, sta_ref.at[h], nga_ref[...]
                v_off, g_off, o_off = Z_IA + h * A_DV, Z_GA + h * A_DV, h * A_DV
            else:
                hb = h - A_HEADS
                dv, st_ref, norm_g = B_DV, stb_ref.at[hb], ngb_ref[...]
                v_off, g_off, o_off = Z_VB + hb * B_DV, Z_RB + hb * B_DV, A_V + hb * B_DV
            st = st_ref[...]
            k = ks_ref[h, rows, :]
            bc = bs_ref[h, rows, :]
            b_last = bc[c - 1:c]
            k_dec = (k * jnp.exp(b_last - bc)).astype(BF16)
            vb = z_ref[rows, v_off:v_off + dv].astype(BF16)
            scores = jnp.where(causal, sc_ref[h, ci], 0.0).astype(BF16)
            o = _dot_nt(q1_ref[h, rows, :], st.astype(BF16)) + _dot(scores, vb)
            st_ref[...] = st * jnp.exp(b_last) + _dot_tn(vb, k_dec)
            on = o * lax.rsqrt(jnp.mean(o * o, axis=-1, keepdims=True) + RMS_EPS) * norm_g
            gc = z_ref[rows, g_off:g_off + dv]
            o_ref[rows, o_off:o_off + dv] = (on * (gc * _sigmoid(gc))).astype(o_ref.dtype)


def _gla_kernel(z_ref, lr_ref, lb_ref, wup_ref, bup_ref, tri_ref, nga_ref, ngb_ref, o_ref,
                sta_ref, stb_ref, qs_ref, q1_ref, ks_ref, bs_ref, sc_ref):
    @pl.when(pl.program_id(1) == 0)
    def _():
        sta_ref[...] = jnp.zeros_like(sta_ref)
        stb_ref[...] = jnp.zeros_like(stb_ref)

    low = _gla_prep(z_ref, lr_ref, lb_ref, wup_ref, bup_ref, tri_ref, qs_ref, q1_ref, ks_ref, bs_ref)
    safe = jnp.min(low) >= GLA_SAFE_LOG_DECAY

    @pl.when(safe)
    def _product_form():
        _product_scores(q1_ref, ks_ref, bs_ref, sc_ref)

    @pl.when(jnp.logical_not(safe))
    def _exact_form():
        _exact_scores(qs_ref, ks_ref, bs_ref, sc_ref)

    _gla_outputs(z_ref, q1_ref, ks_ref, bs_ref, sc_ref, sta_ref, stb_ref, nga_ref, ngb_ref, o_ref)


def _gla(z, z_lr, lb, w_up, b_up, norm_a, norm_b, bsz, seq):
    ts = GLA_TILE
    nt = seq // ts
    n_chunks = ts // GLA_CHUNK
    row = lambda b, t: (b * nt + t, 0)
    const = lambda b, t: (0, 0)
    per_head = pltpu.VMEM((N_HEADS, ts, HEAD_DK), F32)
    return pl.pallas_call(
        _gla_kernel,
        grid=(bsz, nt),
        in_specs=[
            pl.BlockSpec((ts, Z_MIXER), row),
            pl.BlockSpec((ts, LANES), row),
            pl.BlockSpec((1, A_HEADS * A_DK), const),
            pl.BlockSpec((LANES, B_HEADS * B_DK), const),
            pl.BlockSpec((1, B_HEADS * B_DK), const),
            pl.BlockSpec((ts, ts), const),
            pl.BlockSpec((1, A_DV), const),
            pl.BlockSpec((1, B_DV), const),
        ],
        out_specs=pl.BlockSpec((ts, 2 * D_MODEL), row),
        out_shape=jax.ShapeDtypeStruct((bsz * seq, 2 * D_MODEL), BF16),
        scratch_shapes=[
            pltpu.VMEM((A_HEADS, A_DV, A_DK), F32),
            pltpu.VMEM((B_HEADS, B_DV, B_DK), F32),
            per_head, pltpu.VMEM((N_HEADS, ts, HEAD_DK), BF16), per_head, per_head,
            pltpu.VMEM((N_HEADS, n_chunks, GLA_CHUNK, GLA_CHUNK), F32),
        ],
        compiler_params=pltpu.CompilerParams(
            dimension_semantics=("parallel", "arbitrary"), vmem_limit_bytes=VMEM_LIMIT),
        name="gla",
    )(z, z_lr, lb, w_up, b_up, _chunk_tri(ts, GLA_CHUNK), norm_a, norm_b)


def _lane_pack(cols, dtype):
    m = cols[0].shape[0]
    lane = lax.broadcasted_iota(jnp.int32, (m, LANES), 1)
    out = jnp.zeros((m, LANES), dtype)
    for k, cvals in enumerate(cols):
        out = jnp.where(lane == k, cvals.astype(dtype), out)
    return out


def _post_mixer_kernel(o_ref, ga_ref, gb_ref, x_ref, eg_ref, eb_ref, wa_ref, wb_ref, wo_ref, lg_ref, lb_ref,
                       wrh_ref, wrl_ref, br_ref, h1_ref, te_ref, tw_ref):
    ya = _dot(o_ref[:, :D_MODEL], wa_ref[...])
    yb = _dot(o_ref[:, D_MODEL:], wb_ref[...])
    merged = _sigmoid(ga_ref[...]) * ya + _sigmoid(gb_ref[...]) * yb
    y = _dot(merged.astype(BF16), wo_ref[...])
    h0 = _layer_norm(x_ref[...], eg_ref[...], eb_ref[...])
    h1 = _layer_norm(DN_ALPHA * h0 + y, lg_ref[...], lb_ref[...])
    h1_ref[...] = h1

    hh = h1.astype(BF16)
    hl = (h1 - hh.astype(F32)).astype(BF16)
    logits = _dot(hh, wrh_ref[...]) + (_dot(hl, wrh_ref[...]) + _dot(hh, wrl_ref[...])) + br_ref[...]
    lane = lax.broadcasted_iota(jnp.int32, logits.shape, 1)
    work = jnp.where(lane < N_EXPERTS, logits, -jnp.inf)
    vals, idxs = [], []
    for _ in range(TOP_K):
        m = jnp.max(work, axis=-1, keepdims=True)
        idx = jnp.min(jnp.where(work == m, lane, LANES), axis=-1, keepdims=True)
        vals.append(m)
        idxs.append(idx)
        work = jnp.where(lane == idx, -jnp.inf, work)
    exps = [jnp.exp(v - vals[0]) for v in vals]
    denom = exps[0] + exps[1] + exps[2] + exps[3]
    te_ref[...] = _lane_pack(idxs, jnp.int32)
    tw_ref[...] = _lane_pack([e / denom for e in exps], F32)


def _post_mixer(o, z, x, emb_g, emb_b, w_a, w_b, w_o, ln_g, ln_b, wr_hi, wr_lo, b_r, tm=512):
    t = x.shape[0]
    row = lambda i: (i, 0)
    const = lambda i: (0, 0)
    full_w = pl.BlockSpec((D_MODEL, D_MODEL), const)
    vec = pl.BlockSpec((1, D_MODEL), const)
    return pl.pallas_call(
        _post_mixer_kernel,
        grid=(t // tm,),
        in_specs=[
            pl.BlockSpec((tm, 2 * D_MODEL), row),
            pl.BlockSpec((tm, D_MODEL), lambda i: (i, Z_GATE_A // D_MODEL)),
            pl.BlockSpec((tm, D_MODEL), lambda i: (i, Z_GATE_B // D_MODEL)),
            pl.BlockSpec((tm, D_MODEL), row),
            vec, vec, full_w, full_w, full_w, vec, vec,
            pl.BlockSpec((D_MODEL, LANES), const),
            pl.BlockSpec((D_MODEL, LANES), const),
            pl.BlockSpec((1, LANES), const),
        ],
        out_specs=[
            pl.BlockSpec((tm, D_MODEL), row),
            pl.BlockSpec((tm, LANES), row),
            pl.BlockSpec((tm, LANES), row),
        ],
        out_shape=[
            jax.ShapeDtypeStruct((t, D_MODEL), F32),
            jax.ShapeDtypeStruct((t, LANES), jnp.int32),
            jax.ShapeDtypeStruct((t, LANES), F32),
        ],
        compiler_params=pltpu.CompilerParams(
            dimension_semantics=("parallel",), vmem_limit_bytes=VMEM_LIMIT),
        name="post_mixer",
    )(o, z, z, x, emb_g, emb_b, w_a, w_b, w_o, ln_g, ln_b, wr_hi, wr_lo, b_r)


def _rank_kernel(te_ref, rank_ref, cnt_ref, carry_ref):
    @pl.when(pl.program_id(0) == 0)
    def _():
        carry_ref[...] = jnp.zeros_like(carry_ref)

    te = te_ref[...]
    tr = te.shape[0]
    lane = lax.broadcasted_iota(jnp.int32, (tr, LANES), 1)
    onehots = [te[:, k:k + 1] == lane for k in range(TOP_K)]
    cnt = jnp.zeros((tr, LANES), F32)
    for oh in onehots:
        cnt = cnt + oh.astype(F32)
    strict_lower = (lax.broadcasted_iota(jnp.int32, (tr, tr), 0)
                    > lax.broadcasted_iota(jnp.int32, (tr, tr), 1))
    earlier = _dot(strict_lower.astype(BF16), cnt.astype(BF16)) + carry_ref[...]
    ranks = [jnp.sum(jnp.where(oh, earlier, 0.0), axis=-1, keepdims=True) for oh in onehots]
    rank_ref[...] = _lane_pack(ranks, jnp.int32)
    carry_ref[...] = carry_ref[...] + jnp.sum(cnt, axis=0, keepdims=True)
    cnt_ref[...] = carry_ref[...]


def _rank(top_e, tr=512):
    t = top_e.shape[0]
    return pl.pallas_call(
        _rank_kernel,
        grid=(t // tr,),
        in_specs=[pl.BlockSpec((tr, LANES), lambda i: (i, 0))],
        out_specs=[
            pl.BlockSpec((tr, LANES), lambda i: (i, 0)),
            pl.BlockSpec((1, LANES), lambda i: (0, 0)),
        ],
        out_shape=[
            jax.ShapeDtypeStruct((t, LANES), jnp.int32),
            jax.ShapeDtypeStruct((1, LANES), F32),
        ],
        scratch_shapes=[pltpu.VMEM((1, LANES), F32)],
        compiler_params=pltpu.CompilerParams(dimension_semantics=("arbitrary",)),
        name="rank",
    )(top_e)


def _row(ref, r):
    return ref.at[pl.ds(r, 1), :]


def _wait_rows(ref_hbm, n, sem):
    pltpu.make_async_copy(ref_hbm.at[pl.ds(0, n), :], ref_hbm.at[pl.ds(0, n), :], sem).wait()


def _dispatch_kernel(pad_start_ref, pad_cnt_ref, nu_ref, dest_ref, h_ref, p_ref, wpg_ref, wpp_ref,
                     xs_hbm, res_ref, zero_ref, sem):
    n_tok = h_ref.shape[0]

    @pl.when(pl.program_id(0) == 0)
    def _zero_rows_without_token():
        zero_ref[...] = jnp.zeros_like(zero_ref)

        def wait_pad_groups(n):
            @pl.when(n > 0)
            def _():
                _wait_rows(xs_hbm, pl.multiple_of(n * PAD_GROUP, PAD_GROUP), sem)

        def per_expert(e, n_prev):
            def start(g, c):
                r0 = pl.multiple_of(pad_start_ref[e] + g * PAD_GROUP, PAD_GROUP)
                pltpu.make_async_copy(zero_ref.at[pl.ds(0, PAD_GROUP), :],
                                      xs_hbm.at[pl.ds(r0, PAD_GROUP), :], sem).start()
                return c

            lax.fori_loop(0, pad_cnt_ref[e], start, 0)
            wait_pad_groups(n_prev)
            return pad_cnt_ref[e]

        wait_pad_groups(lax.fori_loop(0, N_EXPERTS, per_expert, 0))

        def block_copy(blk):
            r0 = pl.multiple_of(blk * MOE_ROWS, MOE_ROWS)
            return pltpu.make_async_copy(zero_ref, xs_hbm.at[pl.ds(r0, MOE_ROWS), :], sem)

        def start_block(blk, c):
            block_copy(blk).start()
            return c

        def wait_block(blk, c):
            block_copy(blk).wait()
            return c

        n_blocks = xs_hbm.shape[0] // MOE_ROWS
        lax.fori_loop(nu_ref[0], n_blocks, start_block, 0)
        lax.fori_loop(nu_ref[0], n_blocks, wait_block, 0)

    for r in range(n_tok):
        for k in range(TOP_K):
            pltpu.make_async_copy(_row(h_ref, r), _row(xs_hbm, dest_ref[r * TOP_K + k]), sem).start(
                priority=k % DMA_QUEUES)
    h1 = h_ref[...]
    ple = _sigmoid(_dot(h1.astype(BF16), wpg_ref[...])) * _dot(p_ref[...].astype(BF16), wpp_ref[...])
    res_ref[...] = DN_ALPHA * h1 + ple
    for _ in range(n_tok * TOP_K // DMA_BATCH):
        _wait_rows(xs_hbm, DMA_BATCH, sem)


def _dispatch(pad_start, pad_cnt, n_used, dest_flat, h1, p, w_pg, w_pp, n_rows, tok_per_step=256):
    t = h1.shape[0]
    row = lambda i, ps, pc, nu: (i, 0)
    const = lambda i, ps, pc, nu: (0, 0)
    grid_spec = pltpu.PrefetchScalarGridSpec(
        num_scalar_prefetch=3,
        grid=(t // tok_per_step,),
        in_specs=[
            pl.BlockSpec((tok_per_step * TOP_K,), lambda i, ps, pc, nu: (i,), memory_space=pltpu.SMEM),
            pl.BlockSpec((tok_per_step, D_MODEL), row),
            pl.BlockSpec((tok_per_step, PLE_DIM), row),
            pl.BlockSpec((D_MODEL, D_MODEL), const),
            pl.BlockSpec((PLE_DIM, D_MODEL), const),
        ],
        out_specs=[
            pl.BlockSpec(memory_space=pl.ANY),
            pl.BlockSpec((tok_per_step, D_MODEL), row),
        ],
        scratch_shapes=[pltpu.VMEM((MOE_ROWS, D_MODEL), F32), pltpu.SemaphoreType.DMA(())],
    )
    return pl.pallas_call(
        _dispatch_kernel,
        grid_spec=grid_spec,
        out_shape=[
            jax.ShapeDtypeStruct((n_rows, D_MODEL), F32),
            jax.ShapeDtypeStruct((t, D_MODEL), F32),
        ],
        compiler_params=pltpu.CompilerParams(
            dimension_semantics=("arbitrary",), vmem_limit_bytes=VMEM_LIMIT),
        name="dispatch",
    )(pad_start, pad_cnt, n_used, dest_flat, h1, p, w_pg, w_pp)


def _experts_kernel(be_ref, nu_ref, x_ref, wgu_ref, bgu_ref, wd_ref, bd_ref, y_ref, wgu_bf_ref, wd_bf_ref):
    i = pl.program_id(0)
    used = i < nu_ref[0]

    @pl.when((i == 0) | (be_ref[i] != be_ref[jnp.maximum(i - 1, 0)]))
    def _():
        wgu_bf_ref[...] = wgu_ref[0].astype(BF16)
        wd_bf_ref[...] = wd_ref[0].astype(BF16)

    @pl.when(used)
    def _():
        hgu = _dot(x_ref[...].astype(BF16), wgu_bf_ref[...]) + bgu_ref[0]
        gate = jnp.minimum(hgu[:, :D_FF], SWIGLU_LIMIT)
        up = jnp.clip(hgu[:, D_FF:], -SWIGLU_LIMIT, SWIGLU_LIMIT)
        act = (up + 1.0) * gate * _sigmoid(SWIGLU_ALPHA * gate)
        y_ref[...] = _dot(act.astype(BF16), wd_bf_ref[...]) + bd_ref[0]

    @pl.when(jnp.logical_not(used))
    def _():
        y_ref[...] = jnp.zeros_like(y_ref)


def _experts(block_e, n_used, xs, w_gu, b_gu, w_d, b_d):
    n_rows = xs.shape[0]
    n_blocks = n_rows // MOE_ROWS
    grid_spec = pltpu.PrefetchScalarGridSpec(
        num_scalar_prefetch=2,
        grid=(n_blocks,),
        in_specs=[
            pl.BlockSpec((MOE_ROWS, D_MODEL), lambda i, be, nu: (jnp.minimum(i, nu[0] - 1), 0)),
            pl.BlockSpec((1, D_MODEL, 2 * D_FF), lambda i, be, nu: (be[i], 0, 0)),
            pl.BlockSpec((1, 1, 2 * D_FF), lambda i, be, nu: (be[i], 0, 0)),
            pl.BlockSpec((1, D_FF, D_MODEL), lambda i, be, nu: (be[i], 0, 0)),
            pl.BlockSpec((1, 1, D_MODEL), lambda i, be, nu: (be[i], 0, 0)),
        ],
        out_specs=pl.BlockSpec((MOE_ROWS, D_MODEL), lambda i, be, nu: (i, 0)),
        scratch_shapes=[pltpu.VMEM((D_MODEL, 2 * D_FF), BF16), pltpu.VMEM((D_FF, D_MODEL), BF16)],
    )
    return pl.pallas_call(
        _experts_kernel,
        grid_spec=grid_spec,
        out_shape=jax.ShapeDtypeStruct((n_rows, D_MODEL), F32),
        compiler_params=pltpu.CompilerParams(
            dimension_semantics=("arbitrary",), vmem_limit_bytes=VMEM_LIMIT),
        name="experts",
    )(block_e, n_used, xs, w_gu, b_gu, w_d, b_d)


GATHER_SLOTS = 3


def _final_kernel(dest0_ref, dest1_ref, dest2_ref, ys_hbm, tw_ref, res_ref, lg_ref, lb_ref, o_ref, g_ref, sem):
    tm = res_ref.shape[0]
    i = pl.program_id(0)
    n = pl.num_programs(0)
    slot = i % GATHER_SLOTS

    def gather(idx_ref, s):
        for r in range(tm):
            for k in range(TOP_K):
                pltpu.make_async_copy(_row(ys_hbm, idx_ref[r * TOP_K + k]), _row(g_ref.at[s, k], r),
                                      sem.at[s]).start(priority=k % DMA_QUEUES)

    def combine():
        tw = tw_ref[...]
        acc = res_ref[...]
        for k in range(TOP_K):
            acc = acc + tw[:, k:k + 1] * g_ref[slot, k]
        o_ref[...] = _layer_norm(acc, lg_ref[...], lb_ref[...])

    @pl.when(i == 0)
    def _():
        gather(dest0_ref, 0)

        @pl.when(n > 1)
        def _():
            gather(dest1_ref, 1)

    for _ in range(tm * TOP_K // DMA_BATCH):
        _wait_rows(ys_hbm, DMA_BATCH, sem.at[slot])

    @pl.when(i + 2 < n)
    def _():
        combine()
        gather(dest2_ref, (i + 2) % GATHER_SLOTS)

    @pl.when(i + 2 >= n)
    def _():
        combine()


def _final(dest_flat, ys, top_w, res, ln_g, ln_b, tm=256):
    t = res.shape[0]
    n_steps = t // tm
    row = lambda i: (i, 0)
    const = lambda i: (0, 0)

    def dest_tile(ahead):
        return pl.BlockSpec((tm * TOP_K,), lambda i: (jnp.minimum(i + ahead, n_steps - 1),),
                            memory_space=pltpu.SMEM)

    return pl.pallas_call(
        _final_kernel,
        grid=(n_steps,),
        in_specs=[
            dest_tile(0), dest_tile(1), dest_tile(2),
            pl.BlockSpec(memory_space=pl.ANY),
            pl.BlockSpec((tm, LANES), row),
            pl.BlockSpec((tm, D_MODEL), row),
            pl.BlockSpec((1, D_MODEL), const),
            pl.BlockSpec((1, D_MODEL), const),
        ],
        out_specs=pl.BlockSpec((tm, D_MODEL), row),
        out_shape=jax.ShapeDtypeStruct((t, D_MODEL), F32),
        scratch_shapes=[pltpu.VMEM((GATHER_SLOTS, TOP_K, tm, D_MODEL), F32),
                        pltpu.SemaphoreType.DMA((GATHER_SLOTS,))],
        compiler_params=pltpu.CompilerParams(
            dimension_semantics=("arbitrary",), vmem_limit_bytes=VMEM_LIMIT),
        name="final",
    )(dest_flat, dest_flat, dest_flat, ys, top_w, res, ln_g, ln_b)


def kernel(x, p, emb_ln_g, emb_ln_b, hgrn_lb, w_in, w_gla_up, b_gla_up, norm_a_g, norm_b_g, w_proj_a, w_proj_b, w_out, ln_mix_g, ln_mix_b, w_router, b_router, w_gate_up, b_gate_up, w_down, b_down, w_ple_gate, w_ple_proj, ln_moe_g, ln_moe_b):
    bsz, seq, d = x.shape
    t = bsz * seq
    assert d == D_MODEL and seq % GLA_TILE == 0 and t % 1024 == 0
    assert w_in.shape[0] == DEPTH == 1
    vec = lambda a: a.reshape(1, -1).astype(F32)

    lb0 = jax.nn.softmax(hgrn_lb.astype(F32), axis=0)[0:1]
    w_in0 = w_in[0]
    w_main = jnp.concatenate([w_in0[:, :W_IN_LR], w_in0[:, W_IN_LR + GLA_RANK:]], axis=1).astype(BF16)
    w_lr = jnp.pad(w_in0[:, W_IN_LR:W_IN_LR + GLA_RANK], ((0, 0), (0, LANES - GLA_RANK))).astype(BF16)
    w_up = jnp.pad(w_gla_up[0], ((0, LANES - GLA_RANK), (0, 0))).astype(BF16)
    wr = jnp.pad(w_router[0].astype(F32), ((0, 0), (0, LANES - N_EXPERTS)))
    wr_hi = wr.astype(BF16)
    wr_lo = (wr - wr_hi.astype(F32)).astype(BF16)
    b_r = jnp.pad(b_router[0].astype(F32), (0, LANES - N_EXPERTS)).reshape(1, LANES)

    xt = x.reshape(t, d)
    z, z_lr = _ln_inproj(xt, vec(emb_ln_g), vec(emb_ln_b), w_main, w_lr)
    o = _gla(z, z_lr, lb0, w_up, vec(b_gla_up[0]), vec(norm_a_g[0]), vec(norm_b_g[0]), bsz, seq)
    h1, top_e, top_w = _post_mixer(
        o, z, xt, vec(emb_ln_g), vec(emb_ln_b),
        w_proj_a[0].astype(BF16), w_proj_b[0].astype(BF16), w_out[0].astype(BF16),
        vec(ln_mix_g[0]), vec(ln_mix_b[0]), wr_hi, wr_lo, b_r)

    rank, cnt = _rank(top_e)
    counts = cnt[0, :N_EXPERTS].astype(jnp.int32)
    padded = (counts + MOE_ROWS - 1) // MOE_ROWS * MOE_ROWS
    pend = jnp.cumsum(padded)
    pstart = pend - padded
    n_blocks = t * TOP_K // MOE_ROWS + N_EXPERTS
    n_rows = n_blocks * MOE_ROWS
    n_used = pend[N_EXPERTS - 1:] // MOE_ROWS
    dest = (pstart[top_e[:, :TOP_K]] + rank[:, :TOP_K]).reshape(-1)
    block_pos = jnp.minimum(jnp.arange(n_blocks, dtype=jnp.int32), n_used - 1) * MOE_ROWS
    block_e = jnp.minimum(
        jnp.sum((pend[None, :] <= block_pos[:, None]).astype(jnp.int32), axis=1), N_EXPERTS - 1)

    pad_start = (pstart + counts) // PAD_GROUP * PAD_GROUP
    xs, res = _dispatch(pad_start, (pend - pad_start) // PAD_GROUP, n_used, dest, h1,
                        p[0].reshape(t, PLE_DIM), w_ple_gate[0].astype(BF16), w_ple_proj[0].astype(BF16),
                        n_rows)
    ys = _experts(block_e, n_used, xs, w_gate_up[0].astype(F32),
                  b_gate_up[0].reshape(N_EXPERTS, 1, 2 * D_FF).astype(F32),
                  w_down[0].astype(F32), b_down[0].reshape(N_EXPERTS, 1, d).astype(F32))
    out = _final(dest, ys, top_w, res, vec(ln_moe_g[0]), vec(ln_moe_b[0]))
    return out.reshape(bsz, seq, d)
```
